```python
import math
import jax
import jax.numpy as jnp
from jax import lax
import numpy as np

D_MODEL = 1024
BATCH = 2
SEQ = 16384
DEPTH = 4

N_MEM = 256
BRANCH_W = D_MODEL // 2
N_BRANCH = 3
MIX_W = N_BRANCH * BRANCH_W

FOX_HD = 64
FOX_HEADS = BRANCH_W // FOX_HD
FOX_W = FOX_HEADS * FOX_HD
Q_BLOCK = 128
FOX_FBIAS_OFFSET = 3.0

S5_W = BRANCH_W
S5_GROUP = 16
S5_GROUPS = S5_W // S5_GROUP
S5_STATE = 64

HG_DK = 128
HG_DV = 128
HG_HEADS = BRANCH_W // HG_DV
HG_W = HG_HEADS * HG_DV
HG_CHUNK = 64

X_HEADS = 4
X_HD = D_MODEL // X_HEADS

D_FF = ((8 * D_MODEL + 3 * 256 - 1) // (3 * 256)) * 256

IN_SIZES = (FOX_W, FOX_W, FOX_W, FOX_HEADS, S5_W, HG_HEADS * HG_DK, HG_HEADS * HG_DK, HG_W, HG_W, N_BRANCH * D_MODEL)
IN_COLS = sum(IN_SIZES)
EPS = 1e-6

kernel_name = 'hybrid_fox_s5_hgrn2_gated_trunk'


def rmsnorm(x, g):
    xf = x.astype(jnp.float32)
    y = xf * lax.rsqrt(jnp.mean(xf * xf, axis=-1, keepdims=True) + EPS)
    return (y * g.astype(jnp.float32)).astype(x.dtype)


def _split_points(sizes):
    pts, acc = [], 0
    for s in sizes[:-1]:
        acc += s
        pts.append(acc)
    return pts


def fox_attention(q, k, v, log_f):
    bsz, seq, n_h, hd = q.shape
    n_blk = seq // Q_BLOCK
    c = jnp.cumsum(log_f, axis=1)
    c_keys = c.transpose(0, 2, 1)[:, :, None, :]
    q_blocks = q.reshape(bsz, n_blk, Q_BLOCK, n_h, hd).transpose(1, 0, 2, 3, 4)
    c_blocks = c.reshape(bsz, n_blk, Q_BLOCK, n_h).transpose(1, 0, 3, 2)
    k_pos = jnp.arange(seq)
    scale = hd ** -0.5

    def block(args):
        qi, ci, bi = args
        s = jnp.einsum('bqhd,bkhd->bhqk', qi, k, preferred_element_type=jnp.float32) * scale
        s = s + ci[..., None] - c_keys
        q_pos = bi * Q_BLOCK + jnp.arange(Q_BLOCK)
        s = jnp.where(k_pos[None, :] <= q_pos[:, None], s, -jnp.inf)
        p = jax.nn.softmax(s, axis=-1).astype(v.dtype)
        return jnp.einsum('bhqk,bkhd->bqhd', p, v)

    out = lax.map(block, (q_blocks, c_blocks, jnp.arange(n_blk)))
    return out.transpose(1, 0, 2, 3, 4).reshape(bsz, seq, n_h, hd)


def _complex_linear_combine(e1, e2):
    a1r, a1i, b1r, b1i = e1
    a2r, a2i, b2r, b2i = e2
    return (a1r * a2r - a1i * a2i,
            a1r * a2i + a1i * a2r,
            a2r * b1r - a2i * b1i + b2r,
            a2r * b1i + a2i * b1r + b2i)


def s5_mixer(u, a_re, a_im, b_re, b_im, c_re, c_im, d_skip, log_dt, w_glu, b_glu):
    bsz, seq, _ = u.shape
    f32 = jnp.float32
    uf = u.astype(f32)
    ug = uf.reshape(bsz, seq, S5_GROUPS, S5_GROUP)
    dt = jnp.exp(log_dt.astype(f32))[:, None]
    ar, ai = a_re.astype(f32), a_im.astype(f32)
    mag = jnp.exp(dt * ar)
    abar_r, abar_i = mag * jnp.cos(dt * ai), mag * jnp.sin(dt * ai)
    inv_den = 1.0 / (ar * ar + ai * ai)
    nr, ni = abar_r - 1.0, abar_i
    coef_r = (nr * ar + ni * ai) * inv_den
    coef_i = (ni * ar - nr * ai) * inv_den
    br, bi = b_re.astype(f32), b_im.astype(f32)
    bbar_r = coef_r[..., None] * br - coef_i[..., None] * bi
    bbar_i = coef_r[..., None] * bi + coef_i[..., None] * br
    drive_r = jnp.einsum('bsgh,gph->bsgp', ug, bbar_r)
    drive_i = jnp.einsum('bsgh,gph->bsgp', ug, bbar_i)
    a_shape = (1, seq, S5_GROUPS, S5_STATE)
    elems = (jnp.broadcast_to(abar_r, a_shape), jnp.broadcast_to(abar_i, a_shape), drive_r, drive_i)
    _, _, h_r, h_i = lax.associative_scan(_complex_linear_combine, elems, axis=1)
    y = (jnp.einsum('bsgp,ghp->bsgh', h_r, c_re.astype(f32))
         - jnp.einsum('bsgp,ghp->bsgh', h_i, c_im.astype(f32)))
    y = y.reshape(bsz, seq, S5_W) + d_skip.astype(f32) * uf
    z = jax.nn.gelu(y)
    out = z * jax.nn.sigmoid(z @ w_glu.astype(f32) + b_glu.astype(f32))
    return out.astype(u.dtype)


def hgrn2_mixer(q, f_logit, i_in, g_out, lb, o_gain):
    bsz, seq = q.shape[0], q.shape[1]
    f32 = jnp.float32
    n_chunks = seq // HG_CHUNK
    z = f_logit.astype(f32)
    log_f = jnp.logaddexp(jnp.log(lb), jnp.log1p(-lb) + jax.nn.log_sigmoid(z))
    key = (1.0 - lb) * jax.nn.sigmoid(-z)
    qf = jax.nn.silu(q.astype(f32))
    vf = i_in.astype(f32)

    def chunks(t):
        return t.reshape(bsz, n_chunks, HG_CHUNK, HG_HEADS, t.shape[-1]).transpose(1, 0, 3, 2, 4)

    causal = jnp.tril(jnp.ones((HG_CHUNK, HG_CHUNK), dtype=bool))[None, None, :, :, None]

    def step(state, xs):
        qc, kc, vc, lfc = xs
        b = jnp.cumsum(lfc, axis=2)
        o_inter = jnp.einsum('bhtk,bhkv->bhtv', qc * jnp.exp(b), state)
        rel = jnp.where(causal, b[:, :, :, None, :] - b[:, :, None, :, :], -jnp.inf)
        scores = jnp.einsum('bhtsk,bhsk->bhts', qc[:, :, :, None, :] * jnp.exp(rel), kc)
        o_intra = jnp.einsum('bhts,bhsv->bhtv', scores, vc)
        b_last = b[:, :, -1:, :]
        new_state = (jnp.exp(b_last[:, :, 0, :])[..., None] * state
                     + jnp.einsum('bhsk,bhsv->bhkv', kc * jnp.exp(b_last - b), vc))
        return new_state, o_inter + o_intra

    init = jnp.zeros((bsz, HG_HEADS, HG_DK, HG_DV), f32)
    _, o = lax.scan(step, init, (chunks(qf), chunks(key), chunks(vf), chunks(log_f)))
    o = o.transpose(1, 0, 3, 2, 4).reshape(bsz, seq, HG_HEADS, HG_DV)
    o = rmsnorm(o, o_gain) * jax.nn.silu(g_out.astype(f32))
    return o.reshape(bsz, seq, HG_W).astype(q.dtype)


def cross_attention(h, m, wq, wk, wv, wo, gq, gk):
    bsz, seq, _ = h.shape
    n_mem = m.shape[1]
    q = rmsnorm((h @ wq).reshape(bsz, seq, X_HEADS, X_HD), gq)
    k = rmsnorm((m @ wk).reshape(bsz, n_mem, X_HEADS, X_HD), gk)
    v = (m @ wv).reshape(bsz, n_mem, X_HEADS, X_HD)
    s = jnp.einsum('bshd,bmhd->bhsm', q, k, preferred_element_type=jnp.float32) * (X_HD ** -0.5)
    p = jax.nn.softmax(s, axis=-1).astype(v.dtype)
    o = jnp.einsum('bhsm,bmhd->bshd', p, v).reshape(bsz, seq, X_HEADS * X_HD)
    return o @ wo


def swiglu(h, w_gu, w_d):
    a, b = jnp.split(h @ w_gu, 2, axis=-1)
    return (jax.nn.silu(a) * b) @ w_d


def setup_inputs(seed: int = 0) -> dict:
    key = jax.random.key(seed)
    keys = jax.random.split(key, 40)
    counter = [0]
    f32 = jnp.float32

    def nxt():
        k = keys[counter[0]]
        counter[0] += 1
        return k

    def nrm(shape, scale):
        return jax.random.normal(nxt(), shape, f32) * scale

    def gain(shape):
        return 1.0 + nrm(shape, 0.02)

    L, D = DEPTH, D_MODEL
    G, P, GS = S5_GROUPS, S5_STATE, S5_GROUP
    return {
        'x': nrm((BATCH, SEQ, D), 1.0),
        'mem': nrm((BATCH, N_MEM, D), 1.0),
        'norm_mix': gain((L, D)),
        'w_in': nrm((L, D, IN_COLS), D ** -0.5),
        'fox_fbias': FOX_FBIAS_OFFSET + nrm((L, FOX_HEADS), 0.5),
        'fox_qnorm': gain((L, FOX_HD)),
        'fox_knorm': gain((L, FOX_HD)),
        's5_a_re': -0.5 + nrm((L, G, P), 0.01),
        's5_a_im': math.pi * jnp.arange(P, dtype=f32)[None, None, :] + nrm((L, G, P), 0.01),
        's5_b_re': nrm((L, G, P, GS), (2 * GS) ** -0.5),
        's5_b_im': nrm((L, G, P, GS), (2 * GS) ** -0.5),
        's5_c_re': nrm((L, G, GS, P), (2 * P) ** -0.5),
        's5_c_im': nrm((L, G, GS, P), (2 * P) ** -0.5),
        's5_d': nrm((L, S5_W), 1.0),
        's5_log_dt': jax.random.uniform(nxt(), (L, G), f32, math.log(1e-3), math.log(1e-1)),
        's5_w_glu': nrm((L, S5_W, S5_W), S5_W ** -0.5),
        's5_b_glu': nrm((L, S5_W), 0.01),
        'hg_lb': nrm((L, HG_HEADS * HG_DK), 0.1),
        'hg_onorm': gain((L, HG_DV)),
        'w_branch': nrm((L, MIX_W, D), BRANCH_W ** -0.5),
        'w_out': nrm((L, D, D), D ** -0.5),
        'norm_x': gain((L, D)),
        'norm_mem': gain((L, D)),
        'xq': nrm((L, D, D), D ** -0.5),
        'xk': nrm((L, D, D), D ** -0.5),
        'xv': nrm((L, D, D), D ** -0.5),
        'xo': nrm((L, D, D), D ** -0.5),
        'x_qnorm': gain((L, X_HD)),
        'x_knorm': gain((L, X_HD)),
        'norm_ffn': gain((L, D)),
        'w_gate_up': nrm((L, D, 2 * D_FF), D ** -0.5),
        'w_down': nrm((L, D_FF, D), D_FF ** -0.5),
    }


def reference(x, mem, norm_mix, w_in, fox_fbias, fox_qnorm, fox_knorm, s5_a_re, s5_a_im,
              s5_b_re, s5_b_im, s5_c_re, s5_c_im, s5_d, s5_log_dt, s5_w_glu, s5_b_glu,
              hg_lb, hg_onorm, w_branch, w_out, norm_x, norm_mem, xq, xk, xv, xo,
              x_qnorm, x_knorm, norm_ffn, w_gate_up, w_down):
    bsz, seq, _ = x.shape
    split_pts = _split_points(IN_SIZES)
    lb_all = jnp.cumsum(jax.nn.softmax(hg_lb.astype(jnp.float32), axis=0), axis=0)
    lb_all = lb_all - lb_all[0:1]
    for l in range(DEPTH):
        h = rmsnorm(x, norm_mix[l])
        proj = h @ w_in[l]
        fq, fk, fv, ff, su, hq, hf, hi, hg, gl = jnp.split(proj, split_pts, axis=-1)
        q = rmsnorm(fq.reshape(bsz, seq, FOX_HEADS, FOX_HD), fox_qnorm[l])
        k = rmsnorm(fk.reshape(bsz, seq, FOX_HEADS, FOX_HD), fox_knorm[l])
        v = fv.reshape(bsz, seq, FOX_HEADS, FOX_HD)
        log_f = jax.nn.log_sigmoid((ff + fox_fbias[l]).astype(jnp.float32))
        y_fox = fox_attention(q, k, v, log_f).reshape(bsz, seq, FOX_W)
        y_s5 = s5_mixer(su, s5_a_re[l], s5_a_im[l], s5_b_re[l], s5_b_im[l], s5_c_re[l], s5_c_im[l],
                        s5_d[l], s5_log_dt[l], s5_w_glu[l], s5_b_glu[l])
        y_hg = hgrn2_mixer(hq.reshape(bsz, seq, HG_HEADS, HG_DK), hf.reshape(bsz, seq, HG_HEADS, HG_DK),
                           hi.reshape(bsz, seq, HG_HEADS, HG_DV), hg.reshape(bsz, seq, HG_HEADS, HG_DV),
                           lb_all[l].reshape(HG_HEADS, HG_DK), hg_onorm[l])
        gates = jax.nn.sigmoid(gl).reshape(bsz, seq, N_BRANCH, D_MODEL)
        wb = w_branch[l]
        merged = (gates[:, :, 0] * (y_fox @ wb[:BRANCH_W])
                  + gates[:, :, 1] * (y_s5 @ wb[BRANCH_W:2 * BRANCH_W])
                  + gates[:, :, 2] * (y_hg @ wb[2 * BRANCH_W:]))
        x = x + merged @ w_out[l]
        x = x + cross_attention(rmsnorm(x, norm_x[l]), rmsnorm(mem, norm_mem[l]),
                                xq[l], xk[l], xv[l], xo[l], x_qnorm[l], x_knorm[l])
        x = x + swiglu(rmsnorm(x, norm_ffn[l]), w_gate_up[l], w_down[l])
    return x
```

```python
import functools
import math

import jax
import jax.numpy as jnp
from jax import lax
from jax.experimental import pallas as pl
from jax.experimental.pallas import tpu as pltpu

F32 = jnp.float32
BF16 = jnp.bfloat16
HIGHEST = lax.Precision.HIGHEST

D_MODEL = 1024
BRANCH_W = 512
FOX_HD = 64
FOX_HEADS = 8
S5_GROUP = 16
S5_GROUPS = 32
S5_STATE = 64
S5_NSTATE = S5_GROUPS * S5_STATE
HG_HEADS = 4
HG_D = 128
HG_CHUNK = 64
HG_SUB = 16
X_HEADS = 4
X_HD = 256
D_FF = 2816
EPS = 1e-6

VMEM_LIMIT_BYTES = 56 * 1024 * 1024

C_FQ, C_FK, C_FV, C_SU, C_HQ, C_HF, C_HI, C_HG, C_GL, C_FF, C_END = (
    0, 512, 1024, 1536, 2048, 2560, 3072, 3584, 4096, 7168, 7296)

NEG_BIG = -1e30


def _cparams(sem):
    return pltpu.CompilerParams(dimension_semantics=sem, vmem_limit_bytes=VMEM_LIMIT_BYTES)


def _rms(xf, g):
    return xf * lax.rsqrt(jnp.mean(xf * xf, axis=-1, keepdims=True) + EPS) * g


def _sigmoid(x):
    return 1.0 / (1.0 + jnp.exp(-x))


def _log_sigmoid(x):
    return jnp.minimum(x, 0.0) - jnp.log1p(jnp.exp(-jnp.abs(x)))


def _dot(a, b):
    return jnp.dot(a, b, preferred_element_type=F32)


def _dot_nt(a, b):
    return lax.dot_general(a, b, (((1,), (1,)), ((), ())), preferred_element_type=F32)


def _dot_tn(a, b):
    return lax.dot_general(a, b, (((0,), (0,)), ((), ())), preferred_element_type=F32)


def _const_spec(shape):
    nd = len(shape)
    return pl.BlockSpec(shape, lambda *_: (0,) * nd)


def _inproj_kernel(x_ref, g_ref, w_ref, gq_ref, gk_ref, hsum_ref,
                   q_ref, k_ref, v_ref, su_ref, hq_ref, hf_ref, hi_ref, hg_ref, gl_ref, ff_ref):
    h = _rms(x_ref[...], g_ref[...]).astype(BF16)

    def proj(lo, hi):
        return _dot(h, w_ref[:, lo:hi])

    def headnorm(t, g):
        ss = _dot((t * t).astype(BF16), hsum_ref[...])
        return t * lax.rsqrt(ss * (1.0 / FOX_HD) + EPS) * g

    q_ref[...] = headnorm(proj(C_FQ, C_FK), gq_ref[...]).astype(BF16)
    k_ref[...] = headnorm(proj(C_FK, C_FV), gk_ref[...]).astype(BF16)
    v_ref[...] = proj(C_FV, C_SU).astype(BF16)
    su_ref[...] = proj(C_SU, C_HQ).astype(BF16)
    hq_ref[...] = proj(C_HQ, C_HF).astype(BF16)
    hf_ref[...] = proj(C_HF, C_HI)
    hi_ref[...] = proj(C_HI, C_HG).astype(BF16)
    hg_ref[...] = proj(C_HG, C_GL).astype(BF16)
    for c in range(C_GL, C_FF, 512):
        gl_ref[:, c - C_GL:c - C_GL + 512] = proj(c, c + 512).astype(BF16)
    ff_ref[...] = proj(C_FF, C_END)


def _inproj(x, g, w, gq, gk, hsum, tm):
    n = x.shape[0]
    row = lambda c: pl.BlockSpec((tm, c), lambda i: (i, 0))
    outs = [(512, BF16)] * 5 + [(512, F32)] + [(512, BF16)] * 2 + [(3072, BF16), (128, F32)]
    return pl.pallas_call(
        _inproj_kernel,
        grid=(n // tm,),
        in_specs=[row(D_MODEL), _const_spec((1, D_MODEL)), _const_spec((D_MODEL, C_END)),
                  _const_spec((1, 512)), _const_spec((1, 512)), _const_spec((512, 512))],
        out_specs=[row(c) for c, _ in outs],
        out_shape=[jax.ShapeDtypeStruct((n, c), dt) for c, dt in outs],
        compiler_params=_cparams(("parallel",)),
        name="inproj",
    )(x, g, w, gq, gk, hsum)


CUM_W = 256


def _logf_cumsum_kernel(ff_ref, bias_ref, c_ref):
    rows = ff_ref.shape[1]
    per_head = rows // FOX_HEADS
    lf = _log_sigmoid(ff_ref[0] + bias_ref[...])
    r = lax.broadcasted_iota(jnp.int32, (CUM_W, CUM_W), 0)
    c = lax.broadcasted_iota(jnp.int32, (CUM_W, CUM_W), 1)
    upper = (r <= c).astype(F32)
    y = jnp.dot(lf, upper, precision=HIGHEST, preferred_element_type=F32)
    tot = jnp.broadcast_to(y[:, CUM_W - 1:CUM_W], (rows, 128))
    rr = lax.broadcasted_iota(jnp.int32, (rows, rows), 0)
    cc = lax.broadcasted_iota(jnp.int32, (rows, rows), 1)
    prev = ((rr // per_head == cc // per_head) & (cc < rr)).astype(F32)
    offs = jnp.dot(prev, tot, precision=HIGHEST, preferred_element_type=F32)
    c_ref[0] = y + offs[:, 0:1]


def _logf_cumsum(ff_rows, bias_rows):
    b, rows, _ = ff_rows.shape
    return pl.pallas_call(
        _logf_cumsum_kernel,
        grid=(b,),
        in_specs=[pl.BlockSpec((1, rows, CUM_W), lambda i: (i, 0, 0)), _const_spec((rows, 1))],
        out_specs=pl.BlockSpec((1, rows, CUM_W), lambda i: (i, 0, 0)),
        out_shape=jax.ShapeDtypeStruct(ff_rows.shape, F32),
        compiler_params=_cparams(("parallel",)),
        name="fox_logf_cumsum",
    )(ff_rows, bias_rows)


def _fox_kernel(q_ref, k_ref, v_ref, ccol_ref, crow_ref, o_ref, m_ref, l_ref, acc_ref, *, tq):
    i = pl.program_id(2)
    lane = lax.broadcasted_iota(jnp.int32, (tq, 128), 1)
    q = q_ref[...]
    qm = [jnp.where((lane // FOX_HD) == hh, q, jnp.zeros_like(q)) for hh in range(2)]
    ccol = ccol_ref[0, 0]
    cq = [ccol[:, hh:hh + 1] for hh in range(2)]
    m_ref[...] = jnp.full(m_ref.shape, NEG_BIG, F32)
    l_ref[...] = jnp.zeros(l_ref.shape, F32)
    acc_ref[...] = jnp.zeros(acc_ref.shape, F32)

    def step(j, masked):
        start = pl.multiple_of(j * tq, tq)
        kb = k_ref[pl.ds(start, tq), :]
        vb = v_ref[pl.ds(start, tq), :]
        for hh in range(2):
            ck = crow_ref[0, 0, hh:hh + 1, pl.ds(start, tq)]
            s = _dot_nt(qm[hh], kb) + (cq[hh] - ck)
            if masked:
                rq = lax.broadcasted_iota(jnp.int32, (tq, tq), 0)
                ck_i = lax.broadcasted_iota(jnp.int32, (tq, tq), 1)
                s = jnp.where(ck_i <= rq, s, NEG_BIG)
            m_prev = m_ref[hh]
            m_new = jnp.maximum(m_prev, jnp.max(s, axis=-1, keepdims=True))
            alpha = jnp.exp(m_prev - m_new)
            p = jnp.exp(s - m_new)
            l_ref[hh] = alpha * l_ref[hh] + jnp.sum(p, axis=-1, keepdims=True)
            acc_ref[hh] = alpha * acc_ref[hh] + _dot(p.astype(BF16), vb)
            m_ref[hh] = m_new

    def body(j, carry):
        step(j, False)
        return carry

    lax.fori_loop(0, i, body, 0)
    step(i, True)
    o0 = acc_ref[0] / l_ref[0]
    o1 = acc_ref[1] / l_ref[1]
    o_ref[...] = jnp.where(lane < FOX_HD, o0, o1).astype(BF16)


def _fox_attention(q, k, v, ccol, crow, bsz, seq, tq):
    n = q.shape[0]
    nq = seq // tq
    qspec = pl.BlockSpec((tq, 128), lambda b, p, i: (b * nq + i, p))
    kvspec = pl.BlockSpec((seq, 128), lambda b, p, i: (b, p))
    return pl.pallas_call(
        functools.partial(_fox_kernel, tq=tq),
        grid=(bsz, FOX_HEADS // 2, nq),
        in_specs=[qspec, kvspec, kvspec,
                  pl.BlockSpec((1, 1, tq, 2), lambda b, p, i: (b, p, i, 0)),
                  pl.BlockSpec((1, 1, 2, seq), lambda b, p, i: (b, p, 0, 0))],
        out_specs=qspec,
        out_shape=jax.ShapeDtypeStruct((n, 512), BF16),
        scratch_shapes=[pltpu.VMEM((2, tq, 1), F32), pltpu.VMEM((2, tq, 1), F32),
                        pltpu.VMEM((2, tq, 128), F32)],
        compiler_params=_cparams(("parallel", "parallel", "arbitrary")),
        name="fox_attention",
    )(q, k, v, ccol, crow)


S5_HALF_CH = 256
S5_HALF_ST = S5_NSTATE // 2


def _s5_kernel(u_ref, bre_ref, bim_ref, cre_ref, cim_ref, pw_ref, d_ref, wglu_ref, bglu_ref,
               o_ref, hr_ref, hi_ref, cr_ref, ci_ref, *, tt):
    t = pl.program_id(1)

    @pl.when(t == 0)
    def _():
        cr_ref[...] = jnp.zeros(cr_ref.shape, F32)
        ci_ref[...] = jnp.zeros(ci_ref.shape, F32)

    u = u_ref[...]
    for hf in range(2):
        uh = u[:, hf * S5_HALF_CH:(hf + 1) * S5_HALF_CH]
        sl = slice(hf * S5_HALF_ST, (hf + 1) * S5_HALF_ST)
        hr_ref[:, sl] = _dot(uh, bre_ref[hf])
        hi_ref[:, sl] = _dot(uh, bim_ref[hf])

    def scan_block(r, carry):
        cr, ci = carry
        rows = pl.ds(pl.multiple_of(r * 8, 8), 8)
        xr = hr_ref[rows, :]
        xi = hi_ref[rows, :]
        for n, k in enumerate((1, 2, 4)):
            ar = pw_ref[2 * n]
            ai = pw_ref[2 * n + 1]
            sr = pltpu.roll(xr, k, 0)
            si = pltpu.roll(xi, k, 0)
            xr, xi = xr + ar * sr - ai * si, xi + ar * si + ai * sr
        pr = pw_ref[6]
        pi = pw_ref[7]
        xr, xi = xr + pr * cr - pi * ci, xi + pr * ci + pi * cr
        hr_ref[rows, :] = xr
        hi_ref[rows, :] = xi
        return (jnp.broadcast_to(xr[7:8, :], xr.shape), jnp.broadcast_to(xi[7:8, :], xi.shape))

    cr, ci = lax.fori_loop(0, tt // 8, scan_block, (cr_ref[...], ci_ref[...]))
    cr_ref[...] = cr
    ci_ref[...] = ci

    ys = []
    for hf in range(2):
        sl = slice(hf * S5_HALF_ST, (hf + 1) * S5_HALF_ST)
        ys.append(_dot(hr_ref[:, sl].astype(BF16), cre_ref[hf])
                  - _dot(hi_ref[:, sl].astype(BF16), cim_ref[hf]))
    y = jnp.concatenate(ys, axis=1) + d_ref[...] * u.astype(F32)
    z = 0.5 * y * (1.0 + jnp.tanh(math.sqrt(2.0 / math.pi) * (y + 0.044715 * (y * y * y))))
    gate = _sigmoid(_dot(z.astype(BF16), wglu_ref[...]) + bglu_ref[...])
    o_ref[...] = (z * gate).astype(BF16)


def _s5(u, bre, bim, cre, cim, pw, d, wglu, bglu, bsz, seq, tt):
    n = u.shape[0]
    nt = seq // tt
    rowspec = pl.BlockSpec((tt, 512), lambda b, t: (b * nt + t, 0))
    return pl.pallas_call(
        functools.partial(_s5_kernel, tt=tt),
        grid=(bsz, nt),
        in_specs=[rowspec,
                  _const_spec((2, S5_HALF_CH, S5_HALF_ST)), _const_spec((2, S5_HALF_CH, S5_HALF_ST)),
                  _const_spec((2, S5_HALF_ST, S5_HALF_CH)), _const_spec((2, S5_HALF_ST, S5_HALF_CH)),
                  _const_spec((8, 8, S5_NSTATE)), _const_spec((1, 512)),
                  _const_spec((512, 512)), _const_spec((1, 512))],
        out_specs=rowspec,
        out_shape=jax.ShapeDtypeStruct((n, 512), BF16),
        scratch_shapes=[pltpu.VMEM((tt, S5_NSTATE), F32), pltpu.VMEM((tt, S5_NSTATE), F32),
                        pltpu.VMEM((8, S5_NSTATE), F32), pltpu.VMEM((8, S5_NSTATE), F32)],
        compiler_params=_cparams(("parallel", "arbitrary")),
        name="s5_mixer",
    )(u, bre, bim, cre, cim, pw, d, wglu, bglu)


def _s5_params(a_re, a_im, b_re, b_im, c_re, c_im, log_dt):
    g, p, gs = S5_GROUPS, S5_STATE, S5_GROUP
    dt = jnp.exp(log_dt.astype(F32))[:, None]
    ar, ai = a_re.astype(F32), a_im.astype(F32)
    mag = jnp.exp(dt * ar)
    abar_r, abar_i = mag * jnp.cos(dt * ai), mag * jnp.sin(dt * ai)
    inv_den = 1.0 / (ar * ar + ai * ai)
    nr, ni = abar_r - 1.0, abar_i
    coef_r = (nr * ar + ni * ai) * inv_den
    coef_i = (ni * ar - nr * ai) * inv_den
    br, bi = b_re.astype(F32), b_im.astype(F32)
    bbar_r = coef_r[..., None] * br - coef_i[..., None] * bi
    bbar_i = coef_r[..., None] * bi + coef_i[..., None] * br

    hg = g // 2
    eye = jnp.eye(hg, dtype=F32)

    def drive_bd(bb):
        x = bb.reshape(2, hg, p, gs).transpose(0, 1, 3, 2)
        x = x[:, :, :, None, :] * eye[None, :, None, :, None]
        return x.reshape(2, hg * gs, hg * p).astype(BF16)

    def read_bd(cc):
        x = cc.astype(F32).reshape(2, hg, gs, p).transpose(0, 1, 3, 2)
        x = x[:, :, :, None, :] * eye[None, :, None, :, None]
        return x.reshape(2, hg * p, hg * gs).astype(BF16)

    ar1, ai1 = abar_r.reshape(-1), abar_i.reshape(-1)
    pows = [(ar1, ai1)]
    for _ in range(7):
        pr, pi = pows[-1]
        pows.append((pr * ar1 - pi * ai1, pr * ai1 + pi * ar1))
    rows = jnp.arange(8)[:, None]

    def step_tab(k):
        r, i = pows[k - 1]
        mask = (rows >= k).astype(F32)
        return [mask * r[None, :], mask * i[None, :]]

    tabs = step_tab(1) + step_tab(2) + step_tab(4)
    tabs.append(jnp.stack([pows[r][0] for r in range(8)]))
    tabs.append(jnp.stack([pows[r][1] for r in range(8)]))
    pw = jnp.stack(tabs)
    return drive_bd(bbar_r), drive_bd(bbar_i), read_bd(c_re), read_bd(c_im), pw


def _hgrn_kernel(q_ref, f_ref, i_ref, g_ref, loglb_ref, log1mlb_ref, onemlb_ref, gain_ref,
                 o_ref, st_ref, *, tt):
    t = pl.program_id(2)

    @pl.when(t == 0)
    def _():
        st_ref[...] = jnp.zeros(st_ref.shape, F32)

    c_sz, sub = HG_CHUNK, HG_SUB
    n_sub = c_sz // sub
    r64 = lax.broadcasted_iota(jnp.int32, (c_sz, c_sz), 0)
    c64 = lax.broadcasted_iota(jnp.int32, (c_sz, c_sz), 1)
    lower = (c64 <= r64).astype(F32)
    row_s = lax.broadcasted_iota(jnp.int32, (sub, 1), 0)
    lane_s = lax.broadcasted_iota(jnp.int32, (sub, c_sz), 1)
    loglb = loglb_ref[...]
    log1mlb = log1mlb_ref[...]
    onemlb = onemlb_ref[...]
    gain = gain_ref[...]

    def chunk(c, carry):
        rows = pl.ds(pl.multiple_of(c * c_sz, c_sz), c_sz)
        z = f_ref[rows, :]
        bb = log1mlb + _log_sigmoid(z)
        logf = jnp.maximum(loglb, bb) + jnp.log1p(jnp.exp(-jnp.abs(loglb - bb)))
        key = onemlb * (1.0 / (1.0 + jnp.exp(z)))
        qx = q_ref[rows, :].astype(F32)
        qf = qx * _sigmoid(qx)
        vb = i_ref[rows, :]
        b = jnp.dot(lower, logf, precision=HIGHEST, preferred_element_type=F32)
        b_last = b[c_sz - 1:c_sz, :]
        st = st_ref[...]
        o_inter = _dot_nt((qf * jnp.exp(b)).astype(BF16), st.astype(BF16))
        kd = (key * jnp.exp(b_last - b)).astype(BF16)
        st_ref[...] = st * jnp.exp(b_last) + _dot_tn(vb, kd)

        srows = []
        for blk in range(n_sub):
            lo = blk * sub
            b_i = b[lo:lo + sub]
            q_i = qf[lo:lo + sub]
            k_i = key[lo:lo + sub]
            sd = jnp.zeros((sub, c_sz), F32)
            for s in range(sub):
                e = jnp.exp(jnp.minimum(b_i - b_i[s:s + 1], 0.0))
                col = jnp.sum(q_i * e * k_i[s:s + 1], axis=-1, keepdims=True)
                col = jnp.where(row_s >= s, col, 0.0)
                sd = jnp.where(lane_s == lo + s, col, sd)
            if blk > 0:
                ref = b[lo - 1:lo]
                qt = (q_i * jnp.exp(b_i - ref)).astype(BF16)
                kt = (key * jnp.exp(jnp.minimum(ref - b, 0.0))).astype(BF16)
                sd = jnp.where(lane_s < lo, _dot_nt(qt, kt), sd)
            srows.append(sd)
        scores = jnp.concatenate(srows, axis=0)
        o = o_inter + _dot(scores.astype(BF16), vb)
        gx = g_ref[rows, :].astype(F32)
        y = o * lax.rsqrt(jnp.mean(o * o, axis=-1, keepdims=True) + EPS) * gain
        o_ref[rows, :] = (y * (gx * _sigmoid(gx))).astype(BF16)
        return carry

    lax.fori_loop(0, tt // c_sz, chunk, 0)


def _hgrn(hq, hf, hi, hg, loglb, log1mlb, onemlb, gain, bsz, seq, tt):
    n = hq.shape[0]
    nt = seq // tt
    spec = pl.BlockSpec((tt, HG_D), lambda b, h, t: (b * nt + t, h))
    vec = pl.BlockSpec((1, HG_D), lambda b, h, t: (0, h))
    return pl.pallas_call(
        functools.partial(_hgrn_kernel, tt=tt),
        grid=(bsz, HG_HEADS, nt),
        in_specs=[spec, spec, spec, spec, vec, vec, vec, _const_spec((1, HG_D))],
        out_specs=spec,
        out_shape=jax.ShapeDtypeStruct((n, 512), BF16),
        scratch_shapes=[pltpu.VMEM((HG_D, HG_D), F32)],
        compiler_params=_cparams(("parallel", "parallel", "arbitrary")),
        name="hgrn2_mixer",
    )(hq, hf, hi, hg, loglb, log1mlb, onemlb, gain)


def _merge_kernel(x_ref, yf_ref, ys_ref, yh_ref, gl_ref, wb_ref, wo_ref, o_ref):
    m = None
    for n, y_ref in enumerate((yf_ref, ys_ref, yh_ref)):
        gate = _sigmoid(gl_ref[:, n * D_MODEL:(n + 1) * D_MODEL].astype(F32))
        term = gate * _dot(y_ref[...], wb_ref[n * BRANCH_W:(n + 1) * BRANCH_W, :])
        m = term if m is None else m + term
    o_ref[...] = x_ref[...] + _dot(m.astype(BF16), wo_ref[...])


def _merge(x, yf, ys, yh, gl, wb, wo, tm):
    n = x.shape[0]
    row = lambda c: pl.BlockSpec((tm, c), lambda i: (i, 0))
    return pl.pallas_call(
        _merge_kernel,
        grid=(n // tm,),
        in_specs=[row(D_MODEL), row(512), row(512), row(512), row(3 * D_MODEL),
                  _const_spec((3 * BRANCH_W, D_MODEL)), _const_spec((D_MODEL, D_MODEL))],
        out_specs=row(D_MODEL),
        out_shape=jax.ShapeDtypeStruct((n, D_MODEL), F32),
        compiler_params=_cparams(("parallel",)),
        name="merge_outproj",
    )(x, yf, ys, yh, gl, wb, wo)


def _memkv_kernel(m_ref, g_ref, wk_ref, wv_ref, gk_ref, k_ref, v_ref):
    h = _rms(m_ref[0], g_ref[...]).astype(BF16)
    kk = _dot(h, wk_ref[...])
    for hd in range(X_HEADS):
        sl = slice(hd * X_HD, (hd + 1) * X_HD)
        k_ref[0, :, sl] = _rms(kk[:, sl], gk_ref[...]).astype(BF16)
    v_ref[0] = _dot(h, wv_ref[...]).astype(BF16)


def _memkv(mem, g, wk, wv, gk):
    bsz, nm, _ = mem.shape
    spec = pl.BlockSpec((1, nm, D_MODEL), lambda b: (b, 0, 0))
    return pl.pallas_call(
        _memkv_kernel,
        grid=(bsz,),
        in_specs=[spec, _const_spec((1, D_MODEL)), _const_spec((D_MODEL, D_MODEL)),
                  _const_spec((D_MODEL, D_MODEL)), _const_spec((1, X_HD))],
        out_specs=[spec, spec],
        out_shape=[jax.ShapeDtypeStruct(mem.shape, BF16)] * 2,
        compiler_params=_cparams(("parallel",)),
        name="mem_kv",
    )(mem, g, wk, wv, gk)


def _xattn_kernel(x_ref, g_ref, wq_ref, gq_ref, k_ref, v_ref, wo_ref, o_ref):
    x = x_ref[...]
    h = _rms(x, g_ref[...]).astype(BF16)
    q = _dot(h, wq_ref[...])
    outs = []
    for hd in range(X_HEADS):
        sl = slice(hd * X_HD, (hd + 1) * X_HD)
        qh = (_rms(q[:, sl], gq_ref[...]) * (X_HD ** -0.5)).astype(BF16)
        s = _dot_nt(qh, k_ref[0, :, sl])
        p = jnp.exp(s - jnp.max(s, axis=-1, keepdims=True))
        l = jnp.sum(p, axis=-1, keepdims=True)
        outs.append((_dot(p.astype(BF16), v_ref[0, :, sl]) / l).astype(BF16))
    o_ref[...] = x + _dot(jnp.concatenate(outs, axis=1), wo_ref[...])


def _xattn(x, g, wq, gq, km, vm, wo, seq, tm):
    n = x.shape[0]
    nm = km.shape[1]
    per_b = seq // tm
    row = pl.BlockSpec((tm, D_MODEL), lambda i: (i, 0))
    kv = pl.BlockSpec((1, nm, D_MODEL), lambda i: (i // per_b, 0, 0))
    return pl.pallas_call(
        _xattn_kernel,
        grid=(n // tm,),
        in_specs=[row, _const_spec((1, D_MODEL)), _const_spec((D_MODEL, D_MODEL)),
                  _const_spec((1, X_HD)), kv, kv, _const_spec((D_MODEL, D_MODEL))],
        out_specs=row,
        out_shape=jax.ShapeDtypeStruct((n, D_MODEL), F32),
        compiler_params=_cparams(("parallel",)),
        name="cross_attention",
    )(x, g, wq, gq, km, vm, wo)


FF_CHUNK = 256


def _ffn_kernel(x_ref, g_ref, wgu_ref, wd_ref, o_ref, act_ref):
    x = x_ref[...]
    h = _rms(x, g_ref[...]).astype(BF16)
    for c in range(0, D_FF, FF_CHUNK):
        a = _dot(h, wgu_ref[:, c:c + FF_CHUNK])
        b = _dot(h, wgu_ref[:, D_FF + c:D_FF + c + FF_CHUNK])
        act_ref[:, c:c + FF_CHUNK] = (a * _sigmoid(a) * b).astype(BF16)
    o_ref[...] = x + _dot(act_ref[...], wd_ref[...])


def _ffn(x, g, wgu, wd, tm):
    n = x.shape[0]
    row = pl.BlockSpec((tm, D_MODEL), lambda i: (i, 0))
    return pl.pallas_call(
        _ffn_kernel,
        grid=(n // tm,),
        in_specs=[row, _const_spec((1, D_MODEL)), _const_spec((D_MODEL, 2 * D_FF)),
                  _const_spec((D_FF, D_MODEL))],
        out_specs=row,
        out_shape=jax.ShapeDtypeStruct((n, D_MODEL), F32),
        scratch_shapes=[pltpu.VMEM((tm, D_FF), BF16)],
        compiler_params=_cparams(("parallel",)),
        name="swiglu",
    )(x, g, wgu, wd)


def _reorder_w_in(w_in):
    depth = w_in.shape[0]
    a = w_in[:, :, :1536]
    ff = w_in[:, :, 1536:1544]
    rest = w_in[:, :, 1544:]
    pad = jnp.zeros((depth, D_MODEL, C_END - C_FF - FOX_HEADS), w_in.dtype)
    return jnp.concatenate([a, rest, ff, pad], axis=-1).astype(BF16)


def kernel(x, mem, norm_mix, w_in, fox_fbias, fox_qnorm, fox_knorm, s5_a_re, s5_a_im, s5_b_re, s5_b_im, s5_c_re, s5_c_im, s5_d, s5_log_dt, s5_w_glu, s5_b_glu, hg_lb, hg_onorm, w_branch, w_out, norm_x, norm_mem, xq, xk, xv, xo, x_qnorm, x_knorm, norm_ffn, w_gate_up, w_down):
    bsz, seq, _ = x.shape
    depth = w_in.shape[0]
    n = bsz * seq
    tm = min(256, seq)
    tq = min(512, seq)
    tt = min(256, seq)
    assert seq % CUM_W == 0 and seq % tq == 0 and seq % tt == 0 and tt % HG_CHUNK == 0

    row = lambda v: v.astype(F32).reshape(1, -1)
    w_in_r = _reorder_w_in(w_in)
    s5_wglu = s5_w_glu.astype(BF16)
    wb, wo = w_branch.astype(BF16), w_out.astype(BF16)
    wq, wk, wv, wxo = xq.astype(BF16), xk.astype(BF16), xv.astype(BF16), xo.astype(BF16)
    wgu, wd = w_gate_up.astype(BF16), w_down.astype(BF16)

    lb_all = jnp.cumsum(jax.nn.softmax(hg_lb.astype(F32), axis=0), axis=0)
    lb_all = lb_all - lb_all[0:1]

    lane = jnp.arange(512)
    hsum = (lane[:, None] // FOX_HD == lane[None, :] // FOX_HD).astype(BF16)
    cum_rows = FOX_HEADS * (seq // CUM_W)

    xf = x.astype(F32).reshape(n, D_MODEL)
    for l in range(depth):
        gq = jnp.tile(row(fox_qnorm[l]), (1, FOX_HEADS)) * (FOX_HD ** -0.5)
        gk = jnp.tile(row(fox_knorm[l]), (1, FOX_HEADS))
        q, k, v, su, hq, hf, hi, hg, gl, ff = _inproj(xf, row(norm_mix[l]), w_in_r[l], gq, gk, hsum, tm)

        ff_rows = ff[:, :FOX_HEADS].reshape(bsz, seq, FOX_HEADS).transpose(0, 2, 1).reshape(bsz, cum_rows, CUM_W)
        bias_rows = jnp.repeat(fox_fbias[l].astype(F32), seq // CUM_W).reshape(cum_rows, 1)
        c = _logf_cumsum(ff_rows, bias_rows).reshape(bsz, FOX_HEADS // 2, 2, seq)
        ccol = c.transpose(0, 1, 3, 2)
        y_fox = _fox_attention(q, k, v, ccol, c, bsz, seq, tq)

        bre, bim, cre, cim, pw = _s5_params(s5_a_re[l], s5_a_im[l], s5_b_re[l], s5_b_im[l],
                                            s5_c_re[l], s5_c_im[l], s5_log_dt[l])
        y_s5 = _s5(su, bre, bim, cre, cim, pw, row(s5_d[l]), s5_wglu[l], row(s5_b_glu[l]), bsz, seq, tt)

        lb = lb_all[l].reshape(1, -1)
        y_hg = _hgrn(hq, hf, hi, hg, jnp.log(lb), jnp.log1p(-lb), 1.0 - lb, row(hg_onorm[l]), bsz, seq, tt)

        xf = _merge(xf, y_fox, y_s5, y_hg, gl, wb[l], wo[l], tm)

        km, vm = _memkv(mem.astype(F32), row(norm_mem[l]), wk[l], wv[l], row(x_knorm[l]))
        xf = _xattn(xf, row(norm_x[l]), wq[l], row(x_qnorm[l]), km, vm, wxo[l], seq, tm)
        xf = _ffn(xf, row(norm_ffn[l]), wgu[l], wd[l], tm)
    return xf.reshape(bsz, seq, D_MODEL).astype(x.dtype)
```

```python
import functools
import math

import jax
import jax.numpy as jnp
from jax import lax
from jax.experimental import pallas as pl
from jax.experimental.pallas import tpu as pltpu

F32 = jnp.float32
BF16 = jnp.bfloat16
HIGHEST = lax.Precision.HIGHEST

D_MODEL = 1024
BRANCH_W = 512
FOX_HD = 64
FOX_HEADS = 8
S5_GROUP = 16
S5_GROUPS = 32
S5_STATE = 64
S5_NSTATE = S5_GROUPS * S5_STATE
HG_HEADS = 4
HG_D = 128
HG_CHUNK = 64
HG_SUB = 16
X_HEADS = 4
X_HD = 256
D_FF = 2816
EPS = 1e-6

VMEM_LIMIT_BYTES = 56 * 1024 * 1024

FOX_PAD = 128
FOX_QK_W = FOX_HEADS * FOX_PAD
C_FQ, C_FK, C_FV, C_SU, C_HQ, C_HF, C_HI, C_HG, C_GL, C_FF, C_END = (
    0, 1024, 2048, 2560, 3072, 3584, 4096, 4608, 5120, 8192, 8320)
FOX_BIAS_LANES = 3
LOG2E = 1.4426950408889634
FOX_SKIP_NATS = 40.0


NEG_BIG = -1e30


def _cparams(sem):
    return pltpu.CompilerParams(dimension_semantics=sem, vmem_limit_bytes=VMEM_LIMIT_BYTES)


def _rms(xf, g):
    return xf * lax.rsqrt(jnp.mean(xf * xf, axis=-1, keepdims=True) + EPS) * g


def _sigmoid(x):
    return 1.0 / (1.0 + jnp.exp(-x))


def _log_sigmoid(x):
    return jnp.minimum(x, 0.0) - jnp.log1p(jnp.exp(-jnp.abs(x)))


def _dot(a, b):
    return jnp.dot(a, b, preferred_element_type=F32)


def _dot_nt(a, b):
    return lax.dot_general(a, b, (((1,), (1,)), ((), ())), preferred_element_type=F32)


def _dot_tn(a, b):
    return lax.dot_general(a, b, (((0,), (0,)), ((), ())), preferred_element_type=F32)


def _const_spec(shape):
    nd = len(shape)
    return pl.BlockSpec(shape, lambda *_: (0,) * nd)


def _inproj_kernel(x_ref, g_ref, w_ref, gq_ref, gk_ref, qone_ref, hsum_ref,
                   q_ref, k_ref, v_ref, su_ref, hq_ref, hf_ref, hi_ref, hg_ref, gl_ref, ff_ref):
    h = _rms(x_ref[...], g_ref[...]).astype(BF16)

    def proj(lo, hi):
        return _dot(h, w_ref[:, lo:hi])

    def headnorm(t, g):
        ss = _dot((t * t).astype(BF16), hsum_ref[...])
        return t * lax.rsqrt(ss * (1.0 / FOX_HD) + EPS) * g

    for c in range(0, FOX_QK_W, 256):
        sl = slice(c, c + 256)
        q_ref[:, sl] = (headnorm(proj(C_FQ + c, C_FQ + c + 256), gq_ref[:, sl]) + qone_ref[:, sl]).astype(BF16)
        k_ref[:, sl] = headnorm(proj(C_FK + c, C_FK + c + 256), gk_ref[:, sl]).astype(BF16)
    v_ref[...] = proj(C_FV, C_SU).astype(BF16)
    su_ref[...] = proj(C_SU, C_HQ).astype(BF16)
    hq_ref[...] = proj(C_HQ, C_HF).astype(BF16)
    hf_ref[...] = proj(C_HF, C_HI)
    hi_ref[...] = proj(C_HI, C_HG).astype(BF16)
    hg_ref[...] = proj(C_HG, C_GL).astype(BF16)
    for c in range(C_GL, C_FF, 512):
        gl_ref[:, c - C_GL:c - C_GL + 512] = proj(c, c + 512).astype(BF16)
    ff_ref[...] = proj(C_FF, C_END)


def _inproj(x, g, w, gq, gk, qone, hsum, tm):
    n = x.shape[0]
    row = lambda c: pl.BlockSpec((tm, c), lambda i: (i, 0))
    outs = ([(FOX_QK_W, BF16)] * 2 + [(512, BF16)] * 3 + [(512, F32)] + [(512, BF16)] * 2
            + [(3072, BF16), (128, F32)])
    return pl.pallas_call(
        _inproj_kernel,
        grid=(n // tm,),
        in_specs=[row(D_MODEL), _const_spec((1, D_MODEL)), _const_spec((D_MODEL, C_END)),
                  _const_spec((1, FOX_QK_W)), _const_spec((1, FOX_QK_W)), _const_spec((1, FOX_QK_W)),
                  _const_spec((256, 256))],
        out_specs=[row(c) for c, _ in outs],
        out_shape=[jax.ShapeDtypeStruct((n, c), dt) for c, dt in outs],
        compiler_params=_cparams(("parallel",)),
        name="inproj",
    )(x, g, w, gq, gk, qone, hsum)


CUM_W = 256


def _logf_cumsum_kernel(ff_ref, bias_ref, c_ref):
    rows = ff_ref.shape[1]
    per_head = rows // FOX_HEADS
    lf = _log_sigmoid(ff_ref[0] + bias_ref[...])
    r = lax.broadcasted_iota(jnp.int32, (CUM_W, CUM_W), 0)
    c = lax.broadcasted_iota(jnp.int32, (CUM_W, CUM_W), 1)
    upper = (r <= c).astype(F32)
    y = jnp.dot(lf, upper, precision=HIGHEST, preferred_element_type=F32)
    tot = jnp.broadcast_to(y[:, CUM_W - 1:CUM_W], (rows, 128))
    rr = lax.broadcasted_iota(jnp.int32, (rows, rows), 0)
    cc = lax.broadcasted_iota(jnp.int32, (rows, rows), 1)
    prev = ((rr // per_head == cc // per_head) & (cc < rr)).astype(F32)
    offs = jnp.dot(prev, tot, precision=HIGHEST, preferred_element_type=F32)
    c_ref[0] = y + offs[:, 0:1]


def _logf_cumsum(ff_rows, bias_rows):
    b, rows, _ = ff_rows.shape
    return pl.pallas_call(
        _logf_cumsum_kernel,
        grid=(b,),
        in_specs=[pl.BlockSpec((1, rows, CUM_W), lambda i: (i, 0, 0)), _const_spec((rows, 1))],
        out_specs=pl.BlockSpec((1, rows, CUM_W), lambda i: (i, 0, 0)),
        out_shape=jax.ShapeDtypeStruct(ff_rows.shape, F32),
        compiler_params=_cparams(("parallel",)),
        name="fox_logf_cumsum",
    )(ff_rows, bias_rows)


def _fox_kernel(jstart_ref, q_ref, k_ref, v_ref, o_ref, m_ref, l_ref, acc_ref, *, tq, nq):
    b, p, i = pl.program_id(0), pl.program_id(1), pl.program_id(2)
    nchunk = tq // 128
    m_ref[...] = jnp.full(m_ref.shape, NEG_BIG, F32)
    l_ref[...] = jnp.zeros(l_ref.shape, F32)
    acc_ref[...] = jnp.zeros(acc_ref.shape, F32)

    def step(j, masked):
        start = pl.multiple_of(j * tq, tq)
        kb = k_ref[pl.ds(start, tq), :]
        vb = v_ref[pl.ds(start, tq), :]
        for hh in range(2):
            sl = slice(hh * FOX_PAD, (hh + 1) * FOX_PAD)
            s = _dot_nt(q_ref[:, sl], kb[:, sl])
            if masked:
                rq = lax.broadcasted_iota(jnp.int32, (tq, tq), 0)
                ck = lax.broadcasted_iota(jnp.int32, (tq, tq), 1)
                s = jnp.where(ck <= rq, s, NEG_BIG)
            mc = s[:, 0:128]
            for c in range(1, nchunk):
                mc = jnp.maximum(mc, s[:, c * 128:(c + 1) * 128])
            m_prev = m_ref[hh]
            m_new = jnp.maximum(m_prev, jnp.max(mc, axis=-1, keepdims=True))
            alpha = jnp.exp2(m_prev - m_new)
            pr = jnp.exp2(s - jnp.concatenate([m_new] * nchunk, axis=1))
            ls = pr[:, 0:128]
            for c in range(1, nchunk):
                ls = ls + pr[:, c * 128:(c + 1) * 128]
            l_ref[hh] = alpha * l_ref[hh] + ls
            acc_ref[hh] = alpha * acc_ref[hh] + _dot(pr.astype(BF16), vb)
            m_ref[hh] = m_new

    def body(j, carry):
        step(j, False)
        return carry

    base = ((b * (FOX_HEADS // 2) + p) * 2) * nq + i
    j0 = jnp.minimum(jstart_ref[base], jstart_ref[base + nq])
    lax.fori_loop(j0, i, body, 0)
    step(i, True)
    lane = lax.broadcasted_iota(jnp.int32, (tq, 128), 1)
    o0 = acc_ref[0] / jnp.sum(l_ref[0], axis=-1, keepdims=True)
    o1 = acc_ref[1] / jnp.sum(l_ref[1], axis=-1, keepdims=True)
    o_ref[...] = jnp.where(lane < FOX_HD, o0, o1).astype(BF16)


def _fox_attention(jstart, q, k, v, bsz, seq, tq):
    n = q.shape[0]
    nq = seq // tq
    grid_spec = pltpu.PrefetchScalarGridSpec(
        num_scalar_prefetch=1,
        grid=(bsz, FOX_HEADS // 2, nq),
        in_specs=[pl.BlockSpec((tq, 2 * FOX_PAD), lambda b, p, i, js: (b * nq + i, p)),
                  pl.BlockSpec((seq, 2 * FOX_PAD), lambda b, p, i, js: (b, p)),
                  pl.BlockSpec((seq, 128), lambda b, p, i, js: (b, p))],
        out_specs=pl.BlockSpec((tq, 128), lambda b, p, i, js: (b * nq + i, p)),
        scratch_shapes=[pltpu.VMEM((2, tq, 128), F32), pltpu.VMEM((2, tq, 128), F32),
                        pltpu.VMEM((2, tq, 128), F32)],
    )
    return pl.pallas_call(
        functools.partial(_fox_kernel, tq=tq, nq=nq),
        grid_spec=grid_spec,
        out_shape=jax.ShapeDtypeStruct((n, 512), BF16),
        compiler_params=_cparams(("parallel", "parallel", "arbitrary")),
        name="fox_attention",
    )(jstart, q, k, v)


def _fox_prepare(c, k, qk_bound, bsz, seq, tq):
    n = bsz * seq
    nq = seq // tq
    def top16(v):
        bits = lax.bitcast_convert_type(v, jnp.uint32) & jnp.uint32(0xFFFF0000)
        return lax.bitcast_convert_type(bits, F32)

    d = -(c * LOG2E)
    hi = top16(d)
    r1 = d - hi
    mid = top16(r1)
    lo = r1 - mid
    pieces = jnp.stack([hi, mid, lo], axis=-1).astype(BF16)
    pieces = pieces.transpose(0, 2, 1, 3).reshape(n, FOX_HEADS, FOX_BIAS_LANES)
    k_aug = k.reshape(n, FOX_HEADS, FOX_PAD).at[:, :, FOX_HD:FOX_HD + FOX_BIAS_LANES].set(pieces)
    cb = c.reshape(bsz, FOX_HEADS, nq, tq)
    c_end = cb[:, :, :, -1]
    c_start = cb[:, :, :, 0]
    gap = c_end[:, :, None, :] - c_start[:, :, :, None]
    skip = gap > (2.0 * qk_bound + FOX_SKIP_NATS)
    jstart = jnp.sum(skip.astype(jnp.int32), axis=-1)
    jstart = jnp.minimum(jstart, jnp.arange(nq, dtype=jnp.int32)[None, None, :])
    return k_aug.reshape(n, FOX_QK_W), jstart.reshape(-1)


S5_HALF_CH = 256
S5_HALF_ST = S5_NSTATE // 2


def _s5_kernel(u_ref, bre_ref, bim_ref, cre_ref, cim_ref, pw_ref, d_ref, wglu_ref, bglu_ref,
               o_ref, hr_ref, hi_ref, cr_ref, ci_ref, *, tt):
    t = pl.program_id(1)

    @pl.when(t == 0)
    def _():
        cr_ref[...] = jnp.zeros(cr_ref.shape, F32)
        ci_ref[...] = jnp.zeros(ci_ref.shape, F32)

    u = u_ref[...]
    for hf in range(2):
        uh = u[:, hf * S5_HALF_CH:(hf + 1) * S5_HALF_CH]
        sl = slice(hf * S5_HALF_ST, (hf + 1) * S5_HALF_ST)
        hr_ref[:, sl] = _dot(uh, bre_ref[hf])
        hi_ref[:, sl] = _dot(uh, bim_ref[hf])

    def scan_block(r, carry):
        cr, ci = carry
        rows = pl.ds(pl.multiple_of(r * 8, 8), 8)
        xr = hr_ref[rows, :]
        xi = hi_ref[rows, :]
        for n, k in enumerate((1, 2, 4)):
            ar = pw_ref[2 * n]
            ai = pw_ref[2 * n + 1]
            sr = pltpu.roll(xr, k, 0)
            si = pltpu.roll(xi, k, 0)
            xr, xi = xr + ar * sr - ai * si, xi + ar * si + ai * sr
        pr = pw_ref[6]
        pi = pw_ref[7]
        xr, xi = xr + pr * cr - pi * ci, xi + pr * ci + pi * cr
        hr_ref[rows, :] = xr
        hi_ref[rows, :] = xi
        return (jnp.broadcast_to(xr[7:8, :], xr.shape), jnp.broadcast_to(xi[7:8, :], xi.shape))

    cr, ci = lax.fori_loop(0, tt // 8, scan_block, (cr_ref[...], ci_ref[...]))
    cr_ref[...] = cr
    ci_ref[...] = ci

    ys = []
    for hf in range(2):
        sl = slice(hf * S5_HALF_ST, (hf + 1) * S5_HALF_ST)
        ys.append(_dot(hr_ref[:, sl].astype(BF16), cre_ref[hf])
                  - _dot(hi_ref[:, sl].astype(BF16), cim_ref[hf]))
    y = jnp.concatenate(ys, axis=1) + d_ref[...] * u.astype(F32)
    z = 0.5 * y * (1.0 + jnp.tanh(math.sqrt(2.0 / math.pi) * (y + 0.044715 * (y * y * y))))
    gate = _sigmoid(_dot(z.astype(BF16), wglu_ref[...]) + bglu_ref[...])
    o_ref[...] = (z * gate).astype(BF16)


def _s5(u, bre, bim, cre, cim, pw, d, wglu, bglu, bsz, seq, tt):
    n = u.shape[0]
    nt = seq // tt
    rowspec = pl.BlockSpec((tt, 512), lambda b, t: (b * nt + t, 0))
    return pl.pallas_call(
        functools.partial(_s5_kernel, tt=tt),
        grid=(bsz, nt),
        in_specs=[rowspec,
                  _const_spec((2, S5_HALF_CH, S5_HALF_ST)), _const_spec((2, S5_HALF_CH, S5_HALF_ST)),
                  _const_spec((2, S5_HALF_ST, S5_HALF_CH)), _const_spec((2, S5_HALF_ST, S5_HALF_CH)),
                  _const_spec((8, 8, S5_NSTATE)), _const_spec((1, 512)),
                  _const_spec((512, 512)), _const_spec((1, 512))],
        out_specs=rowspec,
        out_shape=jax.ShapeDtypeStruct((n, 512), BF16),
        scratch_shapes=[pltpu.VMEM((tt, S5_NSTATE), F32), pltpu.VMEM((tt, S5_NSTATE), F32),
                        pltpu.VMEM((8, S5_NSTATE), F32), pltpu.VMEM((8, S5_NSTATE), F32)],
        compiler_params=_cparams(("parallel", "arbitrary")),
        name="s5_mixer",
    )(u, bre, bim, cre, cim, pw, d, wglu, bglu)


def _s5_params(a_re, a_im, b_re, b_im, c_re, c_im, log_dt):
    g, p, gs = S5_GROUPS, S5_STATE, S5_GROUP
    dt = jnp.exp(log_dt.astype(F32))[:, None]
    ar, ai = a_re.astype(F32), a_im.astype(F32)
    mag = jnp.exp(dt * ar)
    abar_r, abar_i = mag * jnp.cos(dt * ai), mag * jnp.sin(dt * ai)
    inv_den = 1.0 / (ar * ar + ai * ai)
    nr, ni = abar_r - 1.0, abar_i
    coef_r = (nr * ar + ni * ai) * inv_den
    coef_i = (ni * ar - nr * ai) * inv_den
    br, bi = b_re.astype(F32), b_im.astype(F32)
    bbar_r = coef_r[..., None] * br - coef_i[..., None] * bi
    bbar_i = coef_r[..., None] * bi + coef_i[..., None] * br

    hg = g // 2
    eye = jnp.eye(hg, dtype=F32)

    def drive_bd(bb):
        x = bb.reshape(2, hg, p, gs).transpose(0, 1, 3, 2)
        x = x[:, :, :, None, :] * eye[None, :, None, :, None]
        return x.reshape(2, hg * gs, hg * p).astype(BF16)

    def read_bd(cc):
        x = cc.astype(F32).reshape(2, hg, gs, p).transpose(0, 1, 3, 2)
        x = x[:, :, :, None, :] * eye[None, :, None, :, None]
        return x.reshape(2, hg * p, hg * gs).astype(BF16)

    ar1, ai1 = abar_r.reshape(-1), abar_i.reshape(-1)
    pows = [(ar1, ai1)]
    for _ in range(7):
        pr, pi = pows[-1]
        pows.append((pr * ar1 - pi * ai1, pr * ai1 + pi * ar1))
    rows = jnp.arange(8)[:, None]

    def step_tab(k):
        r, i = pows[k - 1]
        mask = (rows >= k).astype(F32)
        return [mask * r[None, :], mask * i[None, :]]

    tabs = step_tab(1) + step_tab(2) + step_tab(4)
    tabs.append(jnp.stack([pows[r][0] for r in range(8)]))
    tabs.append(jnp.stack([pows[r][1] for r in range(8)]))
    pw = jnp.stack(tabs)
    return drive_bd(bbar_r), drive_bd(bbar_i), read_bd(c_re), read_bd(c_im), pw


def _hgrn_kernel(q_ref, f_ref, i_ref, g_ref, loglb_ref, log1mlb_ref, onemlb_ref, gain_ref,
                 o_ref, st_ref, *, tt):
    t = pl.program_id(2)

    @pl.when(t == 0)
    def _():
        st_ref[...] = jnp.zeros(st_ref.shape, F32)

    c_sz, sub = HG_CHUNK, HG_SUB
    n_sub = c_sz // sub
    r64 = lax.broadcasted_iota(jnp.int32, (c_sz, c_sz), 0)
    c64 = lax.broadcasted_iota(jnp.int32, (c_sz, c_sz), 1)
    lower = (c64 <= r64).astype(F32)
    row_s = lax.broadcasted_iota(jnp.int32, (sub, 1), 0)
    lane_s = lax.broadcasted_iota(jnp.int32, (sub, c_sz), 1)
    loglb = loglb_ref[...]
    log1mlb = log1mlb_ref[...]
    onemlb = onemlb_ref[...]
    gain = gain_ref[...]

    def chunk(c, carry):
        rows = pl.ds(pl.multiple_of(c * c_sz, c_sz), c_sz)
        z = f_ref[rows, :]
        bb = log1mlb + _log_sigmoid(z)
        logf = jnp.maximum(loglb, bb) + jnp.log1p(jnp.exp(-jnp.abs(loglb - bb)))
        key = onemlb * (1.0 / (1.0 + jnp.exp(z)))
        qx = q_ref[rows, :].astype(F32)
        qf = qx * _sigmoid(qx)
        vb = i_ref[rows, :]
        b = jnp.dot(lower, logf, precision=HIGHEST, preferred_element_type=F32)
        b_last = b[c_sz - 1:c_sz, :]
        st = st_ref[...]
        o_inter = _dot_nt((qf * jnp.exp(b)).astype(BF16), st.astype(BF16))
        kd = (key * jnp.exp(b_last - b)).astype(BF16)
        st_ref[...] = st * jnp.exp(b_last) + _dot_tn(vb, kd)

        srows = []
        for blk in range(n_sub):
            lo = blk * sub
            b_i = b[lo:lo + sub]
            q_i = qf[lo:lo + sub]
            k_i = key[lo:lo + sub]
            sd = jnp.zeros((sub, c_sz), F32)
            for s in range(sub):
                e = jnp.exp(jnp.minimum(b_i - b_i[s:s + 1], 0.0))
                col = jnp.sum(q_i * e * k_i[s:s + 1], axis=-1, keepdims=True)
                col = jnp.where(row_s >= s, col, 0.0)
                sd = jnp.where(lane_s == lo + s, col, sd)
            if blk > 0:
                ref = b[lo - 1:lo]
                qt = (q_i * jnp.exp(b_i - ref)).astype(BF16)
                kt = (key * jnp.exp(jnp.minimum(ref - b, 0.0))).astype(BF16)
                sd = jnp.where(lane_s < lo, _dot_nt(qt, kt), sd)
            srows.append(sd)
        scores = jnp.concatenate(srows, axis=0)
        o = o_inter + _dot(scores.astype(BF16), vb)
        gx = g_ref[rows, :].astype(F32)
        y = o * lax.rsqrt(jnp.mean(o * o, axis=-1, keepdims=True) + EPS) * gain
        o_ref[rows, :] = (y * (gx * _sigmoid(gx))).astype(BF16)
        return carry

    lax.fori_loop(0, tt // c_sz, chunk, 0)


def _hgrn(hq, hf, hi, hg, loglb, log1mlb, onemlb, gain, bsz, seq, tt):
    n = hq.shape[0]
    nt = seq // tt
    spec = pl.BlockSpec((tt, HG_D), lambda b, h, t: (b * nt + t, h))
    vec = pl.BlockSpec((1, HG_D), lambda b, h, t: (0, h))
    return pl.pallas_call(
        functools.partial(_hgrn_kernel, tt=tt),
        grid=(bsz, HG_HEADS, nt),
        in_specs=[spec, spec, spec, spec, vec, vec, vec, _const_spec((1, HG_D))],
        out_specs=spec,
        out_shape=jax.ShapeDtypeStruct((n, 512), BF16),
        scratch_shapes=[pltpu.VMEM((HG_D, HG_D), F32)],
        compiler_params=_cparams(("parallel", "parallel", "arbitrary")),
        name="hgrn2_mixer",
    )(hq, hf, hi, hg, loglb, log1mlb, onemlb, gain)


def _merge_kernel(x_ref, yf_ref, ys_ref, yh_ref, gl_ref, wb_ref, wo_ref, o_ref):
    m = None
    for n, y_ref in enumerate((yf_ref, ys_ref, yh_ref)):
        gate = _sigmoid(gl_ref[:, n * D_MODEL:(n + 1) * D_MODEL].astype(F32))
        term = gate * _dot(y_ref[...], wb_ref[n * BRANCH_W:(n + 1) * BRANCH_W, :])
        m = term if m is None else m + term
    o_ref[...] = x_ref[...] + _dot(m.astype(BF16), wo_ref[...])


def _merge(x, yf, ys, yh, gl, wb, wo, tm):
    n = x.shape[0]
    row = lambda c: pl.BlockSpec((tm, c), lambda i: (i, 0))
    return pl.pallas_call(
        _merge_kernel,
        grid=(n // tm,),
        in_specs=[row(D_MODEL), row(512), row(512), row(512), row(3 * D_MODEL),
                  _const_spec((3 * BRANCH_W, D_MODEL)), _const_spec((D_MODEL, D_MODEL))],
        out_specs=row(D_MODEL),
        out_shape=jax.ShapeDtypeStruct((n, D_MODEL), F32),
        compiler_params=_cparams(("parallel",)),
        name="merge_outproj",
    )(x, yf, ys, yh, gl, wb, wo)


def _memkv_kernel(m_ref, g_ref, wk_ref, wv_ref, gk_ref, k_ref, v_ref):
    h = _rms(m_ref[0], g_ref[...]).astype(BF16)
    kk = _dot(h, wk_ref[...])
    for hd in range(X_HEADS):
        sl = slice(hd * X_HD, (hd + 1) * X_HD)
        k_ref[0, :, sl] = _rms(kk[:, sl], gk_ref[...]).astype(BF16)
    v_ref[0] = _dot(h, wv_ref[...]).astype(BF16)


def _memkv(mem, g, wk, wv, gk):
    bsz, nm, _ = mem.shape
    spec = pl.BlockSpec((1, nm, D_MODEL), lambda b: (b, 0, 0))
    return pl.pallas_call(
        _memkv_kernel,
        grid=(bsz,),
        in_specs=[spec, _const_spec((1, D_MODEL)), _const_spec((D_MODEL, D_MODEL)),
                  _const_spec((D_MODEL, D_MODEL)), _const_spec((1, X_HD))],
        out_specs=[spec, spec],
        out_shape=[jax.ShapeDtypeStruct(mem.shape, BF16)] * 2,
        compiler_params=_cparams(("parallel",)),
        name="mem_kv",
    )(mem, g, wk, wv, gk)


def _xattn_kernel(x_ref, g_ref, wq_ref, gq_ref, k_ref, v_ref, wo_ref, o_ref):
    x = x_ref[...]
    h = _rms(x, g_ref[...]).astype(BF16)
    q = _dot(h, wq_ref[...])
    outs = []
    for hd in range(X_HEADS):
        sl = slice(hd * X_HD, (hd + 1) * X_HD)
        qh = (_rms(q[:, sl], gq_ref[...]) * (X_HD ** -0.5)).astype(BF16)
        s = _dot_nt(qh, k_ref[0, :, sl])
        p = jnp.exp(s - jnp.max(s, axis=-1, keepdims=True))
        l = jnp.sum(p, axis=-1, keepdims=True)
        outs.append((_dot(p.astype(BF16), v_ref[0, :, sl]) / l).astype(BF16))
    o_ref[...] = x + _dot(jnp.concatenate(outs, axis=1), wo_ref[...])


def _xattn(x, g, wq, gq, km, vm, wo, seq, tm):
    n = x.shape[0]
    nm = km.shape[1]
    per_b = seq // tm
    row = pl.BlockSpec((tm, D_MODEL), lambda i: (i, 0))
    kv = pl.BlockSpec((1, nm, D_MODEL), lambda i: (i // per_b, 0, 0))
    return pl.pallas_call(
        _xattn_kernel,
        grid=(n // tm,),
        in_specs=[row, _const_spec((1, D_MODEL)), _const_spec((D_MODEL, D_MODEL)),
                  _const_spec((1, X_HD)), kv, kv, _const_spec((D_MODEL, D_MODEL))],
        out_specs=row,
        out_shape=jax.ShapeDtypeStruct((n, D_MODEL), F32),
        compiler_params=_cparams(("parallel",)),
        name="cross_attention",
    )(x, g, wq, gq, km, vm, wo)


FF_CHUNK = 256


def _ffn_kernel(x_ref, g_ref, wgu_ref, wd_ref, o_ref, act_ref):
    x = x_ref[...]
    h = _rms(x, g_ref[...]).astype(BF16)
    for c in range(0, D_FF, FF_CHUNK):
        a = _dot(h, wgu_ref[:, c:c + FF_CHUNK])
        b = _dot(h, wgu_ref[:, D_FF + c:D_FF + c + FF_CHUNK])
        act_ref[:, c:c + FF_CHUNK] = (a * _sigmoid(a) * b).astype(BF16)
    o_ref[...] = x + _dot(act_ref[...], wd_ref[...])


def _ffn(x, g, wgu, wd, tm):
    n = x.shape[0]
    row = pl.BlockSpec((tm, D_MODEL), lambda i: (i, 0))
    return pl.pallas_call(
        _ffn_kernel,
        grid=(n // tm,),
        in_specs=[row, _const_spec((1, D_MODEL)), _const_spec((D_MODEL, 2 * D_FF)),
                  _const_spec((D_FF, D_MODEL))],
        out_specs=row,
        out_shape=jax.ShapeDtypeStruct((n, D_MODEL), F32),
        scratch_shapes=[pltpu.VMEM((tm, D_FF), BF16)],
        compiler_params=_cparams(("parallel",)),
        name="swiglu",
    )(x, g, wgu, wd)


def _reorder_w_in(w_in):
    depth = w_in.shape[0]

    def pad_heads(w):
        w = w.reshape(depth, D_MODEL, FOX_HEADS, FOX_HD)
        w = jnp.pad(w, ((0, 0), (0, 0), (0, 0), (0, FOX_PAD - FOX_HD)))
        return w.reshape(depth, D_MODEL, FOX_QK_W)

    fq, fk, fv = w_in[:, :, 0:512], w_in[:, :, 512:1024], w_in[:, :, 1024:1536]
    ff = w_in[:, :, 1536:1544]
    rest = w_in[:, :, 1544:]
    pad = jnp.zeros((depth, D_MODEL, C_END - C_FF - FOX_HEADS), w_in.dtype)
    return jnp.concatenate([pad_heads(fq), pad_heads(fk), fv, rest, ff, pad], axis=-1).astype(BF16)


def _pad_head_vec(v, fill=0.0):
    v = jnp.concatenate([v.astype(F32), jnp.full((FOX_PAD - FOX_HD,), fill, F32)])
    return jnp.tile(v, FOX_HEADS).reshape(1, FOX_QK_W)


def kernel(x, mem, norm_mix, w_in, fox_fbias, fox_qnorm, fox_knorm, s5_a_re, s5_a_im, s5_b_re, s5_b_im, s5_c_re, s5_c_im, s5_d, s5_log_dt, s5_w_glu, s5_b_glu, hg_lb, hg_onorm, w_branch, w_out, norm_x, norm_mem, xq, xk, xv, xo, x_qnorm, x_knorm, norm_ffn, w_gate_up, w_down):
    bsz, seq, _ = x.shape
    depth = w_in.shape[0]
    n = bsz * seq
    tm = min(256, seq)
    tq = min(512, seq)
    tt = min(256, seq)
    assert seq % CUM_W == 0 and seq % tq == 0 and seq % tt == 0 and tt % HG_CHUNK == 0

    row = lambda v: v.astype(F32).reshape(1, -1)
    w_in_r = _reorder_w_in(w_in)
    s5_wglu = s5_w_glu.astype(BF16)
    wb, wo = w_branch.astype(BF16), w_out.astype(BF16)
    wq, wk, wv, wxo = xq.astype(BF16), xk.astype(BF16), xv.astype(BF16), xo.astype(BF16)
    wgu, wd = w_gate_up.astype(BF16), w_down.astype(BF16)

    lb_all = jnp.cumsum(jax.nn.softmax(hg_lb.astype(F32), axis=0), axis=0)
    lb_all = lb_all - lb_all[0:1]

    lane = jnp.arange(256)
    hsum = ((lane[:, None] // FOX_PAD == lane[None, :] // FOX_PAD)
            & (lane[:, None] % FOX_PAD < FOX_HD)).astype(BF16)
    qone = _pad_head_vec(jnp.zeros((FOX_HD,), F32)).at[0, :].set(
        jnp.tile((jnp.arange(FOX_PAD) >= FOX_HD) & (jnp.arange(FOX_PAD) < FOX_HD + FOX_BIAS_LANES),
                 FOX_HEADS).astype(F32))
    cum_rows = FOX_HEADS * (seq // CUM_W)

    xf = x.astype(F32).reshape(n, D_MODEL)
    for l in range(depth):
        gq = _pad_head_vec(fox_qnorm[l]) * (FOX_HD ** -0.5 * LOG2E)
        gk = _pad_head_vec(fox_knorm[l])
        q, k, v, su, hq, hf, hi, hg, gl, ff = _inproj(xf, row(norm_mix[l]), w_in_r[l], gq, gk, qone, hsum, tm)

        ff_rows = ff[:, :FOX_HEADS].reshape(bsz, seq, FOX_HEADS).transpose(0, 2, 1).reshape(bsz, cum_rows, CUM_W)
        bias_rows = jnp.repeat(fox_fbias[l].astype(F32), seq // CUM_W).reshape(cum_rows, 1)
        c = _logf_cumsum(ff_rows, bias_rows).reshape(bsz, FOX_HEADS, seq)
        qk_bound = 1.01 * FOX_HD ** 0.5 * jnp.max(jnp.abs(fox_qnorm[l])) * jnp.max(jnp.abs(fox_knorm[l]))
        k_aug, jstart = _fox_prepare(c, k, qk_bound, bsz, seq, tq)
        y_fox = _fox_attention(jstart, q, k_aug, v, bsz, seq, tq)

        bre, bim, cre, cim, pw = _s5_params(s5_a_re[l], s5_a_im[l], s5_b_re[l], s5_b_im[l],
                                            s5_c_re[l], s5_c_im[l], s5_log_dt[l])
        y_s5 = _s5(su, bre, bim, cre, cim, pw, row(s5_d[l]), s5_wglu[l], row(s5_b_glu[l]), bsz, seq, tt)

        lb = lb_all[l].reshape(1, -1)
        y_hg = _hgrn(hq, hf, hi, hg, jnp.log(lb), jnp.log1p(-lb), 1.0 - lb, row(hg_onorm[l]), bsz, seq, tt)

        xf = _merge(xf, y_fox, y_s5, y_hg, gl, wb[l], wo[l], tm)

        km, vm = _memkv(mem.astype(F32), row(norm_mem[l]), wk[l], wv[l], row(x_knorm[l]))
        xf = _xattn(xf, row(norm_x[l]), wq[l], row(x_qnorm[l]), km, vm, wxo[l], seq, tm)
        xf = _ffn(xf, row(norm_ffn[l]), wgu[l], wd[l], tm)
    return xf.reshape(bsz, seq, D_MODEL).astype(x.dtype)
```

```python
import functools
import math

import jax
import jax.numpy as jnp
from jax import lax
from jax.experimental import pallas as pl
from jax.experimental.pallas import tpu as pltpu

F32 = jnp.float32
BF16 = jnp.bfloat16
HIGHEST = lax.Precision.HIGHEST

D_MODEL = 1024
BRANCH_W = 512
FOX_HD = 64
FOX_HEADS = 8
S5_GROUP = 16
S5_GROUPS = 32
S5_STATE = 64
S5_NSTATE = S5_GROUPS * S5_STATE
HG_HEADS = 4
HG_D = 128
HG_CHUNK = 64
HG_SUB = 16
X_HEADS = 4
X_HD = 256
D_FF = 2816
EPS = 1e-6

VMEM_LIMIT_BYTES = 56 * 1024 * 1024

FOX_PAD = 128
FOX_QK_W = FOX_HEADS * FOX_PAD
C_FQ, C_FK, C_FV, C_SU, C_HQ, C_HF, C_HI, C_HG, C_GL, C_FF, C_END = (
    0, 1024, 2048, 2560, 3072, 3584, 4096, 4608, 5120, 8192, 8320)
FOX_BIAS_LANES = 3
LOG2E = 1.4426950408889634
FOX_SKIP_NATS = 40.0


NEG_BIG = -1e30


def _cparams(sem):
    return pltpu.CompilerParams(dimension_semantics=sem, vmem_limit_bytes=VMEM_LIMIT_BYTES)


def _rms(xf, g):
    return xf * lax.rsqrt(jnp.mean(xf * xf, axis=-1, keepdims=True) + EPS) * g


def _sigmoid(x):
    return 1.0 / (1.0 + jnp.exp(-x))


def _log_sigmoid(x):
    return jnp.minimum(x, 0.0) - jnp.log1p(jnp.exp(-jnp.abs(x)))


def _dot(a, b):
    return jnp.dot(a, b, preferred_element_type=F32)


def _dot_nt(a, b):
    return lax.dot_general(a, b, (((1,), (1,)), ((), ())), preferred_element_type=F32)


def _dot_tn(a, b):
    return lax.dot_general(a, b, (((0,), (0,)), ((), ())), preferred_element_type=F32)


def _const_spec(shape, single=False):
    nd = len(shape)
    if single:
        return pl.BlockSpec(shape, lambda *_: (0,) * nd, pipeline_mode=pl.Buffered(1))
    return pl.BlockSpec(shape, lambda *_: (0,) * nd)


def _top16(v):
    bits = lax.bitcast_convert_type(v, jnp.uint32) & jnp.uint32(0xFFFF0000)
    return lax.bitcast_convert_type(bits, F32)


def _inproj_kernel(x_ref, g_ref, w_ref, gq_ref, gk_ref, qone_ref, hsum_ref, fb_ref, place_ref,
                   q_ref, k_ref, v_ref, su_ref, hq_ref, hf_ref, hi_ref, hg_ref, gl_ref, c_ref,
                   carry_ref, *, tiles_per_seq):
    tm = x_ref.shape[0]

    @pl.when(pl.program_id(0) % tiles_per_seq == 0)
    def _():
        carry_ref[...] = jnp.zeros(carry_ref.shape, F32)

    h = _rms(x_ref[...], g_ref[...]).astype(BF16)

    def proj(lo, hi):
        return _dot(h, w_ref[:, lo:hi])

    def sumsq(t):
        return _dot((t * t).astype(BF16), hsum_ref[...])

    def headnorm(t, ss, g):
        return t * lax.rsqrt(ss * (1.0 / FOX_HD) + EPS) * g

    lf = _log_sigmoid(proj(C_FF, C_END) + fb_ref[...])
    r = lax.broadcasted_iota(jnp.int32, (tm, tm), 0)
    cc = lax.broadcasted_iota(jnp.int32, (tm, tm), 1)
    lower = (cc <= r).astype(F32)
    cs = jnp.dot(lower, lf, precision=HIGHEST, preferred_element_type=F32) + carry_ref[...]
    carry_ref[...] = cs[tm - 1:tm, :]
    c_ref[...] = cs
    d = -(cs * LOG2E)
    hi = _top16(d)
    r1 = d - hi
    mid = _top16(r1)
    pieces = jnp.concatenate([hi, mid, r1 - mid], axis=1).astype(BF16)
    kbias = _dot(pieces, place_ref[...])

    chunks = range(0, FOX_QK_W, 256)
    tq = [proj(C_FQ + c, C_FQ + c + 256) for c in chunks]
    tk = [proj(C_FK + c, C_FK + c + 256) for c in chunks]
    sq = [sumsq(t) for t in tq]
    sk = [sumsq(t) for t in tk]
    for n, c in enumerate(chunks):
        sl = slice(c, c + 256)
        q_ref[:, sl] = (headnorm(tq[n], sq[n], gq_ref[:, sl]) + qone_ref[:, sl]).astype(BF16)
        k_ref[:, sl] = (headnorm(tk[n], sk[n], gk_ref[:, sl]) + kbias[:, sl]).astype(BF16)
    v_ref[...] = proj(C_FV, C_SU).astype(BF16)
    su_ref[...] = proj(C_SU, C_HQ).astype(BF16)
    hq_ref[...] = proj(C_HQ, C_HF).astype(BF16)
    hf_ref[...] = proj(C_HF, C_HI)
    hi_ref[...] = proj(C_HI, C_HG).astype(BF16)
    hg_ref[...] = proj(C_HG, C_GL).astype(BF16)
    for c in range(C_GL, C_FF, 512):
        gl_ref[:, c - C_GL:c - C_GL + 512] = proj(c, c + 512).astype(BF16)


def _inproj(x, g, w, gq, gk, qone, hsum, fb, place, seq, tm):
    n = x.shape[0]
    row = lambda c: pl.BlockSpec((tm, c), lambda i: (i, 0))
    outs = ([(FOX_QK_W, BF16)] * 2 + [(512, BF16)] * 3 + [(512, F32)] + [(512, BF16)] * 2
            + [(3072, BF16), (128, F32)])
    return pl.pallas_call(
        functools.partial(_inproj_kernel, tiles_per_seq=seq // tm),
        grid=(n // tm,),
        in_specs=[row(D_MODEL), _const_spec((1, D_MODEL)), _const_spec((D_MODEL, C_END), single=True),
                  _const_spec((1, FOX_QK_W)), _const_spec((1, FOX_QK_W)), _const_spec((1, FOX_QK_W)),
                  _const_spec((256, 256)), _const_spec((1, 128)),
                  _const_spec((FOX_BIAS_LANES * 128, FOX_QK_W))],
        out_specs=[row(c) for c, _ in outs],
        out_shape=[jax.ShapeDtypeStruct((n, c), dt) for c, dt in outs],
        scratch_shapes=[pltpu.VMEM((1, 128), F32)],
        compiler_params=_cparams(("arbitrary",)),
        name="inproj",
    )(x, g, w, gq, gk, qone, hsum, fb, place)


def _fox_kernel(jstart_ref, q_ref, k_ref, v_ref, o_ref, m_ref, l_ref, acc_ref, *, tq, nq):
    b, p, i = pl.program_id(0), pl.program_id(1), pl.program_id(2)
    nchunk = tq // 128
    m_ref[...] = jnp.full(m_ref.shape, NEG_BIG, F32)
    l_ref[...] = jnp.zeros(l_ref.shape, F32)
    acc_ref[...] = jnp.zeros(acc_ref.shape, F32)

    def step(j, masked):
        start = pl.multiple_of(j * tq, tq)
        kb = k_ref[pl.ds(start, tq), :]
        vb = v_ref[pl.ds(start, tq), :]
        for hh in range(2):
            sl = slice(hh * FOX_PAD, (hh + 1) * FOX_PAD)
            s = _dot_nt(q_ref[:, sl], kb[:, sl])
            if masked:
                rq = lax.broadcasted_iota(jnp.int32, (tq, tq), 0)
                ck = lax.broadcasted_iota(jnp.int32, (tq, tq), 1)
                s = jnp.where(ck <= rq, s, NEG_BIG)
            mc = s[:, 0:128]
            for c in range(1, nchunk):
                mc = jnp.maximum(mc, s[:, c * 128:(c + 1) * 128])
            m_prev = m_ref[hh]
            m_new = jnp.maximum(m_prev, jnp.max(mc, axis=-1, keepdims=True))
            alpha = jnp.exp2(m_prev - m_new)
            pr = jnp.exp2(s - jnp.concatenate([m_new] * nchunk, axis=1))
            ls = pr[:, 0:128]
            for c in range(1, nchunk):
                ls = ls + pr[:, c * 128:(c + 1) * 128]
            l_ref[hh] = alpha * l_ref[hh] + ls
            acc_ref[hh] = alpha * acc_ref[hh] + _dot(pr.astype(BF16), vb)
            m_ref[hh] = m_new

    def body(j, carry):
        step(j, False)
        return carry

    base = ((b * (FOX_HEADS // 2) + p) * 2) * nq + i
    j0 = jnp.minimum(jstart_ref[base], jstart_ref[base + nq])
    lax.fori_loop(j0, i, body, 0)
    step(i, True)
    lane = lax.broadcasted_iota(jnp.int32, (tq, 128), 1)
    o0 = acc_ref[0] / jnp.sum(l_ref[0], axis=-1, keepdims=True)
    o1 = acc_ref[1] / jnp.sum(l_ref[1], axis=-1, keepdims=True)
    o_ref[...] = jnp.where(lane < FOX_HD, o0, o1).astype(BF16)


def _fox_attention(jstart, q, k, v, bsz, seq, tq):
    n = q.shape[0]
    nq = seq // tq
    grid_spec = pltpu.PrefetchScalarGridSpec(
        num_scalar_prefetch=1,
        grid=(bsz, FOX_HEADS // 2, nq),
        in_specs=[pl.BlockSpec((tq, 2 * FOX_PAD), lambda b, p, i, js: (b * nq + i, p)),
                  pl.BlockSpec((seq, 2 * FOX_PAD), lambda b, p, i, js: (b, p)),
                  pl.BlockSpec((seq, 128), lambda b, p, i, js: (b, p))],
        out_specs=pl.BlockSpec((tq, 128), lambda b, p, i, js: (b * nq + i, p)),
        scratch_shapes=[pltpu.VMEM((2, tq, 128), F32), pltpu.VMEM((2, tq, 128), F32),
                        pltpu.VMEM((2, tq, 128), F32)],
    )
    return pl.pallas_call(
        functools.partial(_fox_kernel, tq=tq, nq=nq),
        grid_spec=grid_spec,
        out_shape=jax.ShapeDtypeStruct((n, 512), BF16),
        compiler_params=_cparams(("parallel", "parallel", "arbitrary")),
        name="fox_attention",
    )(jstart, q, k, v)


def _fox_first_blocks(c, qk_bound, bsz, seq, tq):
    nq = seq // tq
    cb = c.reshape(bsz, nq, tq, 128)
    c_end = cb[:, :, tq - 1, :FOX_HEADS].transpose(0, 2, 1)
    c_start = cb[:, :, 0, :FOX_HEADS].transpose(0, 2, 1)
    gap = c_end[:, :, None, :] - c_start[:, :, :, None]
    skip = gap > (2.0 * qk_bound + FOX_SKIP_NATS)
    jstart = jnp.sum(skip.astype(jnp.int32), axis=-1)
    jstart = jnp.minimum(jstart, jnp.arange(nq, dtype=jnp.int32)[None, None, :])
    return jstart.reshape(-1)


S5_HALF_CH = 256
S5_HALF_ST = S5_NSTATE // 2


def _s5_kernel(u_ref, bre_ref, bim_ref, cre_ref, cim_ref, pw_ref, d_ref, wglu_ref, bglu_ref,
               o_ref, hr_ref, hi_ref, cr_ref, ci_ref, *, tt):
    t = pl.program_id(1)

    @pl.when(t == 0)
    def _():
        cr_ref[...] = jnp.zeros(cr_ref.shape, F32)
        ci_ref[...] = jnp.zeros(ci_ref.shape, F32)

    u = u_ref[...]
    for hf in range(2):
        uh = u[:, hf * S5_HALF_CH:(hf + 1) * S5_HALF_CH]
        sl = slice(hf * S5_HALF_ST, (hf + 1) * S5_HALF_ST)
        hr_ref[:, sl] = _dot(uh, bre_ref[hf])
        hi_ref[:, sl] = _dot(uh, bim_ref[hf])

    def scan_block(r, carry):
        cr, ci = carry
        rows = pl.ds(pl.multiple_of(r * 8, 8), 8)
        xr = hr_ref[rows, :]
        xi = hi_ref[rows, :]
        for n, k in enumerate((1, 2, 4)):
            ar = pw_ref[2 * n]
            ai = pw_ref[2 * n + 1]
            sr = pltpu.roll(xr, k, 0)
            si = pltpu.roll(xi, k, 0)
            xr, xi = xr + ar * sr - ai * si, xi + ar * si + ai * sr
        pr = pw_ref[6]
        pi = pw_ref[7]
        xr, xi = xr + pr * cr - pi * ci, xi + pr * ci + pi * cr
        hr_ref[rows, :] = xr
        hi_ref[rows, :] = xi
        return (jnp.broadcast_to(xr[7:8, :], xr.shape), jnp.broadcast_to(xi[7:8, :], xi.shape))

    cr, ci = lax.fori_loop(0, tt // 8, scan_block, (cr_ref[...], ci_ref[...]))
    cr_ref[...] = cr
    ci_ref[...] = ci

    ys = []
    for hf in range(2):
        sl = slice(hf * S5_HALF_ST, (hf + 1) * S5_HALF_ST)
        ys.append(_dot(hr_ref[:, sl].astype(BF16), cre_ref[hf])
                  - _dot(hi_ref[:, sl].astype(BF16), cim_ref[hf]))
    y = jnp.concatenate(ys, axis=1) + d_ref[...] * u.astype(F32)
    z = 0.5 * y * (1.0 + jnp.tanh(math.sqrt(2.0 / math.pi) * (y + 0.044715 * (y * y * y))))
    gate = _sigmoid(_dot(z.astype(BF16), wglu_ref[...]) + bglu_ref[...])
    o_ref[...] = (z * gate).astype(BF16)


def _s5(u, bre, bim, cre, cim, pw, d, wglu, bglu, bsz, seq, tt):
    n = u.shape[0]
    nt = seq // tt
    rowspec = pl.BlockSpec((tt, 512), lambda b, t: (b * nt + t, 0))
    return pl.pallas_call(
        functools.partial(_s5_kernel, tt=tt),
        grid=(bsz, nt),
        in_specs=[rowspec,
                  _const_spec((2, S5_HALF_CH, S5_HALF_ST)), _const_spec((2, S5_HALF_CH, S5_HALF_ST)),
                  _const_spec((2, S5_HALF_ST, S5_HALF_CH)), _const_spec((2, S5_HALF_ST, S5_HALF_CH)),
                  _const_spec((8, 8, S5_NSTATE)), _const_spec((1, 512)),
                  _const_spec((512, 512)), _const_spec((1, 512))],
        out_specs=rowspec,
        out_shape=jax.ShapeDtypeStruct((n, 512), BF16),
        scratch_shapes=[pltpu.VMEM((tt, S5_NSTATE), F32), pltpu.VMEM((tt, S5_NSTATE), F32),
                        pltpu.VMEM((8, S5_NSTATE), F32), pltpu.VMEM((8, S5_NSTATE), F32)],
        compiler_params=_cparams(("parallel", "arbitrary")),
        name="s5_mixer",
    )(u, bre, bim, cre, cim, pw, d, wglu, bglu)


def _s5_params(a_re, a_im, b_re, b_im, c_re, c_im, log_dt):
    g, p, gs = S5_GROUPS, S5_STATE, S5_GROUP
    dt = jnp.exp(log_dt.astype(F32))[:, None]
    ar, ai = a_re.astype(F32), a_im.astype(F32)
    mag = jnp.exp(dt * ar)
    abar_r, abar_i = mag * jnp.cos(dt * ai), mag * jnp.sin(dt * ai)
    inv_den = 1.0 / (ar * ar + ai * ai)
    nr, ni = abar_r - 1.0, abar_i
    coef_r = (nr * ar + ni * ai) * inv_den
    coef_i = (ni * ar - nr * ai) * inv_den
    br, bi = b_re.astype(F32), b_im.astype(F32)
    bbar_r = coef_r[..., None] * br - coef_i[..., None] * bi
    bbar_i = coef_r[..., None] * bi + coef_i[..., None] * br

    hg = g // 2
    eye = jnp.eye(hg, dtype=F32)

    def drive_bd(bb):
        x = bb.reshape(2, hg, p, gs).transpose(0, 1, 3, 2)
        x = x[:, :, :, None, :] * eye[None, :, None, :, None]
        return x.reshape(2, hg * gs, hg * p).astype(BF16)

    def read_bd(cc):
        x = cc.astype(F32).reshape(2, hg, gs, p).transpose(0, 1, 3, 2)
        x = x[:, :, :, None, :] * eye[None, :, None, :, None]
        return x.reshape(2, hg * p, hg * gs).astype(BF16)

    ar1, ai1 = abar_r.reshape(-1), abar_i.reshape(-1)
    pows = [(ar1, ai1)]
    for _ in range(7):
        pr, pi = pows[-1]
        pows.append((pr * ar1 - pi * ai1, pr * ai1 + pi * ar1))
    rows = jnp.arange(8)[:, None]

    def step_tab(k):
        r, i = pows[k - 1]
        mask = (rows >= k).astype(F32)
        return [mask * r[None, :], mask * i[None, :]]

    tabs = step_tab(1) + step_tab(2) + step_tab(4)
    tabs.append(jnp.stack([pows[r][0] for r in range(8)]))
    tabs.append(jnp.stack([pows[r][1] for r in range(8)]))
    pw = jnp.stack(tabs)
    return drive_bd(bbar_r), drive_bd(bbar_i), read_bd(c_re), read_bd(c_im), pw


def _hgrn_kernel(q_ref, f_ref, i_ref, g_ref, loglb_ref, log1mlb_ref, onemlb_ref, gain_ref,
                 o_ref, st_ref, *, tt):
    t = pl.program_id(1)

    @pl.when(t == 0)
    def _():
        st_ref[...] = jnp.zeros(st_ref.shape, F32)

    c_sz, sub = HG_CHUNK, HG_SUB
    n_sub = c_sz // sub
    r64 = lax.broadcasted_iota(jnp.int32, (c_sz, c_sz), 0)
    c64 = lax.broadcasted_iota(jnp.int32, (c_sz, c_sz), 1)
    lower = (c64 <= r64).astype(F32)
    row_s = lax.broadcasted_iota(jnp.int32, (sub, 1), 0)
    lane_s = lax.broadcasted_iota(jnp.int32, (sub, c_sz), 1)
    gain = gain_ref[...]

    def head_chunk(rows, hd):
        sl = slice(hd * HG_D, (hd + 1) * HG_D)
        loglb = loglb_ref[:, sl]
        z = f_ref[rows, sl]
        bb = log1mlb_ref[:, sl] + _log_sigmoid(z)
        logf = jnp.maximum(loglb, bb) + jnp.log1p(jnp.exp(-jnp.abs(loglb - bb)))
        key = onemlb_ref[:, sl] * (1.0 / (1.0 + jnp.exp(z)))
        qx = q_ref[rows, sl].astype(F32)
        qf = qx * _sigmoid(qx)
        vb = i_ref[rows, sl]
        b = jnp.dot(lower, logf, precision=HIGHEST, preferred_element_type=F32)
        b_last = b[c_sz - 1:c_sz, :]
        st = st_ref[hd]
        o_inter = _dot_nt((qf * jnp.exp(b)).astype(BF16), st.astype(BF16))
        kd = (key * jnp.exp(b_last - b)).astype(BF16)
        st_ref[hd] = st * jnp.exp(b_last) + _dot_tn(vb, kd)

        srows = []
        for blk in range(n_sub):
            lo = blk * sub
            b_i = b[lo:lo + sub]
            q_i = qf[lo:lo + sub]
            k_i = key[lo:lo + sub]
            sd = jnp.zeros((sub, c_sz), F32)
            for s in range(sub):
                e = jnp.exp(jnp.minimum(b_i - b_i[s:s + 1], 0.0))
                col = jnp.sum(q_i * e * k_i[s:s + 1], axis=-1, keepdims=True)
                col = jnp.where(row_s >= s, col, 0.0)
                sd = jnp.where(lane_s == lo + s, col, sd)
            if blk > 0:
                ref = b[lo - 1:lo]
                qt = (q_i * jnp.exp(b_i - ref)).astype(BF16)
                kt = (key * jnp.exp(jnp.minimum(ref - b, 0.0))).astype(BF16)
                sd = jnp.where(lane_s < lo, _dot_nt(qt, kt), sd)
            srows.append(sd)
        scores = jnp.concatenate(srows, axis=0)
        o = o_inter + _dot(scores.astype(BF16), vb)
        gx = g_ref[rows, sl].astype(F32)
        y = o * lax.rsqrt(jnp.mean(o * o, axis=-1, keepdims=True) + EPS) * gain
        o_ref[rows, sl] = (y * (gx * _sigmoid(gx))).astype(BF16)

    def chunk(c, carry):
        rows = pl.ds(pl.multiple_of(c * c_sz, c_sz), c_sz)
        for hd in range(HG_HEADS):
            head_chunk(rows, hd)
        return carry

    lax.fori_loop(0, tt // c_sz, chunk, 0)


def _hgrn(hq, hf, hi, hg, loglb, log1mlb, onemlb, gain, bsz, seq, tt):
    n = hq.shape[0]
    nt = seq // tt
    spec = pl.BlockSpec((tt, 512), lambda b, t: (b * nt + t, 0))
    return pl.pallas_call(
        functools.partial(_hgrn_kernel, tt=tt),
        grid=(bsz, nt),
        in_specs=[spec, spec, spec, spec, _const_spec((1, 512)), _const_spec((1, 512)),
                  _const_spec((1, 512)), _const_spec((1, HG_D))],
        out_specs=spec,
        out_shape=jax.ShapeDtypeStruct((n, 512), BF16),
        scratch_shapes=[pltpu.VMEM((HG_HEADS, HG_D, HG_D), F32)],
        compiler_params=_cparams(("parallel", "arbitrary")),
        name="hgrn2_mixer",
    )(hq, hf, hi, hg, loglb, log1mlb, onemlb, gain)


def _merge_kernel(x_ref, yf_ref, ys_ref, yh_ref, gl_ref, wb_ref, wo_ref, o_ref):
    m = None
    for n, y_ref in enumerate((yf_ref, ys_ref, yh_ref)):
        gate = _sigmoid(gl_ref[:, n * D_MODEL:(n + 1) * D_MODEL].astype(F32))
        term = gate * _dot(y_ref[...], wb_ref[n * BRANCH_W:(n + 1) * BRANCH_W, :])
        m = term if m is None else m + term
    o_ref[...] = x_ref[...] + _dot(m.astype(BF16), wo_ref[...])


def _merge(x, yf, ys, yh, gl, wb, wo, tm):
    n = x.shape[0]
    row = lambda c: pl.BlockSpec((tm, c), lambda i: (i, 0))
    return pl.pallas_call(
        _merge_kernel,
        grid=(n // tm,),
        in_specs=[row(D_MODEL), row(512), row(512), row(512), row(3 * D_MODEL),
                  _const_spec((3 * BRANCH_W, D_MODEL), single=True),
                  _const_spec((D_MODEL, D_MODEL), single=True)],
        out_specs=row(D_MODEL),
        out_shape=jax.ShapeDtypeStruct((n, D_MODEL), F32),
        compiler_params=_cparams(("parallel",)),
        name="merge_outproj",
    )(x, yf, ys, yh, gl, wb, wo)


def _memkv_kernel(m_ref, g_ref, wk_ref, wv_ref, gk_ref, k_ref, v_ref):
    h = _rms(m_ref[0], g_ref[...]).astype(BF16)
    kk = _dot(h, wk_ref[...])
    for hd in range(X_HEADS):
        sl = slice(hd * X_HD, (hd + 1) * X_HD)
        k_ref[0, :, sl] = _rms(kk[:, sl], gk_ref[...]).astype(BF16)
    v_ref[0] = _dot(h, wv_ref[...]).astype(BF16)


def _memkv(mem, g, wk, wv, gk):
    bsz, nm, _ = mem.shape
    spec = pl.BlockSpec((1, nm, D_MODEL), lambda b: (b, 0, 0))
    return pl.pallas_call(
        _memkv_kernel,
        grid=(bsz,),
        in_specs=[spec, _const_spec((1, D_MODEL)), _const_spec((D_MODEL, D_MODEL)),
                  _const_spec((D_MODEL, D_MODEL)), _const_spec((1, X_HD))],
        out_specs=[spec, spec],
        out_shape=[jax.ShapeDtypeStruct(mem.shape, BF16)] * 2,
        compiler_params=_cparams(("parallel",)),
        name="mem_kv",
    )(mem, g, wk, wv, gk)


def _xattn_kernel(x_ref, g_ref, wq_ref, gq_ref, k_ref, v_ref, wo_ref, o_ref):
    x = x_ref[...]
    h = _rms(x, g_ref[...]).astype(BF16)
    q = _dot(h, wq_ref[...])
    outs = []
    for hd in range(X_HEADS):
        sl = slice(hd * X_HD, (hd + 1) * X_HD)
        qh = (_rms(q[:, sl], gq_ref[...]) * (X_HD ** -0.5)).astype(BF16)
        s = _dot_nt(qh, k_ref[0, :, sl])
        p = jnp.exp(s - jnp.max(s, axis=-1, keepdims=True))
        l = jnp.sum(p, axis=-1, keepdims=True)
        outs.append((_dot(p.astype(BF16), v_ref[0, :, sl]) / l).astype(BF16))
    o_ref[...] = x + _dot(jnp.concatenate(outs, axis=1), wo_ref[...])


def _xattn(x, g, wq, gq, km, vm, wo, seq, tm):
    n = x.shape[0]
    nm = km.shape[1]
    per_b = seq // tm
    row = pl.BlockSpec((tm, D_MODEL), lambda i: (i, 0))
    kv = pl.BlockSpec((1, nm, D_MODEL), lambda i: (i // per_b, 0, 0))
    return pl.pallas_call(
        _xattn_kernel,
        grid=(n // tm,),
        in_specs=[row, _const_spec((1, D_MODEL)), _const_spec((D_MODEL, D_MODEL), single=True),
                  _const_spec((1, X_HD)), kv, kv, _const_spec((D_MODEL, D_MODEL), single=True)],
        out_specs=row,
        out_shape=jax.ShapeDtypeStruct((n, D_MODEL), F32),
        compiler_params=_cparams(("parallel",)),
        name="cross_attention",
    )(x, g, wq, gq, km, vm, wo)


FF_CHUNK = 256


def _ffn_kernel(x_ref, g_ref, wgu_ref, wd_ref, o_ref, act_ref):
    x = x_ref[...]
    h = _rms(x, g_ref[...]).astype(BF16)
    for c in range(0, D_FF, FF_CHUNK):
        a = _dot(h, wgu_ref[:, c:c + FF_CHUNK])
        b = _dot(h, wgu_ref[:, D_FF + c:D_FF + c + FF_CHUNK])
        act_ref[:, c:c + FF_CHUNK] = (a * _sigmoid(a) * b).astype(BF16)
    o_ref[...] = x + _dot(act_ref[...], wd_ref[...])


def _ffn(x, g, wgu, wd, tm):
    n = x.shape[0]
    row = pl.BlockSpec((tm, D_MODEL), lambda i: (i, 0))
    return pl.pallas_call(
        _ffn_kernel,
        grid=(n // tm,),
        in_specs=[row, _const_spec((1, D_MODEL)), _const_spec((D_MODEL, 2 * D_FF), single=True),
                  _const_spec((D_FF, D_MODEL), single=True)],
        out_specs=row,
        out_shape=jax.ShapeDtypeStruct((n, D_MODEL), F32),
        scratch_shapes=[pltpu.VMEM((tm, D_FF), BF16)],
        compiler_params=_cparams(("parallel",)),
        name="swiglu",
    )(x, g, wgu, wd)


def _reorder_w_in(w_in):
    depth = w_in.shape[0]

    def pad_heads(w):
        w = w.reshape(depth, D_MODEL, FOX_HEADS, FOX_HD)
        w = jnp.pad(w, ((0, 0), (0, 0), (0, 0), (0, FOX_PAD - FOX_HD)))
        return w.reshape(depth, D_MODEL, FOX_QK_W)

    fq, fk, fv = w_in[:, :, 0:512], w_in[:, :, 512:1024], w_in[:, :, 1024:1536]
    ff = w_in[:, :, 1536:1544]
    rest = w_in[:, :, 1544:]
    pad = jnp.zeros((depth, D_MODEL, C_END - C_FF - FOX_HEADS), w_in.dtype)
    return jnp.concatenate([pad_heads(fq), pad_heads(fk), fv, rest, ff, pad], axis=-1).astype(BF16)


def _pad_head_vec(v, fill=0.0):
    v = jnp.concatenate([v.astype(F32), jnp.full((FOX_PAD - FOX_HD,), fill, F32)])
    return jnp.tile(v, FOX_HEADS).reshape(1, FOX_QK_W)


def kernel(x, mem, norm_mix, w_in, fox_fbias, fox_qnorm, fox_knorm, s5_a_re, s5_a_im, s5_b_re, s5_b_im, s5_c_re, s5_c_im, s5_d, s5_log_dt, s5_w_glu, s5_b_glu, hg_lb, hg_onorm, w_branch, w_out, norm_x, norm_mem, xq, xk, xv, xo, x_qnorm, x_knorm, norm_ffn, w_gate_up, w_down):
    bsz, seq, _ = x.shape
    depth = w_in.shape[0]
    n = bsz * seq
    tm_in = min(512, seq)
    tm = min(1024, seq)
    tq = min(512, seq)
    tt = min(256, seq)
    assert seq % tm == 0 and seq % tq == 0 and seq % tt == 0 and tt % HG_CHUNK == 0

    row = lambda v: v.astype(F32).reshape(1, -1)
    w_in_r = _reorder_w_in(w_in)
    s5_wglu = s5_w_glu.astype(BF16)
    wb, wo = w_branch.astype(BF16), w_out.astype(BF16)
    wq, wk, wv, wxo = xq.astype(BF16), xk.astype(BF16), xv.astype(BF16), xo.astype(BF16)
    wgu, wd = w_gate_up.astype(BF16), w_down.astype(BF16)

    lb_all = jnp.cumsum(jax.nn.softmax(hg_lb.astype(F32), axis=0), axis=0)
    lb_all = lb_all - lb_all[0:1]

    lane = jnp.arange(256)
    hsum = ((lane[:, None] // FOX_PAD == lane[None, :] // FOX_PAD)
            & (lane[:, None] % FOX_PAD < FOX_HD)).astype(BF16)
    qone = _pad_head_vec(jnp.zeros((FOX_HD,), F32)).at[0, :].set(
        jnp.tile((jnp.arange(FOX_PAD) >= FOX_HD) & (jnp.arange(FOX_PAD) < FOX_HD + FOX_BIAS_LANES),
                 FOX_HEADS).astype(F32))
    src = jnp.arange(FOX_BIAS_LANES * 128)
    dst = jnp.arange(FOX_QK_W)
    place = ((src[:, None] % 128 == dst[None, :] // FOX_PAD)
             & (dst[None, :] % FOX_PAD == FOX_HD + src[:, None] // 128)).astype(BF16)

    xf = x.astype(F32).reshape(n, D_MODEL)
    for l in range(depth):
        gq = _pad_head_vec(fox_qnorm[l]) * (FOX_HD ** -0.5 * LOG2E)
        gk = _pad_head_vec(fox_knorm[l])
        fb = jnp.pad(fox_fbias[l].astype(F32), (0, 128 - FOX_HEADS)).reshape(1, 128)
        q, k, v, su, hq, hf, hi, hg, gl, c = _inproj(xf, row(norm_mix[l]), w_in_r[l], gq, gk, qone, hsum,
                                                     fb, place, seq, tm_in)

        qk_bound = 1.01 * FOX_HD ** 0.5 * jnp.max(jnp.abs(fox_qnorm[l])) * jnp.max(jnp.abs(fox_knorm[l]))
        jstart = _fox_first_blocks(c, qk_bound, bsz, seq, tq)
        y_fox = _fox_attention(jstart, q, k, v, bsz, seq, tq)

        bre, bim, cre, cim, pw = _s5_params(s5_a_re[l], s5_a_im[l], s5_b_re[l], s5_b_im[l],
                                            s5_c_re[l], s5_c_im[l], s5_log_dt[l])
        y_s5 = _s5(su, bre, bim, cre, cim, pw, row(s5_d[l]), s5_wglu[l], row(s5_b_glu[l]), bsz, seq, tt)

        lb = lb_all[l].reshape(1, -1)
        y_hg = _hgrn(hq, hf, hi, hg, jnp.log(lb), jnp.log1p(-lb), 1.0 - lb, row(hg_onorm[l]), bsz, seq, tt)

        xf = _merge(xf, y_fox, y_s5, y_hg, gl, wb[l], wo[l], tm)

        km, vm = _memkv(mem.astype(F32), row(norm_mem[l]), wk[l], wv[l], row(x_knorm[l]))
        xf = _xattn(xf, row(norm_x[l]), wq[l], row(x_qnorm[l]), km, vm, wxo[l], seq, tm)
        xf = _ffn(xf, row(norm_ffn[l]), wgu[l], wd[l], tm)
    return xf.reshape(bsz, seq, D_MODEL).astype(x.dtype)
```

```python
import functools
import math

import jax
import jax.numpy as jnp
from jax import lax
from jax.experimental import pallas as pl
from jax.experimental.pallas import tpu as pltpu

F32 = jnp.float32
BF16 = jnp.bfloat16
HIGHEST = lax.Precision.HIGHEST

D_MODEL = 1024
BRANCH_W = 512
FOX_HD = 64
FOX_HEADS = 8
S5_GROUP = 16
S5_GROUPS = 32
S5_STATE = 64
S5_NSTATE = S5_GROUPS * S5_STATE
HG_HEADS = 4
HG_D = 128
HG_CHUNK = 64
HG_SUB = 16
X_HEADS = 4
X_HD = 256
D_FF = 2816
EPS = 1e-6

VMEM_LIMIT_BYTES = 56 * 1024 * 1024

FOX_PAD = 128
FOX_QK_W = FOX_HEADS * FOX_PAD
C_FQ, C_FK, C_FV, C_SU, C_HQ, C_HF, C_HI, C_HG, C_GL, C_FF, C_END = (
    0, 1024, 2048, 2560, 3072, 3584, 4096, 4608, 5120, 8192, 8320)
FOX_BIAS_LANES = 3
LOG2E = 1.4426950408889634
FOX_SKIP_NATS = 30.0


NEG_BIG = -1e30


def _cparams(sem):
    return pltpu.CompilerParams(dimension_semantics=sem, vmem_limit_bytes=VMEM_LIMIT_BYTES)


def _rms(xf, g):
    return xf * lax.rsqrt(jnp.mean(xf * xf, axis=-1, keepdims=True) + EPS) * g


def _sigmoid(x):
    return 1.0 / (1.0 + jnp.exp(-x))


def _log_sigmoid(x):
    return jnp.minimum(x, 0.0) - jnp.log(1.0 + jnp.exp(-jnp.abs(x)))


def _dot(a, b):
    return jnp.dot(a, b, preferred_element_type=F32)


def _dot_nt(a, b):
    return lax.dot_general(a, b, (((1,), (1,)), ((), ())), preferred_element_type=F32)


def _dot_tn(a, b):
    return lax.dot_general(a, b, (((0,), (0,)), ((), ())), preferred_element_type=F32)


def _const_spec(shape, single=False):
    nd = len(shape)
    if single:
        return pl.BlockSpec(shape, lambda *_: (0,) * nd, pipeline_mode=pl.Buffered(1))
    return pl.BlockSpec(shape, lambda *_: (0,) * nd)


def _top16(v):
    bits = lax.bitcast_convert_type(v, jnp.uint32) & jnp.uint32(0xFFFF0000)
    return lax.bitcast_convert_type(bits, F32)


def _inproj_kernel(x_ref, g_ref, w_ref, gq_ref, gk_ref, qone_ref, hsum_ref, fb_ref, place_ref,
                   q_ref, k_ref, v_ref, su_ref, hq_ref, hf_ref, hi_ref, hg_ref, gl_ref, c_ref,
                   carry_ref, *, tiles_per_seq):
    tm = x_ref.shape[0]

    @pl.when(pl.program_id(0) % tiles_per_seq == 0)
    def _():
        carry_ref[...] = jnp.zeros(carry_ref.shape, F32)

    h = _rms(x_ref[...], g_ref[...]).astype(BF16)

    def proj(lo, hi):
        return _dot(h, w_ref[:, lo:hi])

    def sumsq(t):
        return _dot((t * t).astype(BF16), hsum_ref[...])

    def headnorm(t, ss, g):
        return t * lax.rsqrt(ss * (1.0 / FOX_HD) + EPS) * g

    lf = _log_sigmoid(proj(C_FF, C_END) + fb_ref[...])
    r = lax.broadcasted_iota(jnp.int32, (tm, tm), 0)
    cc = lax.broadcasted_iota(jnp.int32, (tm, tm), 1)
    lower = (cc <= r).astype(F32)
    cs = jnp.dot(lower, lf, precision=HIGHEST, preferred_element_type=F32) + carry_ref[...]
    carry_ref[...] = cs[tm - 1:tm, :]
    c_ref[...] = cs
    d = -(cs * LOG2E)
    hi = _top16(d)
    r1 = d - hi
    mid = _top16(r1)
    pieces = jnp.concatenate([hi, mid, r1 - mid], axis=1).astype(BF16)
    kbias = _dot(pieces, place_ref[...])

    chunks = range(0, FOX_QK_W, 256)
    tq = [proj(C_FQ + c, C_FQ + c + 256) for c in chunks]
    tk = [proj(C_FK + c, C_FK + c + 256) for c in chunks]
    sq = [sumsq(t) for t in tq]
    sk = [sumsq(t) for t in tk]
    for n, c in enumerate(chunks):
        sl = slice(c, c + 256)
        q_ref[:, sl] = (headnorm(tq[n], sq[n], gq_ref[:, sl]) + qone_ref[:, sl]).astype(BF16)
        k_ref[:, sl] = (headnorm(tk[n], sk[n], gk_ref[:, sl]) + kbias[:, sl]).astype(BF16)
    v_ref[...] = proj(C_FV, C_SU).astype(BF16)
    su_ref[...] = proj(C_SU, C_HQ).astype(BF16)
    hq_ref[...] = proj(C_HQ, C_HF).astype(BF16)
    hf_ref[...] = proj(C_HF, C_HI)
    hi_ref[...] = proj(C_HI, C_HG).astype(BF16)
    hg_ref[...] = proj(C_HG, C_GL).astype(BF16)
    for c in range(C_GL, C_FF, 512):
        gl_ref[:, c - C_GL:c - C_GL + 512] = proj(c, c + 512).astype(BF16)


def _inproj(x, g, w, gq, gk, qone, hsum, fb, place, seq, tm):
    n = x.shape[0]
    row = lambda c: pl.BlockSpec((tm, c), lambda i: (i, 0))
    outs = ([(FOX_QK_W, BF16)] * 2 + [(512, BF16)] * 3 + [(512, F32)] + [(512, BF16)] * 2
            + [(3072, BF16), (128, F32)])
    return pl.pallas_call(
        functools.partial(_inproj_kernel, tiles_per_seq=seq // tm),
        grid=(n // tm,),
        in_specs=[row(D_MODEL), _const_spec((1, D_MODEL)), _const_spec((D_MODEL, C_END), single=True),
                  _const_spec((1, FOX_QK_W)), _const_spec((1, FOX_QK_W)), _const_spec((1, FOX_QK_W)),
                  _const_spec((256, 256)), _const_spec((1, 128)),
                  _const_spec((FOX_BIAS_LANES * 128, FOX_QK_W))],
        out_specs=[row(c) for c, _ in outs],
        out_shape=[jax.ShapeDtypeStruct((n, c), dt) for c, dt in outs],
        scratch_shapes=[pltpu.VMEM((1, 128), F32)],
        compiler_params=_cparams(("arbitrary",)),
        name="inproj",
    )(x, g, w, gq, gk, qone, hsum, fb, place)


def _fox_kernel(jstart_ref, q_ref, k_ref, v_ref, o_ref, m_ref, l_ref, acc_ref, *, tq, nq):
    b, p, i = pl.program_id(0), pl.program_id(1), pl.program_id(2)
    nchunk = tq // 128
    m_ref[...] = jnp.full(m_ref.shape, NEG_BIG, F32)
    l_ref[...] = jnp.zeros(l_ref.shape, F32)
    acc_ref[...] = jnp.zeros(acc_ref.shape, F32)

    def step(j, masked):
        start = pl.multiple_of(j * tq, tq)
        kb = k_ref[pl.ds(start, tq), :]
        vb = v_ref[pl.ds(start, tq), :]
        for hh in range(2):
            sl = slice(hh * FOX_PAD, (hh + 1) * FOX_PAD)
            s = _dot_nt(q_ref[:, sl], kb[:, sl])
            if masked:
                rq = lax.broadcasted_iota(jnp.int32, (tq, tq), 0)
                ck = lax.broadcasted_iota(jnp.int32, (tq, tq), 1)
                s = jnp.where(ck <= rq, s, NEG_BIG)
            mc = s[:, 0:128]
            for c in range(1, nchunk):
                mc = jnp.maximum(mc, s[:, c * 128:(c + 1) * 128])
            m_prev = m_ref[hh]
            m_new = jnp.maximum(m_prev, jnp.max(mc, axis=-1, keepdims=True))
            alpha = jnp.exp2(m_prev - m_new)
            pr = jnp.exp2(s - jnp.concatenate([m_new] * nchunk, axis=1))
            ls = pr[:, 0:128]
            for c in range(1, nchunk):
                ls = ls + pr[:, c * 128:(c + 1) * 128]
            l_ref[hh] = alpha * l_ref[hh] + ls
            acc_ref[hh] = alpha * acc_ref[hh] + _dot(pr.astype(BF16), vb)
            m_ref[hh] = m_new

    def body(j, carry):
        step(j, False)
        return carry

    base = ((b * (FOX_HEADS // 2) + p) * 2) * nq + i
    j0 = jnp.minimum(jstart_ref[base], jstart_ref[base + nq])
    lax.fori_loop(j0, i, body, 0)
    step(i, True)
    lane = lax.broadcasted_iota(jnp.int32, (tq, 128), 1)
    o0 = acc_ref[0] / jnp.sum(l_ref[0], axis=-1, keepdims=True)
    o1 = acc_ref[1] / jnp.sum(l_ref[1], axis=-1, keepdims=True)
    o_ref[...] = jnp.where(lane < FOX_HD, o0, o1).astype(BF16)


def _fox_attention(jstart, q, k, v, bsz, seq, tq):
    n = q.shape[0]
    nq = seq // tq
    grid_spec = pltpu.PrefetchScalarGridSpec(
        num_scalar_prefetch=1,
        grid=(bsz, FOX_HEADS // 2, nq),
        in_specs=[pl.BlockSpec((tq, 2 * FOX_PAD), lambda b, p, i, js: (b * nq + i, p)),
                  pl.BlockSpec((seq, 2 * FOX_PAD), lambda b, p, i, js: (b, p)),
                  pl.BlockSpec((seq, 128), lambda b, p, i, js: (b, p))],
        out_specs=pl.BlockSpec((tq, 128), lambda b, p, i, js: (b * nq + i, p)),
        scratch_shapes=[pltpu.VMEM((2, tq, 128), F32), pltpu.VMEM((2, tq, 128), F32),
                        pltpu.VMEM((2, tq, 128), F32)],
    )
    return pl.pallas_call(
        functools.partial(_fox_kernel, tq=tq, nq=nq),
        grid_spec=grid_spec,
        out_shape=jax.ShapeDtypeStruct((n, 512), BF16),
        compiler_params=_cparams(("parallel", "parallel", "arbitrary")),
        name="fox_attention",
    )(jstart, q, k, v)


def _fox_first_blocks(c, qk_bound, bsz, seq, tq):
    nq = seq // tq
    cb = c.reshape(bsz, nq, tq, 128)
    c_end = cb[:, :, tq - 1, :FOX_HEADS].transpose(0, 2, 1)
    c_start = cb[:, :, 0, :FOX_HEADS].transpose(0, 2, 1)
    gap = c_end[:, :, None, :] - c_start[:, :, :, None]
    skip = gap > (2.0 * qk_bound + FOX_SKIP_NATS)
    jstart = jnp.sum(skip.astype(jnp.int32), axis=-1)
    jstart = jnp.minimum(jstart, jnp.arange(nq, dtype=jnp.int32)[None, None, :])
    return jstart.reshape(-1)


S5_T = 16
S5_CW = S5_T * S5_GROUP
S5_SW = 2 * S5_STATE


def _s5_kernel(u_ref, m_ref, bst_ref, cout_ref, a_ref, y_ref, hin_ref, hb_ref, st_ref, *, rows):
    t = pl.program_id(1)
    ng = S5_GROUPS

    @pl.when(t == 0)
    def _():
        st_ref[...] = jnp.zeros(st_ref.shape, F32)

    for g in range(ng):
        hin_ref[g * rows:(g + 1) * rows, :] = _dot(u_ref[g], bst_ref[g])

    a1 = a_ref[0]
    a2 = a_ref[1]

    def step(c, carry):
        h, hs = carry
        x = hin_ref[pl.ds(c, ng, stride=rows), :]
        hb_ref[pl.ds(pl.multiple_of(c * ng, ng), ng), :] = h
        xs = pltpu.roll(x, S5_STATE, 1)
        return a1 * h + a2 * hs + x, a1 * hs - a2 * h + xs

    h, hs = lax.fori_loop(0, rows, step, (st_ref[0], st_ref[1]), unroll=8)
    st_ref[0] = h
    st_ref[1] = hs

    for g in range(ng):
        hb = hb_ref[pl.ds(g, rows, stride=ng), :].astype(BF16)
        y_ref[g] = (_dot(u_ref[g], m_ref[g]) + _dot(hb, cout_ref[g])).astype(BF16)


def _s5_ssm(u_t, m, bst, cout, a, bsz, rows):
    g, nrows, _ = u_t.shape
    nt = nrows // bsz // rows
    spec = pl.BlockSpec((g, rows, S5_CW), lambda b, t: (0, b * nt + t, 0))
    return pl.pallas_call(
        functools.partial(_s5_kernel, rows=rows),
        grid=(bsz, nt),
        in_specs=[spec, _const_spec((g, S5_CW, S5_CW), single=True), _const_spec((g, S5_CW, S5_SW), single=True),
                  _const_spec((g, S5_SW, S5_CW), single=True), _const_spec((2, g, S5_SW))],
        out_specs=spec,
        out_shape=jax.ShapeDtypeStruct(u_t.shape, BF16),
        scratch_shapes=[pltpu.VMEM((g * rows, S5_SW), F32), pltpu.VMEM((g * rows, S5_SW), F32),
                        pltpu.VMEM((2, g, S5_SW), F32)],
        compiler_params=_cparams(("parallel", "arbitrary")),
        name="s5_ssm",
    )(u_t, m, bst, cout, a)


def _s5_glu_kernel(y_ref, u_ref, d_ref, wglu_ref, bglu_ref, o_ref):
    y = y_ref[...].astype(F32) + d_ref[...] * u_ref[...].astype(F32)
    z = 0.5 * y * (1.0 + jnp.tanh(math.sqrt(2.0 / math.pi) * (y + 0.044715 * (y * y * y))))
    gate = _sigmoid(_dot(z.astype(BF16), wglu_ref[...]) + bglu_ref[...])
    o_ref[...] = (z * gate).astype(BF16)


def _s5_glu(y, u, d, wglu, bglu, tm):
    n = y.shape[0]
    row = pl.BlockSpec((tm, 512), lambda i: (i, 0))
    return pl.pallas_call(
        _s5_glu_kernel,
        grid=(n // tm,),
        in_specs=[row, row, _const_spec((1, 512)), _const_spec((512, 512)), _const_spec((1, 512))],
        out_specs=row,
        out_shape=jax.ShapeDtypeStruct((n, 512), BF16),
        compiler_params=_cparams(("parallel",)),
        name="s5_glu",
    )(y, u, d, wglu, bglu)


def _s5_params(a_re, a_im, b_re, b_im, c_re, c_im, log_dt):
    g, p, gs, tt = S5_GROUPS, S5_STATE, S5_GROUP, S5_T
    dt = jnp.exp(log_dt.astype(F32))[:, None]
    ar, ai = a_re.astype(F32), a_im.astype(F32)
    mag = jnp.exp(dt * ar)
    abar_r, abar_i = mag * jnp.cos(dt * ai), mag * jnp.sin(dt * ai)
    inv_den = 1.0 / (ar * ar + ai * ai)
    nr, ni = abar_r - 1.0, abar_i
    coef_r = (nr * ar + ni * ai) * inv_den
    coef_i = (ni * ar - nr * ai) * inv_den
    br, bi = b_re.astype(F32), b_im.astype(F32)
    bbar_r = coef_r[..., None] * br - coef_i[..., None] * bi
    bbar_i = coef_r[..., None] * bi + coef_i[..., None] * br
    cr, ci = c_re.astype(F32), c_im.astype(F32)

    pows = [(jnp.ones_like(abar_r), jnp.zeros_like(abar_i))]
    for _ in range(tt):
        pr, pi = pows[-1]
        pows.append((pr * abar_r - pi * abar_i, pr * abar_i + pi * abar_r))
    pw_r = jnp.stack([x[0] for x in pows])
    pw_i = jnp.stack([x[1] for x in pows])

    cb_r = cr[:, :, :, None] * bbar_r[:, None, :, :] - ci[:, :, :, None] * bbar_i[:, None, :, :]
    cb_i = cr[:, :, :, None] * bbar_i[:, None, :, :] + ci[:, :, :, None] * bbar_r[:, None, :, :]
    kl = (jnp.einsum('lgp,ghpk->lghk', pw_r[:tt], cb_r, precision=HIGHEST)
          - jnp.einsum('lgp,ghpk->lghk', pw_i[:tt], cb_i, precision=HIGHEST))
    lag = jnp.arange(tt)[None, :] - jnp.arange(tt)[:, None]
    m = jnp.where((lag >= 0)[:, :, None, None, None], kl[jnp.clip(lag, 0, tt - 1)], 0.0)
    m = m.transpose(2, 0, 4, 1, 3).reshape(g, tt * gs, tt * gs)

    e_r, e_i = pw_r[:tt][::-1], pw_i[:tt][::-1]
    bs_r = e_r[:, :, :, None] * bbar_r[None] - e_i[:, :, :, None] * bbar_i[None]
    bs_i = e_r[:, :, :, None] * bbar_i[None] + e_i[:, :, :, None] * bbar_r[None]
    bst = jnp.concatenate([bs_r, bs_i], axis=2).transpose(1, 0, 3, 2).reshape(g, tt * gs, 2 * p)

    q_r, q_i = pw_r[1:tt + 1], pw_i[1:tt + 1]
    co_r = cr[None] * q_r[:, :, None, :] - ci[None] * q_i[:, :, None, :]
    co_i = cr[None] * q_i[:, :, None, :] + ci[None] * q_r[:, :, None, :]
    cout = jnp.concatenate([co_r, -co_i], axis=3).transpose(1, 3, 0, 2).reshape(g, 2 * p, tt * gs)

    a_t = jnp.stack([jnp.concatenate([pw_r[tt], pw_r[tt]], axis=-1),
                     jnp.concatenate([-pw_i[tt], pw_i[tt]], axis=-1)])
    return m.astype(BF16), bst.astype(BF16), cout.astype(BF16), a_t


def _hgrn_kernel(q_ref, f_ref, i_ref, g_ref, loglb_ref, log1mlb_ref, onemlb_ref, gain_ref,
                 o_ref, st_ref, *, tt):
    t = pl.program_id(1)

    @pl.when(t == 0)
    def _():
        st_ref[...] = jnp.zeros(st_ref.shape, F32)

    c_sz, sub = HG_CHUNK, HG_SUB
    n_sub = c_sz // sub
    r64 = lax.broadcasted_iota(jnp.int32, (c_sz, c_sz), 0)
    c64 = lax.broadcasted_iota(jnp.int32, (c_sz, c_sz), 1)
    lower = (c64 <= r64).astype(F32)
    row_s = lax.broadcasted_iota(jnp.int32, (sub, 1), 0)
    lane_s = lax.broadcasted_iota(jnp.int32, (sub, c_sz), 1)
    gain = gain_ref[...]

    def head_chunk(rows, hd):
        sl = slice(hd * HG_D, (hd + 1) * HG_D)
        loglb = loglb_ref[:, sl]
        z = f_ref[rows, sl]
        bb = log1mlb_ref[:, sl] + _log_sigmoid(z)
        logf = jnp.maximum(loglb, bb) + jnp.log(1.0 + jnp.exp(-jnp.abs(loglb - bb)))
        key = onemlb_ref[:, sl] * (1.0 / (1.0 + jnp.exp(z)))
        qx = q_ref[rows, sl].astype(F32)
        qf = qx * _sigmoid(qx)
        vb = i_ref[rows, sl]
        b = jnp.dot(lower, logf * LOG2E, precision=HIGHEST, preferred_element_type=F32)
        b_last = b[c_sz - 1:c_sz, :]
        st = st_ref[hd]
        o_inter = _dot_nt((qf * jnp.exp2(b)).astype(BF16), st.astype(BF16))
        kd = (key * jnp.exp2(b_last - b)).astype(BF16)
        st_ref[hd] = st * jnp.exp2(b_last) + _dot_tn(vb, kd)

        srows = []
        for blk in range(n_sub):
            lo = blk * sub
            b_i = b[lo:lo + sub]
            q_i = qf[lo:lo + sub]
            k_i = key[lo:lo + sub]
            sd = jnp.zeros((sub, c_sz), F32)
            for s in range(sub):
                e = jnp.exp2(jnp.minimum(b_i - b_i[s:s + 1], 0.0))
                col = jnp.sum(q_i * e * k_i[s:s + 1], axis=-1, keepdims=True)
                sd = jnp.where(lane_s == lo + s, col, sd)
            sd = jnp.where(lane_s <= lo + row_s, sd, 0.0)
            if blk > 0:
                ref = b[lo - 1:lo]
                qt = (q_i * jnp.exp2(b_i - ref)).astype(BF16)
                kt = (key * jnp.exp2(jnp.minimum(ref - b, 0.0))).astype(BF16)
                sd = jnp.where(lane_s < lo, _dot_nt(qt, kt), sd)
            srows.append(sd)
        scores = jnp.concatenate(srows, axis=0)
        o = o_inter + _dot(scores.astype(BF16), vb)
        gx = g_ref[rows, sl].astype(F32)
        y = o * lax.rsqrt(jnp.mean(o * o, axis=-1, keepdims=True) + EPS) * gain
        o_ref[rows, sl] = (y * (gx * _sigmoid(gx))).astype(BF16)

    def chunk(c, carry):
        rows = pl.ds(pl.multiple_of(c * c_sz, c_sz), c_sz)
        for hd in range(HG_HEADS):
            head_chunk(rows, hd)
        return carry

    lax.fori_loop(0, tt // c_sz, chunk, 0)


def _hgrn(hq, hf, hi, hg, loglb, log1mlb, onemlb, gain, bsz, seq, tt):
    n = hq.shape[0]
    nt = seq // tt
    spec = pl.BlockSpec((tt, 512), lambda b, t: (b * nt + t, 0))
    return pl.pallas_call(
        functools.partial(_hgrn_kernel, tt=tt),
        grid=(bsz, nt),
        in_specs=[spec, spec, spec, spec, _const_spec((1, 512)), _const_spec((1, 512)),
                  _const_spec((1, 512)), _const_spec((1, HG_D))],
        out_specs=spec,
        out_shape=jax.ShapeDtypeStruct((n, 512), BF16),
        scratch_shapes=[pltpu.VMEM((HG_HEADS, HG_D, HG_D), F32)],
        compiler_params=_cparams(("parallel", "arbitrary")),
        name="hgrn2_mixer",
    )(hq, hf, hi, hg, loglb, log1mlb, onemlb, gain)


def _merge_kernel(x_ref, yf_ref, ys_ref, yh_ref, gl_ref, wb_ref, wo_ref, o_ref):
    m = None
    for n, y_ref in enumerate((yf_ref, ys_ref, yh_ref)):
        gate = _sigmoid(gl_ref[:, n * D_MODEL:(n + 1) * D_MODEL].astype(F32))
        term = gate * _dot(y_ref[...], wb_ref[n * BRANCH_W:(n + 1) * BRANCH_W, :])
        m = term if m is None else m + term
    o_ref[...] = x_ref[...] + _dot(m.astype(BF16), wo_ref[...])


def _merge(x, yf, ys, yh, gl, wb, wo, tm):
    n = x.shape[0]
    row = lambda c: pl.BlockSpec((tm, c), lambda i: (i, 0))
    return pl.pallas_call(
        _merge_kernel,
        grid=(n // tm,),
        in_specs=[row(D_MODEL), row(512), row(512), row(512), row(3 * D_MODEL),
                  _const_spec((3 * BRANCH_W, D_MODEL), single=True),
                  _const_spec((D_MODEL, D_MODEL), single=True)],
        out_specs=row(D_MODEL),
        out_shape=jax.ShapeDtypeStruct((n, D_MODEL), F32),
        compiler_params=_cparams(("parallel",)),
        name="merge_outproj",
    )(x, yf, ys, yh, gl, wb, wo)


def _memkv_kernel(m_ref, g_ref, wk_ref, wv_ref, gk_ref, k_ref, v_ref):
    h = _rms(m_ref[0], g_ref[...]).astype(BF16)
    kk = _dot(h, wk_ref[...])
    for hd in range(X_HEADS):
        sl = slice(hd * X_HD, (hd + 1) * X_HD)
        k_ref[0, :, sl] = _rms(kk[:, sl], gk_ref[...]).astype(BF16)
    v_ref[0] = _dot(h, wv_ref[...]).astype(BF16)


def _memkv(mem, g, wk, wv, gk):
    bsz, nm, _ = mem.shape
    spec = pl.BlockSpec((1, nm, D_MODEL), lambda b: (b, 0, 0))
    return pl.pallas_call(
        _memkv_kernel,
        grid=(bsz,),
        in_specs=[spec, _const_spec((1, D_MODEL)), _const_spec((D_MODEL, D_MODEL)),
                  _const_spec((D_MODEL, D_MODEL)), _const_spec((1, X_HD))],
        out_specs=[spec, spec],
        out_shape=[jax.ShapeDtypeStruct(mem.shape, BF16)] * 2,
        compiler_params=_cparams(("parallel",)),
        name="mem_kv",
    )(mem, g, wk, wv, gk)


def _xattn_kernel(x_ref, g_ref, wq_ref, gq_ref, k_ref, v_ref, wo_ref, o_ref):
    x = x_ref[...]
    h = _rms(x, g_ref[...]).astype(BF16)
    q = _dot(h, wq_ref[...])
    outs = []
    for hd in range(X_HEADS):
        sl = slice(hd * X_HD, (hd + 1) * X_HD)
        qh = (_rms(q[:, sl], gq_ref[...]) * (X_HD ** -0.5)).astype(BF16)
        s = _dot_nt(qh, k_ref[0, :, sl])
        p = jnp.exp(s - jnp.max(s, axis=-1, keepdims=True))
        l = jnp.sum(p, axis=-1, keepdims=True)
        outs.append((_dot(p.astype(BF16), v_ref[0, :, sl]) / l).astype(BF16))
    o_ref[...] = x + _dot(jnp.concatenate(outs, axis=1), wo_ref[...])


def _xattn(x, g, wq, gq, km, vm, wo, seq, tm):
    n = x.shape[0]
    nm = km.shape[1]
    per_b = seq // tm
    row = pl.BlockSpec((tm, D_MODEL), lambda i: (i, 0))
    kv = pl.BlockSpec((1, nm, D_MODEL), lambda i: (i // per_b, 0, 0))
    return pl.pallas_call(
        _xattn_kernel,
        grid=(n // tm,),
        in_specs=[row, _const_spec((1, D_MODEL)), _const_spec((D_MODEL, D_MODEL), single=True),
                  _const_spec((1, X_HD)), kv, kv, _const_spec((D_MODEL, D_MODEL), single=True)],
        out_specs=row,
        out_shape=jax.ShapeDtypeStruct((n, D_MODEL), F32),
        compiler_params=_cparams(("parallel",)),
        name="cross_attention",
    )(x, g, wq, gq, km, vm, wo)


FF_CHUNK = 256


def _ffn_kernel(x_ref, g_ref, wgu_ref, wd_ref, o_ref, act_ref):
    x = x_ref[...]
    h = _rms(x, g_ref[...]).astype(BF16)
    for c in range(0, D_FF, FF_CHUNK):
        a = _dot(h, wgu_ref[:, c:c + FF_CHUNK])
        b = _dot(h, wgu_ref[:, D_FF + c:D_FF + c + FF_CHUNK])
        act_ref[:, c:c + FF_CHUNK] = (a * _sigmoid(a) * b).astype(BF16)
    o_ref[...] = x + _dot(act_ref[...], wd_ref[...])


def _ffn(x, g, wgu, wd, tm):
    n = x.shape[0]
    row = pl.BlockSpec((tm, D_MODEL), lambda i: (i, 0))
    return pl.pallas_call(
        _ffn_kernel,
        grid=(n // tm,),
        in_specs=[row, _const_spec((1, D_MODEL)), _const_spec((D_MODEL, 2 * D_FF), single=True),
                  _const_spec((D_FF, D_MODEL), single=True)],
        out_specs=row,
        out_shape=jax.ShapeDtypeStruct((n, D_MODEL), F32),
        scratch_shapes=[pltpu.VMEM((tm, D_FF), BF16)],
        compiler_params=_cparams(("parallel",)),
        name="swiglu",
    )(x, g, wgu, wd)


def _reorder_w_in(w_in):
    depth = w_in.shape[0]

    def pad_heads(w):
        w = w.reshape(depth, D_MODEL, FOX_HEADS, FOX_HD)
        w = jnp.pad(w, ((0, 0), (0, 0), (0, 0), (0, FOX_PAD - FOX_HD)))
        return w.reshape(depth, D_MODEL, FOX_QK_W)

    fq, fk, fv = w_in[:, :, 0:512], w_in[:, :, 512:1024], w_in[:, :, 1024:1536]
    ff = w_in[:, :, 1536:1544]
    rest = w_in[:, :, 1544:]
    pad = jnp.zeros((depth, D_MODEL, C_END - C_FF - FOX_HEADS), w_in.dtype)
    return jnp.concatenate([pad_heads(fq), pad_heads(fk), fv, rest, ff, pad], axis=-1).astype(BF16)


def _pad_head_vec(v, fill=0.0):
    v = jnp.concatenate([v.astype(F32), jnp.full((FOX_PAD - FOX_HD,), fill, F32)])
    return jnp.tile(v, FOX_HEADS).reshape(1, FOX_QK_W)


def kernel(x, mem, norm_mix, w_in, fox_fbias, fox_qnorm, fox_knorm, s5_a_re, s5_a_im, s5_b_re, s5_b_im, s5_c_re, s5_c_im, s5_d, s5_log_dt, s5_w_glu, s5_b_glu, hg_lb, hg_onorm, w_branch, w_out, norm_x, norm_mem, xq, xk, xv, xo, x_qnorm, x_knorm, norm_ffn, w_gate_up, w_down):
    bsz, seq, _ = x.shape
    depth = w_in.shape[0]
    n = bsz * seq
    tm_in = min(512, seq)
    tm = min(1024, seq)
    tq = min(512, seq)
    tt = min(256, seq)
    s5_rows = min(256, seq // S5_T)
    assert seq % tm == 0 and seq % tq == 0 and seq % tt == 0 and tt % HG_CHUNK == 0
    assert (seq // S5_T) % s5_rows == 0

    row = lambda v: v.astype(F32).reshape(1, -1)
    w_in_r = _reorder_w_in(w_in)
    s5_wglu = s5_w_glu.astype(BF16)
    wb, wo = w_branch.astype(BF16), w_out.astype(BF16)
    wq, wk, wv, wxo = xq.astype(BF16), xk.astype(BF16), xv.astype(BF16), xo.astype(BF16)
    wgu, wd = w_gate_up.astype(BF16), w_down.astype(BF16)

    lb_all = jnp.cumsum(jax.nn.softmax(hg_lb.astype(F32), axis=0), axis=0)
    lb_all = lb_all - lb_all[0:1]

    lane = jnp.arange(256)
    hsum = ((lane[:, None] // FOX_PAD == lane[None, :] // FOX_PAD)
            & (lane[:, None] % FOX_PAD < FOX_HD)).astype(BF16)
    qone = _pad_head_vec(jnp.zeros((FOX_HD,), F32)).at[0, :].set(
        jnp.tile((jnp.arange(FOX_PAD) >= FOX_HD) & (jnp.arange(FOX_PAD) < FOX_HD + FOX_BIAS_LANES),
                 FOX_HEADS).astype(F32))
    src = jnp.arange(FOX_BIAS_LANES * 128)
    dst = jnp.arange(FOX_QK_W)
    place = ((src[:, None] % 128 == dst[None, :] // FOX_PAD)
             & (dst[None, :] % FOX_PAD == FOX_HD + src[:, None] // 128)).astype(BF16)

    xf = x.astype(F32).reshape(n, D_MODEL)
    for l in range(depth):
        gq = _pad_head_vec(fox_qnorm[l]) * (FOX_HD ** -0.5 * LOG2E)
        gk = _pad_head_vec(fox_knorm[l])
        fb = jnp.pad(fox_fbias[l].astype(F32), (0, 128 - FOX_HEADS)).reshape(1, 128)
        q, k, v, su, hq, hf, hi, hg, gl, c = _inproj(xf, row(norm_mix[l]), w_in_r[l], gq, gk, qone, hsum,
                                                     fb, place, seq, tm_in)

        qk_bound = 1.01 * FOX_HD ** 0.5 * jnp.max(jnp.abs(fox_qnorm[l])) * jnp.max(jnp.abs(fox_knorm[l]))
        jstart = _fox_first_blocks(c, qk_bound, bsz, seq, tq)
        y_fox = _fox_attention(jstart, q, k, v, bsz, seq, tq)

        s5_m, s5_bst, s5_cout, s5_at = _s5_params(s5_a_re[l], s5_a_im[l], s5_b_re[l], s5_b_im[l],
                                                  s5_c_re[l], s5_c_im[l], s5_log_dt[l])
        u_t = su.reshape(n // S5_T, S5_T, S5_GROUPS, S5_GROUP).transpose(2, 0, 1, 3)
        y_t = _s5_ssm(u_t.reshape(S5_GROUPS, n // S5_T, S5_CW), s5_m, s5_bst, s5_cout, s5_at, bsz, s5_rows)
        y_ssm = y_t.reshape(S5_GROUPS, n // S5_T, S5_T, S5_GROUP).transpose(1, 2, 0, 3).reshape(n, BRANCH_W)
        y_s5 = _s5_glu(y_ssm, su, row(s5_d[l]), s5_wglu[l], row(s5_b_glu[l]), tm)

        lb = lb_all[l].reshape(1, -1)
        y_hg = _hgrn(hq, hf, hi, hg, jnp.log(lb), jnp.log1p(-lb), 1.0 - lb, row(hg_onorm[l]), bsz, seq, tt)

        xf = _merge(xf, y_fox, y_s5, y_hg, gl, wb[l], wo[l], tm)

        km, vm = _memkv(mem.astype(F32), row(norm_mem[l]), wk[l], wv[l], row(x_knorm[l]))
        xf = _xattn(xf, row(norm_x[l]), wq[l], row(x_qnorm[l]), km, vm, wxo[l], seq, tm)
        xf = _ffn(xf, row(norm_ffn[l]), wgu[l], wd[l], tm)
    return xf.reshape(bsz, seq, D_MODEL).astype(x.dtype)
```

```python
import functools
import math

import jax
import jax.numpy as jnp
from jax import lax
from jax.experimental import pallas as pl
from jax.experimental.pallas import tpu as pltpu

F32 = jnp.float32
BF16 = jnp.bfloat16
HIGHEST = lax.Precision.HIGHEST

D_MODEL = 1024
BRANCH_W = 512
FOX_HD = 64
FOX_HEADS = 8
S5_GROUP = 16
S5_GROUPS = 32
S5_STATE = 64
S5_NSTATE = S5_GROUPS * S5_STATE
HG_HEADS = 4
HG_D = 128
HG_CHUNK = 64
HG_SUB = 16
X_HEADS = 4
X_HD = 256
D_FF = 2816
EPS = 1e-6

VMEM_LIMIT_BYTES = 56 * 1024 * 1024

FOX_PAD = 128
FOX_QK_W = FOX_HEADS * FOX_PAD
C_FQ, C_FK, C_FV, C_SU, C_HQ, C_HF, C_HI, C_HG, C_GL, C_FF, C_END = (
    0, 1024, 2048, 2560, 3072, 3584, 4096, 4608, 5120, 8192, 8320)
FOX_BIAS_LANES = 3
LOG2E = 1.4426950408889634
FOX_SKIP_NATS = 30.0


NEG_BIG = -1e30


def _cparams(sem):
    return pltpu.CompilerParams(dimension_semantics=sem, vmem_limit_bytes=VMEM_LIMIT_BYTES)


def _rms(xf, g):
    return xf * lax.rsqrt(jnp.mean(xf * xf, axis=-1, keepdims=True) + EPS) * g


def _sigmoid(x):
    return 1.0 / (1.0 + jnp.exp(-x))


def _log_sigmoid(x):
    return jnp.minimum(x, 0.0) - jnp.log(1.0 + jnp.exp(-jnp.abs(x)))


def _dot(a, b):
    return jnp.dot(a, b, preferred_element_type=F32)


def _dot_nt(a, b):
    return lax.dot_general(a, b, (((1,), (1,)), ((), ())), preferred_element_type=F32)


def _dot_tn(a, b):
    return lax.dot_general(a, b, (((0,), (0,)), ((), ())), preferred_element_type=F32)


def _const_spec(shape, single=False):
    nd = len(shape)
    if single:
        return pl.BlockSpec(shape, lambda *_: (0,) * nd, pipeline_mode=pl.Buffered(1))
    return pl.BlockSpec(shape, lambda *_: (0,) * nd)


def _top16(v):
    bits = lax.bitcast_convert_type(v, jnp.uint32) & jnp.uint32(0xFFFF0000)
    return lax.bitcast_convert_type(bits, F32)


def _inproj_kernel(x_ref, g_ref, w_ref, gq_ref, gk_ref, qone_ref, hsum_ref, fb_ref, place_ref,
                   q_ref, k_ref, v_ref, su_ref, hq_ref, hf_ref, hi_ref, hg_ref, gl_ref, c_ref,
                   carry_ref, *, tiles_per_seq):
    tm = x_ref.shape[0]

    @pl.when(pl.program_id(0) % tiles_per_seq == 0)
    def _():
        carry_ref[...] = jnp.zeros(carry_ref.shape, F32)

    h = _rms(x_ref[...], g_ref[...]).astype(BF16)

    def proj(lo, hi):
        return _dot(h, w_ref[:, lo:hi])

    def sumsq(t):
        return _dot((t * t).astype(BF16), hsum_ref[...])

    def headnorm(t, ss, g):
        return t * lax.rsqrt(ss * (1.0 / FOX_HD) + EPS) * g

    lf = _log_sigmoid(proj(C_FF, C_END) + fb_ref[...])
    r = lax.broadcasted_iota(jnp.int32, (tm, tm), 0)
    cc = lax.broadcasted_iota(jnp.int32, (tm, tm), 1)
    lower = (cc <= r).astype(F32)
    cs = jnp.dot(lower, lf, precision=HIGHEST, preferred_element_type=F32) + carry_ref[...]
    carry_ref[...] = cs[tm - 1:tm, :]
    c_ref[...] = cs
    d = -(cs * LOG2E)
    hi = _top16(d)
    r1 = d - hi
    mid = _top16(r1)
    pieces = jnp.concatenate([hi, mid, r1 - mid], axis=1).astype(BF16)
    kbias = _dot(pieces, place_ref[...])

    chunks = range(0, FOX_QK_W, 256)
    tq = [proj(C_FQ + c, C_FQ + c + 256) for c in chunks]
    tk = [proj(C_FK + c, C_FK + c + 256) for c in chunks]
    sq = [sumsq(t) for t in tq]
    sk = [sumsq(t) for t in tk]
    for n, c in enumerate(chunks):
        sl = slice(c, c + 256)
        q_ref[:, sl] = (headnorm(tq[n], sq[n], gq_ref[:, sl]) + qone_ref[:, sl]).astype(BF16)
        k_ref[:, sl] = (headnorm(tk[n], sk[n], gk_ref[:, sl]) + kbias[:, sl]).astype(BF16)
    v_ref[...] = proj(C_FV, C_SU).astype(BF16)
    su_ref[...] = proj(C_SU, C_HQ).astype(BF16)
    hq_ref[...] = proj(C_HQ, C_HF).astype(BF16)
    hf_ref[...] = proj(C_HF, C_HI)
    hi_ref[...] = proj(C_HI, C_HG).astype(BF16)
    hg_ref[...] = proj(C_HG, C_GL).astype(BF16)
    for c in range(C_GL, C_FF, 512):
        gl_ref[:, c - C_GL:c - C_GL + 512] = proj(c, c + 512).astype(BF16)


def _inproj(x, g, w, gq, gk, qone, hsum, fb, place, seq, tm):
    n = x.shape[0]
    row = lambda c: pl.BlockSpec((tm, c), lambda i: (i, 0))
    outs = ([(FOX_QK_W, BF16)] * 2 + [(512, BF16)] * 3 + [(512, F32)] + [(512, BF16)] * 2
            + [(3072, BF16), (128, F32)])
    return pl.pallas_call(
        functools.partial(_inproj_kernel, tiles_per_seq=seq // tm),
        grid=(n // tm,),
        in_specs=[row(D_MODEL), _const_spec((1, D_MODEL)), _const_spec((D_MODEL, C_END), single=True),
                  _const_spec((1, FOX_QK_W)), _const_spec((1, FOX_QK_W)), _const_spec((1, FOX_QK_W)),
                  _const_spec((256, 256)), _const_spec((1, 128)),
                  _const_spec((FOX_BIAS_LANES * 128, FOX_QK_W))],
        out_specs=[row(c) for c, _ in outs],
        out_shape=[jax.ShapeDtypeStruct((n, c), dt) for c, dt in outs],
        scratch_shapes=[pltpu.VMEM((1, 128), F32)],
        compiler_params=_cparams(("arbitrary",)),
        name="inproj",
    )(x, g, w, gq, gk, qone, hsum, fb, place)


def _fox_kernel(jstart_ref, q_ref, k_ref, v_ref, o_ref, m_ref, l_ref, acc_ref, *, tq, nq):
    b, p, i = pl.program_id(0), pl.program_id(1), pl.program_id(2)
    nchunk = tq // 128
    m_ref[...] = jnp.full(m_ref.shape, NEG_BIG, F32)
    l_ref[...] = jnp.zeros(l_ref.shape, F32)
    acc_ref[...] = jnp.zeros(acc_ref.shape, F32)

    def step(j, masked):
        start = pl.multiple_of(j * tq, tq)
        kb = k_ref[pl.ds(start, tq), :]
        vb = v_ref[pl.ds(start, tq), :]
        for hh in range(2):
            sl = slice(hh * FOX_PAD, (hh + 1) * FOX_PAD)
            s = _dot_nt(q_ref[:, sl], kb[:, sl])
            if masked:
                rq = lax.broadcasted_iota(jnp.int32, (tq, tq), 0)
                ck = lax.broadcasted_iota(jnp.int32, (tq, tq), 1)
                s = jnp.where(ck <= rq, s, NEG_BIG)
            mc = s[:, 0:128]
            for c in range(1, nchunk):
                mc = jnp.maximum(mc, s[:, c * 128:(c + 1) * 128])
            m_prev = m_ref[hh]
            m_new = jnp.maximum(m_prev, jnp.max(mc, axis=-1, keepdims=True))
            alpha = jnp.exp2(m_prev - m_new)
            pr = jnp.exp2(s - jnp.concatenate([m_new] * nchunk, axis=1))
            ls = pr[:, 0:128]
            for c in range(1, nchunk):
                ls = ls + pr[:, c * 128:(c + 1) * 128]
            l_ref[hh] = alpha * l_ref[hh] + ls
            acc_ref[hh] = alpha * acc_ref[hh] + _dot(pr.astype(BF16), vb)
            m_ref[hh] = m_new

    def body(j, carry):
        step(j, False)
        return carry

    base = ((b * (FOX_HEADS // 2) + p) * 2) * nq + i
    j0 = jnp.minimum(jstart_ref[base], jstart_ref[base + nq])
    lax.fori_loop(j0, i, body, 0)
    step(i, True)
    lane = lax.broadcasted_iota(jnp.int32, (tq, 128), 1)
    o0 = acc_ref[0] / jnp.sum(l_ref[0], axis=-1, keepdims=True)
    o1 = acc_ref[1] / jnp.sum(l_ref[1], axis=-1, keepdims=True)
    o_ref[...] = jnp.where(lane < FOX_HD, o0, o1).astype(BF16)


def _fox_attention(jstart, q, k, v, bsz, seq, tq):
    n = q.shape[0]
    nq = seq // tq
    grid_spec = pltpu.PrefetchScalarGridSpec(
        num_scalar_prefetch=1,
        grid=(bsz, FOX_HEADS // 2, nq),
        in_specs=[pl.BlockSpec((tq, 2 * FOX_PAD), lambda b, p, i, js: (b * nq + i, p)),
                  pl.BlockSpec((seq, 2 * FOX_PAD), lambda b, p, i, js: (b, p)),
                  pl.BlockSpec((seq, 128), lambda b, p, i, js: (b, p))],
        out_specs=pl.BlockSpec((tq, 128), lambda b, p, i, js: (b * nq + i, p)),
        scratch_shapes=[pltpu.VMEM((2, tq, 128), F32), pltpu.VMEM((2, tq, 128), F32),
                        pltpu.VMEM((2, tq, 128), F32)],
    )
    return pl.pallas_call(
        functools.partial(_fox_kernel, tq=tq, nq=nq),
        grid_spec=grid_spec,
        out_shape=jax.ShapeDtypeStruct((n, 512), BF16),
        compiler_params=_cparams(("parallel", "parallel", "arbitrary")),
        name="fox_attention",
    )(jstart, q, k, v)


def _fox_first_blocks(c, qk_bound, bsz, seq, tq):
    nq = seq // tq
    cb = c.reshape(bsz, nq, tq, 128)
    c_end = cb[:, :, tq - 1, :FOX_HEADS].transpose(0, 2, 1)
    c_start = cb[:, :, 0, :FOX_HEADS].transpose(0, 2, 1)
    gap = c_end[:, :, None, :] - c_start[:, :, :, None]
    skip = gap > (2.0 * qk_bound + FOX_SKIP_NATS)
    jstart = jnp.sum(skip.astype(jnp.int32), axis=-1)
    jstart = jnp.minimum(jstart, jnp.arange(nq, dtype=jnp.int32)[None, None, :])
    return jstart.reshape(-1)


S5_T = 8
S5_BLK = 128
S5_NBLK = BRANCH_W // S5_BLK
S5_KW = S5_T * S5_BLK
S5_BST = S5_NSTATE // S5_NBLK


def _s5_kernel(u_ref, w_ref, bst_ref, cout_ref, pw_ref, d_ref, wglu_ref, bglu_ref, o_ref,
               uf_ref, y_ref, hr_ref, hi_ref, cr_ref, ci_ref, *, tt):
    t = pl.program_id(1)
    rows = tt // S5_T

    @pl.when(t == 0)
    def _():
        cr_ref[...] = jnp.zeros(cr_ref.shape, F32)
        ci_ref[...] = jnp.zeros(ci_ref.shape, F32)

    for j in range(S5_NBLK):
        uf_ref[j] = u_ref[:, j * S5_BLK:(j + 1) * S5_BLK].astype(F32)

    def chunk_rows(j):
        return jnp.concatenate([uf_ref[j, pl.ds(s, rows, stride=S5_T), :] for s in range(S5_T)],
                               axis=1).astype(BF16)

    for j in range(S5_NBLK):
        hin = _dot(chunk_rows(j), bst_ref[j])
        hr_ref[:, j * S5_BST:(j + 1) * S5_BST] = hin[:, :S5_BST]
        hi_ref[:, j * S5_BST:(j + 1) * S5_BST] = hin[:, S5_BST:]

    row8 = lax.broadcasted_iota(jnp.int32, (8, S5_NSTATE), 0)

    def scan_block(r, carry):
        cr, ci = carry
        rws = pl.ds(pl.multiple_of(r * 8, 8), 8)
        xr = hr_ref[rws, :]
        xi = hi_ref[rws, :]
        for n, k in enumerate((1, 2, 4)):
            ar = pw_ref[2 * n]
            ai = pw_ref[2 * n + 1]
            sr = pltpu.roll(xr, k, 0)
            si = pltpu.roll(xi, k, 0)
            xr, xi = xr + ar * sr - ai * si, xi + ar * si + ai * sr
        pr = pw_ref[6]
        pi = pw_ref[7]
        xr, xi = xr + pr * cr - pi * ci, xi + pr * ci + pi * cr
        hr_ref[rws, :] = jnp.where(row8 == 0, cr, pltpu.roll(xr, 1, 0))
        hi_ref[rws, :] = jnp.where(row8 == 0, ci, pltpu.roll(xi, 1, 0))
        return (jnp.broadcast_to(xr[7:8, :], xr.shape), jnp.broadcast_to(xi[7:8, :], xi.shape))

    cr, ci = lax.fori_loop(0, rows // 8, scan_block, (cr_ref[...], ci_ref[...]))
    cr_ref[...] = cr
    ci_ref[...] = ci

    for j in range(S5_NBLK):
        st = slice(j * S5_BST, (j + 1) * S5_BST)
        hb = jnp.concatenate([hr_ref[:, st], hi_ref[:, st]], axis=1).astype(BF16)
        yj = _dot(chunk_rows(j), w_ref[j]) + _dot(hb, cout_ref[j])
        for s in range(S5_T):
            y_ref[j, pl.ds(s, rows, stride=S5_T), :] = yj[:, s * S5_BLK:(s + 1) * S5_BLK]

    y = jnp.concatenate([y_ref[j] + d_ref[:, j * S5_BLK:(j + 1) * S5_BLK] * uf_ref[j]
                         for j in range(S5_NBLK)], axis=1)
    z = 0.5 * y * (1.0 + jnp.tanh(math.sqrt(2.0 / math.pi) * (y + 0.044715 * (y * y * y))))
    gate = _sigmoid(_dot(z.astype(BF16), wglu_ref[...]) + bglu_ref[...])
    o_ref[...] = (z * gate).astype(BF16)


def _s5(u, w, bst, cout, pw, d, wglu, bglu, bsz, seq, tt):
    n = u.shape[0]
    nt = seq // tt
    rowspec = pl.BlockSpec((tt, BRANCH_W), lambda b, t: (b * nt + t, 0))
    return pl.pallas_call(
        functools.partial(_s5_kernel, tt=tt),
        grid=(bsz, nt),
        in_specs=[rowspec,
                  _const_spec((S5_NBLK, S5_KW, S5_KW), single=True),
                  _const_spec((S5_NBLK, S5_KW, 2 * S5_BST), single=True),
                  _const_spec((S5_NBLK, 2 * S5_BST, S5_KW), single=True),
                  _const_spec((8, 8, S5_NSTATE), single=True), _const_spec((1, BRANCH_W)),
                  _const_spec((BRANCH_W, BRANCH_W)), _const_spec((1, BRANCH_W))],
        out_specs=rowspec,
        out_shape=jax.ShapeDtypeStruct((n, BRANCH_W), BF16),
        scratch_shapes=[pltpu.VMEM((S5_NBLK, tt, S5_BLK), F32), pltpu.VMEM((S5_NBLK, tt, S5_BLK), F32),
                        pltpu.VMEM((tt // S5_T, S5_NSTATE), F32), pltpu.VMEM((tt // S5_T, S5_NSTATE), F32),
                        pltpu.VMEM((8, S5_NSTATE), F32), pltpu.VMEM((8, S5_NSTATE), F32)],
        compiler_params=_cparams(("parallel", "arbitrary")),
        name="s5_mixer",
    )(u, w, bst, cout, pw, d, wglu, bglu)


def _s5_params(a_re, a_im, b_re, b_im, c_re, c_im, log_dt):
    g, p, gs, tt = S5_GROUPS, S5_STATE, S5_GROUP, S5_T
    dt = jnp.exp(log_dt.astype(F32))[:, None]
    ar, ai = a_re.astype(F32), a_im.astype(F32)
    mag = jnp.exp(dt * ar)
    abar_r, abar_i = mag * jnp.cos(dt * ai), mag * jnp.sin(dt * ai)
    inv_den = 1.0 / (ar * ar + ai * ai)
    nr, ni = abar_r - 1.0, abar_i
    coef_r = (nr * ar + ni * ai) * inv_den
    coef_i = (ni * ar - nr * ai) * inv_den
    br, bi = b_re.astype(F32), b_im.astype(F32)
    bbar_r = coef_r[..., None] * br - coef_i[..., None] * bi
    bbar_i = coef_r[..., None] * bi + coef_i[..., None] * br
    cr, ci = c_re.astype(F32), c_im.astype(F32)

    pows = [(jnp.ones_like(abar_r), jnp.zeros_like(abar_i))]
    for _ in range(tt):
        pr, pi = pows[-1]
        pows.append((pr * abar_r - pi * abar_i, pr * abar_i + pi * abar_r))
    pw_r = jnp.stack([x[0] for x in pows])
    pw_i = jnp.stack([x[1] for x in pows])

    cb_r = cr[:, :, :, None] * bbar_r[:, None, :, :] - ci[:, :, :, None] * bbar_i[:, None, :, :]
    cb_i = cr[:, :, :, None] * bbar_i[:, None, :, :] + ci[:, :, :, None] * bbar_r[:, None, :, :]
    kl = (jnp.einsum('lgp,ghpk->lghk', pw_r[:tt], cb_r, precision=HIGHEST)
          - jnp.einsum('lgp,ghpk->lghk', pw_i[:tt], cb_i, precision=HIGHEST))
    nb, gl = S5_NBLK, g // S5_NBLK
    eye = jnp.eye(gl, dtype=F32)

    def blockdiag(x):
        expand = x[:, :, None] * eye.reshape((1, gl, gl) + (1,) * (x.ndim - 2))
        return expand

    lag = jnp.arange(tt)[None, :] - jnp.arange(tt)[:, None]
    m = jnp.where((lag >= 0)[:, :, None, None, None], kl[jnp.clip(lag, 0, tt - 1)], 0.0)
    m = m.transpose(2, 0, 4, 1, 3).reshape(nb, gl, tt, gs, tt, gs)
    w = blockdiag(m).transpose(0, 3, 1, 4, 5, 2, 6)
    w = w.reshape(nb, tt * gl * gs, tt * gl * gs)

    e_r, e_i = pw_r[:tt][::-1], pw_i[:tt][::-1]
    bs_r = e_r[:, :, :, None] * bbar_r[None] - e_i[:, :, :, None] * bbar_i[None]
    bs_i = e_r[:, :, :, None] * bbar_i[None] + e_i[:, :, :, None] * bbar_r[None]
    bs = jnp.stack([bs_r, bs_i]).transpose(2, 1, 4, 0, 3).reshape(nb, gl, tt, gs, 2, p)
    bst = blockdiag(bs).transpose(0, 3, 1, 4, 5, 2, 6)
    bst = bst.reshape(nb, tt * gl * gs, 2 * gl * p)

    q_r, q_i = pw_r[1:tt + 1], pw_i[1:tt + 1]
    co_r = cr[None] * q_r[:, :, None, :] - ci[None] * q_i[:, :, None, :]
    co_i = cr[None] * q_i[:, :, None, :] + ci[None] * q_r[:, :, None, :]
    co = jnp.stack([co_r, -co_i]).transpose(2, 0, 4, 1, 3).reshape(nb, gl, 2, p, tt, gs)
    cout = blockdiag(co).transpose(0, 3, 1, 4, 5, 2, 6)
    cout = cout.reshape(nb, 2 * gl * p, tt * gl * gs)

    ar1, ai1 = pw_r[tt].reshape(-1), pw_i[tt].reshape(-1)
    apow = [(ar1, ai1)]
    for _ in range(7):
        pr, pi = apow[-1]
        apow.append((pr * ar1 - pi * ai1, pr * ai1 + pi * ar1))
    rows8 = jnp.arange(8)[:, None]
    tabs = []
    for k in (1, 2, 4):
        mask = (rows8 >= k).astype(F32)
        tabs += [mask * apow[k - 1][0][None, :], mask * apow[k - 1][1][None, :]]
    tabs.append(jnp.stack([apow[r][0] for r in range(8)]))
    tabs.append(jnp.stack([apow[r][1] for r in range(8)]))
    pw = jnp.stack(tabs)
    return w.astype(BF16), bst.astype(BF16), cout.astype(BF16), pw


def _hgrn_kernel(q_ref, f_ref, i_ref, g_ref, loglb_ref, log1mlb_ref, onemlb_ref, gain_ref,
                 o_ref, st_ref, *, tt):
    t = pl.program_id(1)

    @pl.when(t == 0)
    def _():
        st_ref[...] = jnp.zeros(st_ref.shape, F32)

    c_sz, sub = HG_CHUNK, HG_SUB
    n_sub = c_sz // sub
    r64 = lax.broadcasted_iota(jnp.int32, (c_sz, c_sz), 0)
    c64 = lax.broadcasted_iota(jnp.int32, (c_sz, c_sz), 1)
    lower = (c64 <= r64).astype(F32)
    row_s = lax.broadcasted_iota(jnp.int32, (sub, 1), 0)
    lane_s = lax.broadcasted_iota(jnp.int32, (sub, c_sz), 1)
    gain = gain_ref[...]

    def head_chunk(rows, hd):
        sl = slice(hd * HG_D, (hd + 1) * HG_D)
        loglb = loglb_ref[:, sl]
        z = f_ref[rows, sl]
        bb = log1mlb_ref[:, sl] + _log_sigmoid(z)
        logf = jnp.maximum(loglb, bb) + jnp.log(1.0 + jnp.exp(-jnp.abs(loglb - bb)))
        key = onemlb_ref[:, sl] * (1.0 / (1.0 + jnp.exp(z)))
        qx = q_ref[rows, sl].astype(F32)
        qf = qx * _sigmoid(qx)
        vb = i_ref[rows, sl]
        b = jnp.dot(lower, logf * LOG2E, precision=HIGHEST, preferred_element_type=F32)
        b_last = b[c_sz - 1:c_sz, :]
        st = st_ref[hd]
        o_inter = _dot_nt((qf * jnp.exp2(b)).astype(BF16), st.astype(BF16))
        kd = (key * jnp.exp2(b_last - b)).astype(BF16)
        st_ref[hd] = st * jnp.exp2(b_last) + _dot_tn(vb, kd)

        srows = []
        for blk in range(n_sub):
            lo = blk * sub
            b_i = b[lo:lo + sub]
            q_i = qf[lo:lo + sub]
            k_i = key[lo:lo + sub]
            sd = jnp.zeros((sub, c_sz), F32)
            for s in range(sub):
                e = jnp.exp2(jnp.minimum(b_i - b_i[s:s + 1], 0.0))
                col = jnp.sum(q_i * e * k_i[s:s + 1], axis=-1, keepdims=True)
                sd = jnp.where(lane_s == lo + s, col, sd)
            sd = jnp.where(lane_s <= lo + row_s, sd, 0.0)
            if blk > 0:
                ref = b[lo - 1:lo]
                qt = (q_i * jnp.exp2(b_i - ref)).astype(BF16)
                kt = (key * jnp.exp2(jnp.minimum(ref - b, 0.0))).astype(BF16)
                sd = jnp.where(lane_s < lo, _dot_nt(qt, kt), sd)
            srows.append(sd)
        scores = jnp.concatenate(srows, axis=0)
        o = o_inter + _dot(scores.astype(BF16), vb)
        gx = g_ref[rows, sl].astype(F32)
        y = o * lax.rsqrt(jnp.mean(o * o, axis=-1, keepdims=True) + EPS) * gain
        o_ref[rows, sl] = (y * (gx * _sigmoid(gx))).astype(BF16)

    def chunk(c, carry):
        rows = pl.ds(pl.multiple_of(c * c_sz, c_sz), c_sz)
        for hd in range(HG_HEADS):
            head_chunk(rows, hd)
        return carry

    lax.fori_loop(0, tt // c_sz, chunk, 0)


def _hgrn(hq, hf, hi, hg, loglb, log1mlb, onemlb, gain, bsz, seq, tt):
    n = hq.shape[0]
    nt = seq // tt
    spec = pl.BlockSpec((tt, 512), lambda b, t: (b * nt + t, 0))
    return pl.pallas_call(
        functools.partial(_hgrn_kernel, tt=tt),
        grid=(bsz, nt),
        in_specs=[spec, spec, spec, spec, _const_spec((1, 512)), _const_spec((1, 512)),
                  _const_spec((1, 512)), _const_spec((1, HG_D))],
        out_specs=spec,
        out_shape=jax.ShapeDtypeStruct((n, 512), BF16),
        scratch_shapes=[pltpu.VMEM((HG_HEADS, HG_D, HG_D), F32)],
        compiler_params=_cparams(("parallel", "arbitrary")),
        name="hgrn2_mixer",
    )(hq, hf, hi, hg, loglb, log1mlb, onemlb, gain)


def _merge_kernel(x_ref, yf_ref, ys_ref, yh_ref, gl_ref, wb_ref, wo_ref, o_ref):
    m = None
    for n, y_ref in enumerate((yf_ref, ys_ref, yh_ref)):
        gate = _sigmoid(gl_ref[:, n * D_MODEL:(n + 1) * D_MODEL].astype(F32))
        term = gate * _dot(y_ref[...], wb_ref[n * BRANCH_W:(n + 1) * BRANCH_W, :])
        m = term if m is None else m + term
    o_ref[...] = x_ref[...] + _dot(m.astype(BF16), wo_ref[...])


def _merge(x, yf, ys, yh, gl, wb, wo, tm):
    n = x.shape[0]
    row = lambda c: pl.BlockSpec((tm, c), lambda i: (i, 0))
    return pl.pallas_call(
        _merge_kernel,
        grid=(n // tm,),
        in_specs=[row(D_MODEL), row(512), row(512), row(512), row(3 * D_MODEL),
                  _const_spec((3 * BRANCH_W, D_MODEL), single=True),
                  _const_spec((D_MODEL, D_MODEL), single=True)],
        out_specs=row(D_MODEL),
        out_shape=jax.ShapeDtypeStruct((n, D_MODEL), F32),
        compiler_params=_cparams(("parallel",)),
        name="merge_outproj",
    )(x, yf, ys, yh, gl, wb, wo)


def _memkv_kernel(m_ref, g_ref, wk_ref, wv_ref, gk_ref, k_ref, v_ref):
    h = _rms(m_ref[0], g_ref[...]).astype(BF16)
    kk = _dot(h, wk_ref[...])
    for hd in range(X_HEADS):
        sl = slice(hd * X_HD, (hd + 1) * X_HD)
        k_ref[0, :, sl] = _rms(kk[:, sl], gk_ref[...]).astype(BF16)
    v_ref[0] = _dot(h, wv_ref[...]).astype(BF16)


def _memkv(mem, g, wk, wv, gk):
    bsz, nm, _ = mem.shape
    spec = pl.BlockSpec((1, nm, D_MODEL), lambda b: (b, 0, 0))
    return pl.pallas_call(
        _memkv_kernel,
        grid=(bsz,),
        in_specs=[spec, _const_spec((1, D_MODEL)), _const_spec((D_MODEL, D_MODEL)),
                  _const_spec((D_MODEL, D_MODEL)), _const_spec((1, X_HD))],
        out_specs=[spec, spec],
        out_shape=[jax.ShapeDtypeStruct(mem.shape, BF16)] * 2,
        compiler_params=_cparams(("parallel",)),
        name="mem_kv",
    )(mem, g, wk, wv, gk)


def _xattn_kernel(x_ref, g_ref, wq_ref, gq_ref, k_ref, v_ref, wo_ref, o_ref):
    x = x_ref[...]
    h = _rms(x, g_ref[...]).astype(BF16)
    q = _dot(h, wq_ref[...])
    outs = []
    for hd in range(X_HEADS):
        sl = slice(hd * X_HD, (hd + 1) * X_HD)
        qh = (_rms(q[:, sl], gq_ref[...]) * (X_HD ** -0.5)).astype(BF16)
        s = _dot_nt(qh, k_ref[0, :, sl])
        p = jnp.exp(s - jnp.max(s, axis=-1, keepdims=True))
        l = jnp.sum(p, axis=-1, keepdims=True)
        outs.append((_dot(p.astype(BF16), v_ref[0, :, sl]) / l).astype(BF16))
    o_ref[...] = x + _dot(jnp.concatenate(outs, axis=1), wo_ref[...])


def _xattn(x, g, wq, gq, km, vm, wo, seq, tm):
    n = x.shape[0]
    nm = km.shape[1]
    per_b = seq // tm
    row = pl.BlockSpec((tm, D_MODEL), lambda i: (i, 0))
    kv = pl.BlockSpec((1, nm, D_MODEL), lambda i: (i // per_b, 0, 0))
    return pl.pallas_call(
        _xattn_kernel,
        grid=(n // tm,),
        in_specs=[row, _const_spec((1, D_MODEL)), _const_spec((D_MODEL, D_MODEL), single=True),
                  _const_spec((1, X_HD)), kv, kv, _const_spec((D_MODEL, D_MODEL), single=True)],
        out_specs=row,
        out_shape=jax.ShapeDtypeStruct((n, D_MODEL), F32),
        compiler_params=_cparams(("parallel",)),
        name="cross_attention",
    )(x, g, wq, gq, km, vm, wo)


FF_CHUNK = 256


def _ffn_kernel(x_ref, g_ref, wgu_ref, wd_ref, o_ref, act_ref):
    x = x_ref[...]
    h = _rms(x, g_ref[...]).astype(BF16)
    for c in range(0, D_FF, FF_CHUNK):
        a = _dot(h, wgu_ref[:, c:c + FF_CHUNK])
        b = _dot(h, wgu_ref[:, D_FF + c:D_FF + c + FF_CHUNK])
        act_ref[:, c:c + FF_CHUNK] = (a * _sigmoid(a) * b).astype(BF16)
    o_ref[...] = x + _dot(act_ref[...], wd_ref[...])


def _ffn(x, g, wgu, wd, tm):
    n = x.shape[0]
    row = pl.BlockSpec((tm, D_MODEL), lambda i: (i, 0))
    return pl.pallas_call(
        _ffn_kernel,
        grid=(n // tm,),
        in_specs=[row, _const_spec((1, D_MODEL)), _const_spec((D_MODEL, 2 * D_FF), single=True),
                  _const_spec((D_FF, D_MODEL), single=True)],
        out_specs=row,
        out_shape=jax.ShapeDtypeStruct((n, D_MODEL), F32),
        scratch_shapes=[pltpu.VMEM((tm, D_FF), BF16)],
        compiler_params=_cparams(("parallel",)),
        name="swiglu",
    )(x, g, wgu, wd)


def _reorder_w_in(w_in):
    depth = w_in.shape[0]

    def pad_heads(w):
        w = w.reshape(depth, D_MODEL, FOX_HEADS, FOX_HD)
        w = jnp.pad(w, ((0, 0), (0, 0), (0, 0), (0, FOX_PAD - FOX_HD)))
        return w.reshape(depth, D_MODEL, FOX_QK_W)

    fq, fk, fv = w_in[:, :, 0:512], w_in[:, :, 512:1024], w_in[:, :, 1024:1536]
    ff = w_in[:, :, 1536:1544]
    rest = w_in[:, :, 1544:]
    pad = jnp.zeros((depth, D_MODEL, C_END - C_FF - FOX_HEADS), w_in.dtype)
    return jnp.concatenate([pad_heads(fq), pad_heads(fk), fv, rest, ff, pad], axis=-1).astype(BF16)


def _pad_head_vec(v, fill=0.0):
    v = jnp.concatenate([v.astype(F32), jnp.full((FOX_PAD - FOX_HD,), fill, F32)])
    return jnp.tile(v, FOX_HEADS).reshape(1, FOX_QK_W)


def kernel(x, mem, norm_mix, w_in, fox_fbias, fox_qnorm, fox_knorm, s5_a_re, s5_a_im, s5_b_re, s5_b_im, s5_c_re, s5_c_im, s5_d, s5_log_dt, s5_w_glu, s5_b_glu, hg_lb, hg_onorm, w_branch, w_out, norm_x, norm_mem, xq, xk, xv, xo, x_qnorm, x_knorm, norm_ffn, w_gate_up, w_down):
    bsz, seq, _ = x.shape
    depth = w_in.shape[0]
    n = bsz * seq
    tm_in = min(512, seq)
    tm = min(1024, seq)
    tq = min(512, seq)
    tt = min(256, seq)
    s5_tt = min(2048, seq)
    assert seq % tm == 0 and seq % tq == 0 and seq % tt == 0 and tt % HG_CHUNK == 0
    assert seq % s5_tt == 0 and s5_tt % (8 * S5_T) == 0

    row = lambda v: v.astype(F32).reshape(1, -1)
    w_in_r = _reorder_w_in(w_in)
    s5_wglu = s5_w_glu.astype(BF16)
    wb, wo = w_branch.astype(BF16), w_out.astype(BF16)
    wq, wk, wv, wxo = xq.astype(BF16), xk.astype(BF16), xv.astype(BF16), xo.astype(BF16)
    wgu, wd = w_gate_up.astype(BF16), w_down.astype(BF16)

    lb_all = jnp.cumsum(jax.nn.softmax(hg_lb.astype(F32), axis=0), axis=0)
    lb_all = lb_all - lb_all[0:1]

    lane = jnp.arange(256)
    hsum = ((lane[:, None] // FOX_PAD == lane[None, :] // FOX_PAD)
            & (lane[:, None] % FOX_PAD < FOX_HD)).astype(BF16)
    qone = _pad_head_vec(jnp.zeros((FOX_HD,), F32)).at[0, :].set(
        jnp.tile((jnp.arange(FOX_PAD) >= FOX_HD) & (jnp.arange(FOX_PAD) < FOX_HD + FOX_BIAS_LANES),
                 FOX_HEADS).astype(F32))
    src = jnp.arange(FOX_BIAS_LANES * 128)
    dst = jnp.arange(FOX_QK_W)
    place = ((src[:, None] % 128 == dst[None, :] // FOX_PAD)
             & (dst[None, :] % FOX_PAD == FOX_HD + src[:, None] // 128)).astype(BF16)

    xf = x.astype(F32).reshape(n, D_MODEL)
    for l in range(depth):
        gq = _pad_head_vec(fox_qnorm[l]) * (FOX_HD ** -0.5 * LOG2E)
        gk = _pad_head_vec(fox_knorm[l])
        fb = jnp.pad(fox_fbias[l].astype(F32), (0, 128 - FOX_HEADS)).reshape(1, 128)
        q, k, v, su, hq, hf, hi, hg, gl, c = _inproj(xf, row(norm_mix[l]), w_in_r[l], gq, gk, qone, hsum,
                                                     fb, place, seq, tm_in)

        qk_bound = 1.01 * FOX_HD ** 0.5 * jnp.max(jnp.abs(fox_qnorm[l])) * jnp.max(jnp.abs(fox_knorm[l]))
        jstart = _fox_first_blocks(c, qk_bound, bsz, seq, tq)
        y_fox = _fox_attention(jstart, q, k, v, bsz, seq, tq)

        s5_w, s5_bst, s5_cout, s5_pw = _s5_params(s5_a_re[l], s5_a_im[l], s5_b_re[l], s5_b_im[l],
                                                  s5_c_re[l], s5_c_im[l], s5_log_dt[l])
        y_s5 = _s5(su, s5_w, s5_bst, s5_cout, s5_pw, row(s5_d[l]), s5_wglu[l], row(s5_b_glu[l]),
                   bsz, seq, s5_tt)

        lb = lb_all[l].reshape(1, -1)
        y_hg = _hgrn(hq, hf, hi, hg, jnp.log(lb), jnp.log1p(-lb), 1.0 - lb, row(hg_onorm[l]), bsz, seq, tt)

        xf = _merge(xf, y_fox, y_s5, y_hg, gl, wb[l], wo[l], tm)

        km, vm = _memkv(mem.astype(F32), row(norm_mem[l]), wk[l], wv[l], row(x_knorm[l]))
        xf = _xattn(xf, row(norm_x[l]), wq[l], row(x_qnorm[l]), km, vm, wxo[l], seq, tm)
        xf = _ffn(xf, row(norm_ffn[l]), wgu[l], wd[l], tm)
    return xf.reshape(bsz, seq, D_MODEL).astype(x.dtype)
```

```python
import functools
import math

import jax
import jax.numpy as jnp
from jax import lax
from jax.experimental import pallas as pl
from jax.experimental.pallas import tpu as pltpu

F32 = jnp.float32
BF16 = jnp.bfloat16
HIGHEST = lax.Precision.HIGHEST

D_MODEL = 1024
BRANCH_W = 512
FOX_HD = 64
FOX_HEADS = 8
S5_GROUP = 16
S5_GROUPS = 32
S5_STATE = 64
S5_NSTATE = S5_GROUPS * S5_STATE
HG_HEADS = 4
HG_D = 128
HG_CHUNK = 64
HG_SUB = 16
X_HEADS = 4
X_HD = 256
D_FF = 2816
EPS = 1e-6

VMEM_LIMIT_BYTES = 56 * 1024 * 1024

FOX_PAD = 128
FOX_QK_W = FOX_HEADS * FOX_PAD
C_FQ, C_FK, C_FV, C_SU, C_HQ, C_HF, C_HI, C_HG, C_GL, C_FF, C_END = (
    0, 1024, 2048, 2560, 3072, 3584, 4096, 4608, 5120, 8192, 8320)
FOX_BIAS_LANES = 3
LOG2E = 1.4426950408889634
FOX_SKIP_NATS = 30.0
FOX_REF_MARGIN = 100.0
FOX_BOUNDED_MAX = 100.0


NEG_BIG = -1e30


def _cparams(sem, vmem_limit_bytes=VMEM_LIMIT_BYTES):
    return pltpu.CompilerParams(dimension_semantics=sem, vmem_limit_bytes=vmem_limit_bytes)


def _rms(xf, g):
    return xf * lax.rsqrt(jnp.mean(xf * xf, axis=-1, keepdims=True) + EPS) * g


def _sigmoid(x):
    return 1.0 / (1.0 + jnp.exp(-x))


def _log_sigmoid(x):
    return jnp.minimum(x, 0.0) - jnp.log(1.0 + jnp.exp(-jnp.abs(x)))


def _dot(a, b):
    return jnp.dot(a, b, preferred_element_type=F32)


def _dot_nt(a, b):
    return lax.dot_general(a, b, (((1,), (1,)), ((), ())), preferred_element_type=F32)


def _dot_tn(a, b):
    return lax.dot_general(a, b, (((0,), (0,)), ((), ())), preferred_element_type=F32)


def _const_spec(shape, single=False):
    nd = len(shape)
    if single:
        return pl.BlockSpec(shape, lambda *_: (0,) * nd, pipeline_mode=pl.Buffered(1))
    return pl.BlockSpec(shape, lambda *_: (0,) * nd)


def _top16(v):
    bits = lax.bitcast_convert_type(v, jnp.uint32) & jnp.uint32(0xFFFF0000)
    return lax.bitcast_convert_type(bits, F32)


def _split3(v):
    hi = _top16(v)
    r1 = v - hi
    mid = _top16(r1)
    return jnp.concatenate([hi, mid, r1 - mid], axis=1).astype(BF16)


def _inproj_kernel(x_ref, g_ref, w_ref, gq_ref, gk_ref, qone_ref, kone_ref, hsum_ref, fb_ref, qoff_ref,
                   placek_ref, placeq_ref,
                   q_ref, k_ref, v_ref, su_ref, hq_ref, hf_ref, hi_ref, hg_ref, gl_ref, c_ref,
                   carry_ref, *, tiles_per_seq):
    tm = x_ref.shape[0]

    @pl.when(pl.program_id(0) % tiles_per_seq == 0)
    def _():
        carry_ref[...] = jnp.zeros(carry_ref.shape, F32)

    h = _rms(x_ref[...], g_ref[...]).astype(BF16)

    def proj(lo, hi):
        return _dot(h, w_ref[:, lo:hi])

    def sumsq(t):
        return _dot((t * t).astype(BF16), hsum_ref[...])

    def headnorm(t, ss, g):
        return t * lax.rsqrt(ss * (1.0 / FOX_HD) + EPS) * g

    lf = _log_sigmoid(proj(C_FF, C_END) + fb_ref[...])
    r = lax.broadcasted_iota(jnp.int32, (tm, tm), 0)
    cc = lax.broadcasted_iota(jnp.int32, (tm, tm), 1)
    lower = (cc <= r).astype(F32)
    cs = jnp.dot(lower, lf, precision=HIGHEST, preferred_element_type=F32) + carry_ref[...]
    carry_ref[...] = cs[tm - 1:tm, :]
    c_ref[...] = cs
    d = -(cs * LOG2E)
    kbias = _dot(_split3(d), placek_ref[...])
    qbias = _dot(_split3(-d - qoff_ref[...]), placeq_ref[...])

    chunks = range(0, FOX_QK_W, 256)
    tq = [proj(C_FQ + c, C_FQ + c + 256) for c in chunks]
    tk = [proj(C_FK + c, C_FK + c + 256) for c in chunks]
    sq = [sumsq(t) for t in tq]
    sk = [sumsq(t) for t in tk]
    for n, c in enumerate(chunks):
        sl = slice(c, c + 256)
        q_ref[:, sl] = (headnorm(tq[n], sq[n], gq_ref[:, sl]) + qone_ref[:, sl] + qbias[:, sl]).astype(BF16)
        k_ref[:, sl] = (headnorm(tk[n], sk[n], gk_ref[:, sl]) + kone_ref[:, sl] + kbias[:, sl]).astype(BF16)
    v_ref[...] = proj(C_FV, C_SU).astype(BF16)
    su_ref[...] = proj(C_SU, C_HQ).astype(BF16)
    hq_ref[...] = proj(C_HQ, C_HF).astype(BF16)
    hf_ref[...] = proj(C_HF, C_HI)
    hi_ref[...] = proj(C_HI, C_HG).astype(BF16)
    hg_ref[...] = proj(C_HG, C_GL).astype(BF16)
    for c in range(C_GL, C_FF, 512):
        gl_ref[:, c - C_GL:c - C_GL + 512] = proj(c, c + 512).astype(BF16)


def _inproj(x, g, w, gq, gk, qone, kone, hsum, fb, qoff, placek, placeq, seq, tm):
    n = x.shape[0]
    row = lambda c: pl.BlockSpec((tm, c), lambda i: (i, 0))
    outs = ([(FOX_QK_W, BF16)] * 2 + [(512, BF16)] * 3 + [(512, F32)] + [(512, BF16)] * 2
            + [(3072, BF16), (128, F32)])
    vec = _const_spec((1, FOX_QK_W))
    place = _const_spec((FOX_BIAS_LANES * 128, FOX_QK_W))
    return pl.pallas_call(
        functools.partial(_inproj_kernel, tiles_per_seq=seq // tm),
        grid=(n // tm,),
        in_specs=[row(D_MODEL), _const_spec((1, D_MODEL)), _const_spec((D_MODEL, C_END), single=True),
                  vec, vec, vec, vec, _const_spec((256, 256)), _const_spec((1, 128)), _const_spec((1, 128)),
                  place, place],
        out_specs=[row(c) for c, _ in outs],
        out_shape=[jax.ShapeDtypeStruct((n, c), dt) for c, dt in outs],
        scratch_shapes=[pltpu.VMEM((1, 128), F32)],
        compiler_params=_cparams(("arbitrary",)),
        name="inproj",
    )(x, g, w, gq, gk, qone, kone, hsum, fb, qoff, placek, placeq)


def _fox_kernel(jstart_ref, bounded_ref, q_ref, k_ref, v_ref, o_ref, m_ref, l_ref, acc_ref, *, tq, nq):
    b, p, i = pl.program_id(0), pl.program_id(1), pl.program_id(2)
    nchunk = tq // 128
    m_ref[...] = jnp.full(m_ref.shape, NEG_BIG, F32)
    l_ref[...] = jnp.zeros(l_ref.shape, F32)
    acc_ref[...] = jnp.zeros(acc_ref.shape, F32)

    def logits(j, hh, masked):
        start = pl.multiple_of(j * tq, tq)
        sl = slice(hh * FOX_PAD, (hh + 1) * FOX_PAD)
        s = _dot_nt(q_ref[:, sl], k_ref[pl.ds(start, tq), sl])
        if masked:
            rq = lax.broadcasted_iota(jnp.int32, (tq, tq), 0)
            ck = lax.broadcasted_iota(jnp.int32, (tq, tq), 1)
            s = jnp.where(ck <= rq, s, NEG_BIG)
        return s, v_ref[pl.ds(start, tq), :]

    def lane_partial_sum(pr):
        ls = pr[:, 0:128]
        for c in range(1, nchunk):
            ls = ls + pr[:, c * 128:(c + 1) * 128]
        return ls

    def step_bounded(j, masked):
        for hh in range(2):
            s, vb = logits(j, hh, masked)
            pr = jnp.exp2(s)
            l_ref[hh] += lane_partial_sum(pr)
            acc_ref[hh] += _dot(pr.astype(BF16), vb)

    def step_online(j, masked):
        for hh in range(2):
            s, vb = logits(j, hh, masked)
            mc = s[:, 0:128]
            for c in range(1, nchunk):
                mc = jnp.maximum(mc, s[:, c * 128:(c + 1) * 128])
            m_prev = m_ref[hh]
            m_new = jnp.maximum(m_prev, jnp.max(mc, axis=-1, keepdims=True))
            alpha = jnp.exp2(m_prev - m_new)
            pr = jnp.exp2(s - jnp.concatenate([m_new] * nchunk, axis=1))
            l_ref[hh] = alpha * l_ref[hh] + lane_partial_sum(pr)
            acc_ref[hh] = alpha * acc_ref[hh] + _dot(pr.astype(BF16), vb)
            m_ref[hh] = m_new

    base = ((b * (FOX_HEADS // 2) + p) * 2) * nq + i
    j0 = jnp.minimum(jstart_ref[base], jstart_ref[base + nq])
    count = i - j0

    def sweep(step):
        def body(n, carry):
            step(j0 + 2 * n, False)
            step(j0 + 2 * n + 1, False)
            return carry

        lax.fori_loop(0, count // 2, body, 0)

        @pl.when(count % 2 == 1)
        def _():
            step(i - 1, False)

        step(i, True)

    bounded = bounded_ref[0] == 1

    @pl.when(bounded)
    def _():
        sweep(step_bounded)

    @pl.when(jnp.logical_not(bounded))
    def _():
        sweep(step_online)

    lane = lax.broadcasted_iota(jnp.int32, (tq, 128), 1)
    o0 = acc_ref[0] / jnp.sum(l_ref[0], axis=-1, keepdims=True)
    o1 = acc_ref[1] / jnp.sum(l_ref[1], axis=-1, keepdims=True)
    o_ref[...] = jnp.where(lane < FOX_HD, o0, o1).astype(BF16)


def _fox_attention(jstart, bounded, q, k, v, bsz, seq, tq):
    n = q.shape[0]
    nq = seq // tq
    grid_spec = pltpu.PrefetchScalarGridSpec(
        num_scalar_prefetch=2,
        grid=(bsz, FOX_HEADS // 2, nq),
        in_specs=[pl.BlockSpec((tq, 2 * FOX_PAD), lambda b, p, i, js, bd: (b * nq + i, p)),
                  pl.BlockSpec((seq, 2 * FOX_PAD), lambda b, p, i, js, bd: (b, p)),
                  pl.BlockSpec((seq, 128), lambda b, p, i, js, bd: (b, p))],
        out_specs=pl.BlockSpec((tq, 128), lambda b, p, i, js, bd: (b * nq + i, p)),
        scratch_shapes=[pltpu.VMEM((2, tq, 128), F32), pltpu.VMEM((2, tq, 128), F32),
                        pltpu.VMEM((2, tq, 128), F32)],
    )
    return pl.pallas_call(
        functools.partial(_fox_kernel, tq=tq, nq=nq),
        grid_spec=grid_spec,
        out_shape=jax.ShapeDtypeStruct((n, 512), BF16),
        compiler_params=_cparams(("parallel", "parallel", "arbitrary")),
        name="fox_attention",
    )(jstart, bounded, q, k, v)


def _fox_first_blocks(c, qk_bound, bsz, seq, tq):
    nq = seq // tq
    cb = c.reshape(bsz, nq, tq, 128)
    c_end = cb[:, :, tq - 1, :FOX_HEADS].transpose(0, 2, 1)
    c_start = cb[:, :, 0, :FOX_HEADS].transpose(0, 2, 1)
    gap = c_end[:, :, None, :] - c_start[:, :, :, None]
    skip = gap > (2.0 * qk_bound + FOX_SKIP_NATS)
    jstart = jnp.sum(skip.astype(jnp.int32), axis=-1)
    jstart = jnp.minimum(jstart, jnp.arange(nq, dtype=jnp.int32)[None, None, :])
    return jstart.reshape(-1)


S5_T = 8
S5_BLK = 128
S5_NBLK = BRANCH_W // S5_BLK
S5_KW = S5_T * S5_BLK
S5_BST = S5_NSTATE // S5_NBLK
S5_VMEM_LIMIT_BYTES = 58 * 1024 * 1024


def _s5_expand(src_ref, rep_ref, dst_ref, row_div, col_div):
    n = S5_KW
    for j in range(S5_NBLK):
        for c in range(0, n, 256):
            rg = (lax.broadcasted_iota(jnp.int32, (n, 256), 0) // row_div) % 8
            cg = ((lax.broadcasted_iota(jnp.int32, (n, 256), 1) + c) // col_div) % 8
            rep = _dot(src_ref[j], rep_ref[:, c:c + 256])
            dst_ref[j, :, c:c + 256] = jnp.where(rg == cg, rep, 0.0).astype(BF16)


def _s5_kernel(u_ref, wc_ref, bc_ref, cc_ref, rep_th_ref, rep_rp_ref, pw_ref, d_ref, wglu_ref, bglu_ref,
               o_ref, w_ref, bst_ref, cout_ref, uf_ref, y_ref, hr_ref, hi_ref, cr_ref, ci_ref, *, tt):
    t = pl.program_id(1)
    rows = tt // S5_T

    @pl.when(t == 0)
    def _():
        cr_ref[...] = jnp.zeros(cr_ref.shape, F32)
        ci_ref[...] = jnp.zeros(ci_ref.shape, F32)
        _s5_expand(wc_ref, rep_th_ref, w_ref, S5_GROUP, S5_GROUP)
        _s5_expand(bc_ref, rep_rp_ref, bst_ref, S5_GROUP, S5_STATE)
        _s5_expand(cc_ref, rep_th_ref, cout_ref, S5_STATE, S5_GROUP)

    for j in range(S5_NBLK):
        uf_ref[j] = u_ref[:, j * S5_BLK:(j + 1) * S5_BLK].astype(F32)

    def chunk_rows(j):
        return jnp.concatenate([uf_ref[j, pl.ds(s, rows, stride=S5_T), :] for s in range(S5_T)],
                               axis=1).astype(BF16)

    for j in range(S5_NBLK):
        hin = _dot(chunk_rows(j), bst_ref[j])
        hr_ref[:, j * S5_BST:(j + 1) * S5_BST] = hin[:, :S5_BST]
        hi_ref[:, j * S5_BST:(j + 1) * S5_BST] = hin[:, S5_BST:]

    row8 = lax.broadcasted_iota(jnp.int32, (8, S5_NSTATE), 0)

    def scan_block(r, carry):
        cr, ci = carry
        rws = pl.ds(pl.multiple_of(r * 8, 8), 8)
        xr = hr_ref[rws, :]
        xi = hi_ref[rws, :]
        for n, k in enumerate((1, 2, 4)):
            ar = pw_ref[2 * n]
            ai = pw_ref[2 * n + 1]
            sr = pltpu.roll(xr, k, 0)
            si = pltpu.roll(xi, k, 0)
            xr, xi = xr + ar * sr - ai * si, xi + ar * si + ai * sr
        pr = pw_ref[6]
        pi = pw_ref[7]
        xr, xi = xr + pr * cr - pi * ci, xi + pr * ci + pi * cr
        hr_ref[rws, :] = jnp.where(row8 == 0, cr, pltpu.roll(xr, 1, 0))
        hi_ref[rws, :] = jnp.where(row8 == 0, ci, pltpu.roll(xi, 1, 0))
        return (jnp.broadcast_to(xr[7:8, :], xr.shape), jnp.broadcast_to(xi[7:8, :], xi.shape))

    cr, ci = lax.fori_loop(0, rows // 8, scan_block, (cr_ref[...], ci_ref[...]))
    cr_ref[...] = cr
    ci_ref[...] = ci

    for j in range(S5_NBLK):
        st = slice(j * S5_BST, (j + 1) * S5_BST)
        hb = jnp.concatenate([hr_ref[:, st], hi_ref[:, st]], axis=1).astype(BF16)
        yj = _dot(chunk_rows(j), w_ref[j]) + _dot(hb, cout_ref[j])
        for s in range(S5_T):
            y_ref[j, pl.ds(s, rows, stride=S5_T), :] = yj[:, s * S5_BLK:(s + 1) * S5_BLK]

    y = jnp.concatenate([y_ref[j] + d_ref[:, j * S5_BLK:(j + 1) * S5_BLK] * uf_ref[j]
                         for j in range(S5_NBLK)], axis=1)
    z = 0.5 * y * (1.0 + jnp.tanh(math.sqrt(2.0 / math.pi) * (y + 0.044715 * (y * y * y))))
    gate = _sigmoid(_dot(z.astype(BF16), wglu_ref[...]) + bglu_ref[...])
    o_ref[...] = (z * gate).astype(BF16)


def _s5(u, wc, bc, cc, rep_th, rep_rp, pw, d, wglu, bglu, bsz, seq, tt):
    n = u.shape[0]
    nt = seq // tt
    rowspec = pl.BlockSpec((tt, BRANCH_W), lambda b, t: (b * nt + t, 0))
    compact = _const_spec((S5_NBLK, S5_KW, S5_BLK), single=True)
    expanded = pltpu.VMEM((S5_NBLK, S5_KW, S5_KW), BF16)
    return pl.pallas_call(
        functools.partial(_s5_kernel, tt=tt),
        grid=(bsz, nt),
        in_specs=[rowspec, compact, compact, compact,
                  _const_spec((S5_BLK, S5_KW), single=True), _const_spec((S5_BLK, S5_KW), single=True),
                  _const_spec((8, 8, S5_NSTATE), single=True), _const_spec((1, BRANCH_W)),
                  _const_spec((BRANCH_W, BRANCH_W)), _const_spec((1, BRANCH_W))],
        out_specs=rowspec,
        out_shape=jax.ShapeDtypeStruct((n, BRANCH_W), BF16),
        scratch_shapes=[expanded, expanded, expanded,
                        pltpu.VMEM((S5_NBLK, tt, S5_BLK), F32), pltpu.VMEM((S5_NBLK, tt, S5_BLK), F32),
                        pltpu.VMEM((tt // S5_T, S5_NSTATE), F32), pltpu.VMEM((tt // S5_T, S5_NSTATE), F32),
                        pltpu.VMEM((8, S5_NSTATE), F32), pltpu.VMEM((8, S5_NSTATE), F32)],
        compiler_params=_cparams(("parallel", "arbitrary"), S5_VMEM_LIMIT_BYTES),
        name="s5_mixer",
    )(u, wc, bc, cc, rep_th, rep_rp, pw, d, wglu, bglu)


def _s5_params(a_re, a_im, b_re, b_im, c_re, c_im, log_dt):
    g, p, gs, tt = S5_GROUPS, S5_STATE, S5_GROUP, S5_T
    dt = jnp.exp(log_dt.astype(F32))[:, None]
    ar, ai = a_re.astype(F32), a_im.astype(F32)
    mag = jnp.exp(dt * ar)
    abar_r, abar_i = mag * jnp.cos(dt * ai), mag * jnp.sin(dt * ai)
    inv_den = 1.0 / (ar * ar + ai * ai)
    nr, ni = abar_r - 1.0, abar_i
    coef_r = (nr * ar + ni * ai) * inv_den
    coef_i = (ni * ar - nr * ai) * inv_den
    br, bi = b_re.astype(F32), b_im.astype(F32)
    bbar_r = coef_r[..., None] * br - coef_i[..., None] * bi
    bbar_i = coef_r[..., None] * bi + coef_i[..., None] * br
    cr, ci = c_re.astype(F32), c_im.astype(F32)

    pows = [(jnp.ones_like(abar_r), jnp.zeros_like(abar_i))]
    for _ in range(tt):
        pr, pi = pows[-1]
        pows.append((pr * abar_r - pi * abar_i, pr * abar_i + pi * abar_r))
    pw_r = jnp.stack([x[0] for x in pows])
    pw_i = jnp.stack([x[1] for x in pows])

    cb_r = cr[:, :, :, None] * bbar_r[:, None, :, :] - ci[:, :, :, None] * bbar_i[:, None, :, :]
    cb_i = cr[:, :, :, None] * bbar_i[:, None, :, :] + ci[:, :, :, None] * bbar_r[:, None, :, :]
    kl = (jnp.einsum('lgp,ghpk->lghk', pw_r[:tt], cb_r, precision=HIGHEST)
          - jnp.einsum('lgp,ghpk->lghk', pw_i[:tt], cb_i, precision=HIGHEST))
    nb, gl = S5_NBLK, g // S5_NBLK

    lag = jnp.arange(tt)[None, :] - jnp.arange(tt)[:, None]
    m = jnp.where((lag >= 0)[:, :, None, None, None], kl[jnp.clip(lag, 0, tt - 1)], 0.0)
    m = m.transpose(2, 0, 4, 1, 3).reshape(nb, gl, tt, gs, tt * gs)
    w = m.transpose(0, 2, 1, 3, 4).reshape(nb, S5_KW, S5_BLK)

    e_r, e_i = pw_r[:tt][::-1], pw_i[:tt][::-1]
    bs_r = e_r[:, :, :, None] * bbar_r[None] - e_i[:, :, :, None] * bbar_i[None]
    bs_i = e_r[:, :, :, None] * bbar_i[None] + e_i[:, :, :, None] * bbar_r[None]
    bs = jnp.stack([bs_r, bs_i]).transpose(2, 1, 4, 0, 3).reshape(nb, gl, tt, gs, 2 * p)
    bst = bs.transpose(0, 2, 1, 3, 4).reshape(nb, S5_KW, S5_BLK)

    q_r, q_i = pw_r[1:tt + 1], pw_i[1:tt + 1]
    co_r = cr[None] * q_r[:, :, None, :] - ci[None] * q_i[:, :, None, :]
    co_i = cr[None] * q_i[:, :, None, :] + ci[None] * q_r[:, :, None, :]
    co = jnp.stack([co_r, -co_i]).transpose(2, 0, 4, 1, 3).reshape(nb, gl, 2, p, tt * gs)
    cout = co.transpose(0, 2, 1, 3, 4).reshape(nb, S5_KW, S5_BLK)

    ar1, ai1 = pw_r[tt].reshape(-1), pw_i[tt].reshape(-1)
    apow = [(ar1, ai1)]
    for _ in range(7):
        pr, pi = apow[-1]
        apow.append((pr * ar1 - pi * ai1, pr * ai1 + pi * ar1))
    rows8 = jnp.arange(8)[:, None]
    tabs = []
    for k in (1, 2, 4):
        mask = (rows8 >= k).astype(F32)
        tabs += [mask * apow[k - 1][0][None, :], mask * apow[k - 1][1][None, :]]
    tabs.append(jnp.stack([apow[r][0] for r in range(8)]))
    tabs.append(jnp.stack([apow[r][1] for r in range(8)]))
    pw = jnp.stack(tabs)
    return w.astype(BF16), bst.astype(BF16), cout.astype(BF16), pw


def _hgrn_kernel(q_ref, f_ref, i_ref, g_ref, loglb_ref, log1mlb_ref, onemlb_ref, gain_ref,
                 o_ref, st_ref, *, tt):
    t = pl.program_id(1)

    @pl.when(t == 0)
    def _():
        st_ref[...] = jnp.zeros(st_ref.shape, F32)

    c_sz, sub = HG_CHUNK, HG_SUB
    n_sub = c_sz // sub
    r64 = lax.broadcasted_iota(jnp.int32, (c_sz, c_sz), 0)
    c64 = lax.broadcasted_iota(jnp.int32, (c_sz, c_sz), 1)
    lower = (c64 <= r64).astype(F32)
    row_s = lax.broadcasted_iota(jnp.int32, (sub, 1), 0)
    lane_s = lax.broadcasted_iota(jnp.int32, (sub, c_sz), 1)
    gain = gain_ref[...]

    def head_chunk(rows, hd):
        sl = slice(hd * HG_D, (hd + 1) * HG_D)
        loglb = loglb_ref[:, sl]
        z = f_ref[rows, sl]
        bb = log1mlb_ref[:, sl] + _log_sigmoid(z)
        logf = jnp.maximum(loglb, bb) + jnp.log(1.0 + jnp.exp(-jnp.abs(loglb - bb)))
        key = onemlb_ref[:, sl] * (1.0 / (1.0 + jnp.exp(z)))
        qx = q_ref[rows, sl].astype(F32)
        qf = qx * _sigmoid(qx)
        vb = i_ref[rows, sl]
        b = jnp.dot(lower, logf * LOG2E, precision=HIGHEST, preferred_element_type=F32)
        b_last = b[c_sz - 1:c_sz, :]
        st = st_ref[hd]
        o_inter = _dot_nt((qf * jnp.exp2(b)).astype(BF16), st.astype(BF16))
        kd = (key * jnp.exp2(b_last - b)).astype(BF16)
        st_ref[hd] = st * jnp.exp2(b_last) + _dot_tn(vb, kd)

        srows = []
        for blk in range(n_sub):
            lo = blk * sub
            b_i = b[lo:lo + sub]
            q_i = qf[lo:lo + sub]
            k_i = key[lo:lo + sub]
            sd = jnp.zeros((sub, c_sz), F32)
            for s in range(sub):
                e = jnp.exp2(jnp.minimum(b_i - b_i[s:s + 1], 0.0))
                col = jnp.sum(q_i * e * k_i[s:s + 1], axis=-1, keepdims=True)
                sd = jnp.where(lane_s == lo + s, col, sd)
            sd = jnp.where(lane_s <= lo + row_s, sd, 0.0)
            if blk > 0:
                ref = b[lo - 1:lo]
                qt = (q_i * jnp.exp2(b_i - ref)).astype(BF16)
                kt = (key * jnp.exp2(jnp.minimum(ref - b, 0.0))).astype(BF16)
                sd = jnp.where(lane_s < lo, _dot_nt(qt, kt), sd)
            srows.append(sd)
        scores = jnp.concatenate(srows, axis=0)
        o = o_inter + _dot(scores.astype(BF16), vb)
        gx = g_ref[rows, sl].astype(F32)
        y = o * lax.rsqrt(jnp.mean(o * o, axis=-1, keepdims=True) + EPS) * gain
        o_ref[rows, sl] = (y * (gx * _sigmoid(gx))).astype(BF16)

    def chunk(c, carry):
        rows = pl.ds(pl.multiple_of(c * c_sz, c_sz), c_sz)
        for hd in range(HG_HEADS):
            head_chunk(rows, hd)
        return carry

    lax.fori_loop(0, tt // c_sz, chunk, 0)


def _hgrn(hq, hf, hi, hg, loglb, log1mlb, onemlb, gain, bsz, seq, tt):
    n = hq.shape[0]
    nt = seq // tt
    spec = pl.BlockSpec((tt, 512), lambda b, t: (b * nt + t, 0))
    return pl.pallas_call(
        functools.partial(_hgrn_kernel, tt=tt),
        grid=(bsz, nt),
        in_specs=[spec, spec, spec, spec, _const_spec((1, 512)), _const_spec((1, 512)),
                  _const_spec((1, 512)), _const_spec((1, HG_D))],
        out_specs=spec,
        out_shape=jax.ShapeDtypeStruct((n, 512), BF16),
        scratch_shapes=[pltpu.VMEM((HG_HEADS, HG_D, HG_D), F32)],
        compiler_params=_cparams(("parallel", "arbitrary")),
        name="hgrn2_mixer",
    )(hq, hf, hi, hg, loglb, log1mlb, onemlb, gain)


def _merge_kernel(x_ref, yf_ref, ys_ref, yh_ref, gl_ref, wb_ref, wo_ref, o_ref):
    m = None
    for n, y_ref in enumerate((yf_ref, ys_ref, yh_ref)):
        gate = _sigmoid(gl_ref[:, n * D_MODEL:(n + 1) * D_MODEL].astype(F32))
        term = gate * _dot(y_ref[...], wb_ref[n * BRANCH_W:(n + 1) * BRANCH_W, :])
        m = term if m is None else m + term
    o_ref[...] = x_ref[...] + _dot(m.astype(BF16), wo_ref[...])


def _merge(x, yf, ys, yh, gl, wb, wo, tm):
    n = x.shape[0]
    row = lambda c: pl.BlockSpec((tm, c), lambda i: (i, 0))
    return pl.pallas_call(
        _merge_kernel,
        grid=(n // tm,),
        in_specs=[row(D_MODEL), row(512), row(512), row(512), row(3 * D_MODEL),
                  _const_spec((3 * BRANCH_W, D_MODEL), single=True),
                  _const_spec((D_MODEL, D_MODEL), single=True)],
        out_specs=row(D_MODEL),
        out_shape=jax.ShapeDtypeStruct((n, D_MODEL), F32),
        compiler_params=_cparams(("parallel",)),
        name="merge_outproj",
    )(x, yf, ys, yh, gl, wb, wo)


def _memkv_kernel(m_ref, g_ref, wk_ref, wv_ref, gk_ref, k_ref, v_ref):
    h = _rms(m_ref[0], g_ref[...]).astype(BF16)
    kk = _dot(h, wk_ref[...])
    for hd in range(X_HEADS):
        sl = slice(hd * X_HD, (hd + 1) * X_HD)
        k_ref[0, :, sl] = _rms(kk[:, sl], gk_ref[...]).astype(BF16)
    v_ref[0] = _dot(h, wv_ref[...]).astype(BF16)


def _memkv(mem, g, wk, wv, gk):
    bsz, nm, _ = mem.shape
    spec = pl.BlockSpec((1, nm, D_MODEL), lambda b: (b, 0, 0))
    return pl.pallas_call(
        _memkv_kernel,
        grid=(bsz,),
        in_specs=[spec, _const_spec((1, D_MODEL)), _const_spec((D_MODEL, D_MODEL)),
                  _const_spec((D_MODEL, D_MODEL)), _const_spec((1, X_HD))],
        out_specs=[spec, spec],
        out_shape=[jax.ShapeDtypeStruct(mem.shape, BF16)] * 2,
        compiler_params=_cparams(("parallel",)),
        name="mem_kv",
    )(mem, g, wk, wv, gk)


def _xattn_kernel(x_ref, g_ref, wq_ref, gq_ref, k_ref, v_ref, wo_ref, o_ref):
    x = x_ref[...]
    h = _rms(x, g_ref[...]).astype(BF16)
    q = _dot(h, wq_ref[...])
    outs = []
    for hd in range(X_HEADS):
        sl = slice(hd * X_HD, (hd + 1) * X_HD)
        qh = (_rms(q[:, sl], gq_ref[...]) * (X_HD ** -0.5)).astype(BF16)
        s = _dot_nt(qh, k_ref[0, :, sl])
        p = jnp.exp(s - jnp.max(s, axis=-1, keepdims=True))
        l = jnp.sum(p, axis=-1, keepdims=True)
        outs.append((_dot(p.astype(BF16), v_ref[0, :, sl]) / l).astype(BF16))
    o_ref[...] = x + _dot(jnp.concatenate(outs, axis=1), wo_ref[...])


def _xattn(x, g, wq, gq, km, vm, wo, seq, tm):
    n = x.shape[0]
    nm = km.shape[1]
    per_b = seq // tm
    row = pl.BlockSpec((tm, D_MODEL), lambda i: (i, 0))
    kv = pl.BlockSpec((1, nm, D_MODEL), lambda i: (i // per_b, 0, 0))
    return pl.pallas_call(
        _xattn_kernel,
        grid=(n // tm,),
        in_specs=[row, _const_spec((1, D_MODEL)), _const_spec((D_MODEL, D_MODEL), single=True),
                  _const_spec((1, X_HD)), kv, kv, _const_spec((D_MODEL, D_MODEL), single=True)],
        out_specs=row,
        out_shape=jax.ShapeDtypeStruct((n, D_MODEL), F32),
        compiler_params=_cparams(("parallel",)),
        name="cross_attention",
    )(x, g, wq, gq, km, vm, wo)


FF_CHUNK = 256


def _ffn_kernel(x_ref, g_ref, wgu_ref, wd_ref, o_ref, act_ref):
    x = x_ref[...]
    h = _rms(x, g_ref[...]).astype(BF16)
    for c in range(0, D_FF, FF_CHUNK):
        a = _dot(h, wgu_ref[:, c:c + FF_CHUNK])
        b = _dot(h, wgu_ref[:, D_FF + c:D_FF + c + FF_CHUNK])
        act_ref[:, c:c + FF_CHUNK] = (a * _sigmoid(a) * b).astype(BF16)
    o_ref[...] = x + _dot(act_ref[...], wd_ref[...])


def _ffn(x, g, wgu, wd, tm):
    n = x.shape[0]
    row = pl.BlockSpec((tm, D_MODEL), lambda i: (i, 0))
    return pl.pallas_call(
        _ffn_kernel,
        grid=(n // tm,),
        in_specs=[row, _const_spec((1, D_MODEL)), _const_spec((D_MODEL, 2 * D_FF), single=True),
                  _const_spec((D_FF, D_MODEL), single=True)],
        out_specs=row,
        out_shape=jax.ShapeDtypeStruct((n, D_MODEL), F32),
        scratch_shapes=[pltpu.VMEM((tm, D_FF), BF16)],
        compiler_params=_cparams(("parallel",)),
        name="swiglu",
    )(x, g, wgu, wd)


def _reorder_w_in(w_in):
    depth = w_in.shape[0]

    def pad_heads(w):
        w = w.reshape(depth, D_MODEL, FOX_HEADS, FOX_HD)
        w = jnp.pad(w, ((0, 0), (0, 0), (0, 0), (0, FOX_PAD - FOX_HD)))
        return w.reshape(depth, D_MODEL, FOX_QK_W)

    fq, fk, fv = w_in[:, :, 0:512], w_in[:, :, 512:1024], w_in[:, :, 1024:1536]
    ff = w_in[:, :, 1536:1544]
    rest = w_in[:, :, 1544:]
    pad = jnp.zeros((depth, D_MODEL, C_END - C_FF - FOX_HEADS), w_in.dtype)
    return jnp.concatenate([pad_heads(fq), pad_heads(fk), fv, rest, ff, pad], axis=-1).astype(BF16)


def _pad_head_vec(v, fill=0.0):
    v = jnp.concatenate([v.astype(F32), jnp.full((FOX_PAD - FOX_HD,), fill, F32)])
    return jnp.tile(v, FOX_HEADS).reshape(1, FOX_QK_W)


def kernel(x, mem, norm_mix, w_in, fox_fbias, fox_qnorm, fox_knorm, s5_a_re, s5_a_im, s5_b_re, s5_b_im, s5_c_re, s5_c_im, s5_d, s5_log_dt, s5_w_glu, s5_b_glu, hg_lb, hg_onorm, w_branch, w_out, norm_x, norm_mem, xq, xk, xv, xo, x_qnorm, x_knorm, norm_ffn, w_gate_up, w_down):
    bsz, seq, _ = x.shape
    depth = w_in.shape[0]
    n = bsz * seq
    tm_in = min(512, seq)
    tm = min(1024, seq)
    tq = min(512, seq)
    tt = min(256, seq)
    s5_tt = min(2048, seq)
    assert seq % tm == 0 and seq % tq == 0 and seq % tt == 0 and tt % HG_CHUNK == 0
    assert seq % s5_tt == 0 and s5_tt % (8 * S5_T) == 0

    row = lambda v: v.astype(F32).reshape(1, -1)
    w_in_r = _reorder_w_in(w_in)
    s5_wglu = s5_w_glu.astype(BF16)
    wb, wo = w_branch.astype(BF16), w_out.astype(BF16)
    wq, wk, wv, wxo = xq.astype(BF16), xk.astype(BF16), xv.astype(BF16), xo.astype(BF16)
    wgu, wd = w_gate_up.astype(BF16), w_down.astype(BF16)

    lb_all = jnp.cumsum(jax.nn.softmax(hg_lb.astype(F32), axis=0), axis=0)
    lb_all = lb_all - lb_all[0:1]

    lane = jnp.arange(256)
    hsum = ((lane[:, None] // FOX_PAD == lane[None, :] // FOX_PAD)
            & (lane[:, None] % FOX_PAD < FOX_HD)).astype(BF16)
    pad_lane = jnp.tile(jnp.arange(FOX_PAD), FOX_HEADS)
    k_lanes = (pad_lane >= FOX_HD) & (pad_lane < FOX_HD + FOX_BIAS_LANES)
    q_lanes = (pad_lane >= FOX_HD + FOX_BIAS_LANES) & (pad_lane < FOX_HD + 2 * FOX_BIAS_LANES)
    qone = k_lanes.astype(F32).reshape(1, FOX_QK_W)
    kone = q_lanes.astype(F32).reshape(1, FOX_QK_W)
    src = jnp.arange(FOX_BIAS_LANES * 128)
    dst = jnp.arange(FOX_QK_W)

    def placement(first):
        return ((src[:, None] % 128 == dst[None, :] // FOX_PAD)
                & (dst[None, :] % FOX_PAD == first + src[:, None] // 128)).astype(BF16)

    placek = placement(FOX_HD)
    placeq = placement(FOX_HD + FOX_BIAS_LANES)

    col = jnp.arange(S5_KW)
    src = jnp.arange(S5_BLK)
    rep_th = ((src[:, None] // S5_GROUP == col[None, :] // S5_BLK)
              & (src[:, None] % S5_GROUP == col[None, :] % S5_GROUP)).astype(BF16)
    rep_rp = ((src[:, None] // S5_STATE == col[None, :] // S5_BST)
              & (src[:, None] % S5_STATE == col[None, :] % S5_STATE)).astype(BF16)

    xf = x.astype(F32).reshape(n, D_MODEL)
    for l in range(depth):
        gq = _pad_head_vec(fox_qnorm[l]) * (FOX_HD ** -0.5 * LOG2E)
        gk = _pad_head_vec(fox_knorm[l])
        fb = jnp.pad(fox_fbias[l].astype(F32), (0, 128 - FOX_HEADS)).reshape(1, 128)
        qk_bound = 1.01 * FOX_HD ** 0.5 * jnp.max(jnp.abs(fox_qnorm[l])) * jnp.max(jnp.abs(fox_knorm[l]))
        qk_bound2 = qk_bound * LOG2E
        qoff = jnp.full((1, 128), qk_bound2 - FOX_REF_MARGIN, F32)
        q, k, v, su, hq, hf, hi, hg, gl, c = _inproj(xf, row(norm_mix[l]), w_in_r[l], gq, gk, qone, kone, hsum,
                                                     fb, qoff, placek, placeq, seq, tm_in)

        jstart = _fox_first_blocks(c, qk_bound, bsz, seq, tq)
        bounded = (qk_bound2 <= FOX_BOUNDED_MAX).astype(jnp.int32).reshape(1)
        y_fox = _fox_attention(jstart, bounded, q, k, v, bsz, seq, tq)

        s5_w, s5_bst, s5_cout, s5_pw = _s5_params(s5_a_re[l], s5_a_im[l], s5_b_re[l], s5_b_im[l],
                                                  s5_c_re[l], s5_c_im[l], s5_log_dt[l])
        y_s5 = _s5(su, s5_w, s5_bst, s5_cout, rep_th, rep_rp, s5_pw, row(s5_d[l]), s5_wglu[l],
                   row(s5_b_glu[l]), bsz, seq, s5_tt)

        lb = lb_all[l].reshape(1, -1)
        y_hg = _hgrn(hq, hf, hi, hg, jnp.log(lb), jnp.log1p(-lb), 1.0 - lb, row(hg_onorm[l]), bsz, seq, tt)

        xf = _merge(xf, y_fox, y_s5, y_hg, gl, wb[l], wo[l], tm)

        km, vm = _memkv(mem.astype(F32), row(norm_mem[l]), wk[l], wv[l], row(x_knorm[l]))
        xf = _xattn(xf, row(norm_x[l]), wq[l], row(x_qnorm[l]), km, vm, wxo[l], seq, tm)
        xf = _ffn(xf, row(norm_ffn[l]), wgu[l], wd[l], tm)
    return xf.reshape(bsz, seq, D_MODEL).astype(x.dtype)
```

```python
import functools
import math

import jax
import jax.numpy as jnp
from jax import lax
from jax.experimental import pallas as pl
from jax.experimental.pallas import tpu as pltpu

F32 = jnp.float32
BF16 = jnp.bfloat16
HIGHEST = lax.Precision.HIGHEST

D_MODEL = 1024
BRANCH_W = 512
FOX_HD = 64
FOX_HEADS = 8
S5_GROUP = 16
S5_GROUPS = 32
S5_STATE = 64
S5_NSTATE = S5_GROUPS * S5_STATE
HG_HEADS = 4
HG_D = 128
HG_CHUNK = 64
HG_SUB = 8
X_HEADS = 4
X_HD = 256
D_FF = 2816
EPS = 1e-6

VMEM_LIMIT_BYTES = 56 * 1024 * 1024

FOX_PAD = 128
FOX_QK_W = FOX_HEADS * FOX_PAD
A_FQ, A_FK, A_FV, A_FF, A_END = 0, 1024, 2048, 2560, 2688
B_SU, B_HQ, B_HF, B_HI, B_HG, B_GL, B_END = 0, 512, 1024, 1536, 2048, 2560, 5632
FOX_BIAS_LANES = 3
LOG2E = 1.4426950408889634
FOX_SKIP_NATS = 30.0
FOX_REF_MARGIN = 100.0
FOX_BOUNDED_MAX = 100.0


NEG_BIG = -1e30


def _cparams(sem, vmem_limit_bytes=VMEM_LIMIT_BYTES):
    return pltpu.CompilerParams(dimension_semantics=sem, vmem_limit_bytes=vmem_limit_bytes)


def _rms(xf, g):
    return xf * lax.rsqrt(jnp.mean(xf * xf, axis=-1, keepdims=True) + EPS) * g


def _sigmoid(x):
    return 1.0 / (1.0 + jnp.exp(-x))


def _log_sigmoid(x):
    return jnp.minimum(x, 0.0) - jnp.log(1.0 + jnp.exp(-jnp.abs(x)))


def _dot(a, b):
    return jnp.dot(a, b, preferred_element_type=F32)


def _dot_nt(a, b):
    return lax.dot_general(a, b, (((1,), (1,)), ((), ())), preferred_element_type=F32)


def _dot_tn(a, b):
    return lax.dot_general(a, b, (((0,), (0,)), ((), ())), preferred_element_type=F32)


def _const_spec(shape, single=False):
    nd = len(shape)
    if single:
        return pl.BlockSpec(shape, lambda *_: (0,) * nd, pipeline_mode=pl.Buffered(1))
    return pl.BlockSpec(shape, lambda *_: (0,) * nd)


def _top16(v):
    bits = lax.bitcast_convert_type(v, jnp.uint32) & jnp.uint32(0xFFFF0000)
    return lax.bitcast_convert_type(bits, F32)


def _split3(v):
    hi = _top16(v)
    r1 = v - hi
    mid = _top16(r1)
    return [hi, mid, r1 - mid]


CUM_ROWS = 256


def _inproj_fox_kernel(x_ref, g_ref, w_ref, gq_ref, gk_ref, qone_ref, kone_ref, hsum_ref, fb_ref, qoff_ref,
                       place_ref, q_ref, k_ref, v_ref, c_ref, carry_ref, *, tiles_per_seq):
    tm = x_ref.shape[0]

    @pl.when(pl.program_id(0) % tiles_per_seq == 0)
    def _():
        carry_ref[...] = jnp.zeros(carry_ref.shape, F32)

    h = _rms(x_ref[...], g_ref[...]).astype(BF16)

    def proj(lo, hi):
        return _dot(h, w_ref[:, lo:hi])

    def sumsq(t):
        return _dot((t * t).astype(BF16), hsum_ref[...])

    def headnorm(t, ss, g):
        return t * lax.rsqrt(ss * (1.0 / FOX_HD) + EPS) * g

    lf = _log_sigmoid(proj(A_FF, A_END) + fb_ref[...])
    r = lax.broadcasted_iota(jnp.int32, (CUM_ROWS, CUM_ROWS), 0)
    cc = lax.broadcasted_iota(jnp.int32, (CUM_ROWS, CUM_ROWS), 1)
    lower = (cc <= r).astype(F32)
    carry = carry_ref[...]
    parts = []
    for r0 in range(0, tm, CUM_ROWS):
        part = jnp.dot(lower, lf[r0:r0 + CUM_ROWS], precision=HIGHEST, preferred_element_type=F32) + carry
        carry = part[CUM_ROWS - 1:CUM_ROWS, :]
        parts.append(part)
    carry_ref[...] = carry
    cs = jnp.concatenate(parts, axis=0)
    c_ref[...] = cs
    d = -(cs * LOG2E)
    lane = lax.broadcasted_iota(jnp.int32, (tm, 128), 1)
    packed = None
    for n, piece in enumerate(_split3(d) + _split3(-d - qoff_ref[...])):
        term = jnp.where(lane < FOX_HEADS, piece, 0.0)
        term = term if n == 0 else pltpu.roll(term, FOX_HEADS * n, 1)
        packed = term if packed is None else packed + term
    bias = _dot(packed.astype(BF16), place_ref[...])
    qbias, kbias = bias[:, :FOX_QK_W], bias[:, FOX_QK_W:]

    chunks = range(0, FOX_QK_W, 256)
    tq = [proj(A_FQ + c, A_FQ + c + 256) for c in chunks]
    tk = [proj(A_FK + c, A_FK + c + 256) for c in chunks]
    sq = [sumsq(t) for t in tq]
    sk = [sumsq(t) for t in tk]
    for n, c in enumerate(chunks):
        sl = slice(c, c + 256)
        q_ref[:, sl] = (headnorm(tq[n], sq[n], gq_ref[:, sl]) + qone_ref[:, sl] + qbias[:, sl]).astype(BF16)
        k_ref[:, sl] = (headnorm(tk[n], sk[n], gk_ref[:, sl]) + kone_ref[:, sl] + kbias[:, sl]).astype(BF16)
    v_ref[...] = proj(A_FV, A_FF).astype(BF16)


def _inproj_fox(x, g, w, gq, gk, qone, kone, hsum, fb, qoff, place, seq, tm):
    n = x.shape[0]
    row = lambda c: pl.BlockSpec((tm, c), lambda i: (i, 0))
    outs = [(FOX_QK_W, BF16)] * 2 + [(512, BF16), (128, F32)]
    vec = _const_spec((1, FOX_QK_W))
    return pl.pallas_call(
        functools.partial(_inproj_fox_kernel, tiles_per_seq=seq // tm),
        grid=(n // tm,),
        in_specs=[row(D_MODEL), _const_spec((1, D_MODEL)), _const_spec((D_MODEL, A_END), single=True),
                  vec, vec, vec, vec, _const_spec((256, 256)), _const_spec((1, 128)), _const_spec((1, 128)),
                  _const_spec((128, 2 * FOX_QK_W))],
        out_specs=[row(c) for c, _ in outs],
        out_shape=[jax.ShapeDtypeStruct((n, c), dt) for c, dt in outs],
        scratch_shapes=[pltpu.VMEM((1, 128), F32)],
        compiler_params=_cparams(("arbitrary",)),
        name="inproj_fox",
    )(x, g, w, gq, gk, qone, kone, hsum, fb, qoff, place)


def _inproj_rest_kernel(x_ref, g_ref, w_ref, su_ref, hq_ref, hf_ref, hi_ref, hg_ref, gl_ref):
    h = _rms(x_ref[...], g_ref[...]).astype(BF16)

    def proj(lo, hi):
        return _dot(h, w_ref[:, lo:hi])

    su_ref[...] = proj(B_SU, B_HQ).astype(BF16)
    hq_ref[...] = proj(B_HQ, B_HF).astype(BF16)
    hf_ref[...] = proj(B_HF, B_HI)
    hi_ref[...] = proj(B_HI, B_HG).astype(BF16)
    hg_ref[...] = proj(B_HG, B_GL).astype(BF16)
    for c in range(B_GL, B_END, 512):
        gl_ref[:, c - B_GL:c - B_GL + 512] = proj(c, c + 512).astype(BF16)


def _inproj_rest(x, g, w, tm):
    n = x.shape[0]
    row = lambda c: pl.BlockSpec((tm, c), lambda i: (i, 0))
    outs = [(512, BF16)] * 2 + [(512, F32)] + [(512, BF16)] * 2 + [(3072, BF16)]
    return pl.pallas_call(
        _inproj_rest_kernel,
        grid=(n // tm,),
        in_specs=[row(D_MODEL), _const_spec((1, D_MODEL)), _const_spec((D_MODEL, B_END), single=True)],
        out_specs=[row(c) for c, _ in outs],
        out_shape=[jax.ShapeDtypeStruct((n, c), dt) for c, dt in outs],
        compiler_params=_cparams(("parallel",)),
        name="inproj_rest",
    )(x, g, w)


def _fox_kernel(jstart_ref, bounded_ref, q_ref, k_ref, v_ref, o_ref, m_ref, l_ref, acc_ref, *, tq, nq):
    b, p, i = pl.program_id(0), pl.program_id(1), pl.program_id(2)
    nchunk = tq // 128
    m_ref[...] = jnp.full(m_ref.shape, NEG_BIG, F32)
    l_ref[...] = jnp.zeros(l_ref.shape, F32)
    acc_ref[...] = jnp.zeros(acc_ref.shape, F32)

    def logits(j, hh, masked):
        start = pl.multiple_of(j * tq, tq)
        sl = slice(hh * FOX_PAD, (hh + 1) * FOX_PAD)
        s = _dot_nt(q_ref[:, sl], k_ref[pl.ds(start, tq), sl])
        if masked:
            rq = lax.broadcasted_iota(jnp.int32, (tq, tq), 0)
            ck = lax.broadcasted_iota(jnp.int32, (tq, tq), 1)
            s = jnp.where(ck <= rq, s, NEG_BIG)
        return s, v_ref[pl.ds(start, tq), :]

    def lane_partial_sum(pr):
        ls = pr[:, 0:128]
        for c in range(1, nchunk):
            ls = ls + pr[:, c * 128:(c + 1) * 128]
        return ls

    def step_bounded(j, masked):
        for hh in range(2):
            s, vb = logits(j, hh, masked)
            pr = jnp.exp2(s)
            l_ref[hh] += lane_partial_sum(pr)
            acc_ref[hh] += _dot(pr.astype(BF16), vb)

    def step_online(j, masked):
        for hh in range(2):
            s, vb = logits(j, hh, masked)
            mc = s[:, 0:128]
            for c in range(1, nchunk):
                mc = jnp.maximum(mc, s[:, c * 128:(c + 1) * 128])
            m_prev = m_ref[hh]
            m_new = jnp.maximum(m_prev, jnp.max(mc, axis=-1, keepdims=True))
            alpha = jnp.exp2(m_prev - m_new)
            pr = jnp.exp2(s - jnp.concatenate([m_new] * nchunk, axis=1))
            l_ref[hh] = alpha * l_ref[hh] + lane_partial_sum(pr)
            acc_ref[hh] = alpha * acc_ref[hh] + _dot(pr.astype(BF16), vb)
            m_ref[hh] = m_new

    base = ((b * (FOX_HEADS // 2) + p) * 2) * nq + i
    j0 = jnp.minimum(jstart_ref[base], jstart_ref[base + nq])
    count = i - j0

    def sweep(step):
        def body(n, carry):
            step(j0 + 2 * n, False)
            step(j0 + 2 * n + 1, False)
            return carry

        lax.fori_loop(0, count // 2, body, 0)

        @pl.when(count % 2 == 1)
        def _():
            step(i - 1, False)

        step(i, True)

    bounded = bounded_ref[0] == 1

    @pl.when(bounded)
    def _():
        sweep(step_bounded)

    @pl.when(jnp.logical_not(bounded))
    def _():
        sweep(step_online)

    lane = lax.broadcasted_iota(jnp.int32, (tq, 128), 1)
    o0 = acc_ref[0] / jnp.sum(l_ref[0], axis=-1, keepdims=True)
    o1 = acc_ref[1] / jnp.sum(l_ref[1], axis=-1, keepdims=True)
    o_ref[...] = jnp.where(lane < FOX_HD, o0, o1).astype(BF16)


def _fox_attention(jstart, bounded, q, k, v, bsz, seq, tq):
    n = q.shape[0]
    nq = seq // tq
    grid_spec = pltpu.PrefetchScalarGridSpec(
        num_scalar_prefetch=2,
        grid=(bsz, FOX_HEADS // 2, nq),
        in_specs=[pl.BlockSpec((tq, 2 * FOX_PAD), lambda b, p, i, js, bd: (b * nq + i, p)),
                  pl.BlockSpec((seq, 2 * FOX_PAD), lambda b, p, i, js, bd: (b, p)),
                  pl.BlockSpec((seq, 128), lambda b, p, i, js, bd: (b, p))],
        out_specs=pl.BlockSpec((tq, 128), lambda b, p, i, js, bd: (b * nq + i, p)),
        scratch_shapes=[pltpu.VMEM((2, tq, 128), F32), pltpu.VMEM((2, tq, 128), F32),
                        pltpu.VMEM((2, tq, 128), F32)],
    )
    return pl.pallas_call(
        functools.partial(_fox_kernel, tq=tq, nq=nq),
        grid_spec=grid_spec,
        out_shape=jax.ShapeDtypeStruct((n, 512), BF16),
        compiler_params=_cparams(("parallel", "parallel", "arbitrary")),
        name="fox_attention",
    )(jstart, bounded, q, k, v)


def _fox_first_blocks(c, qk_bound, bsz, seq, tq):
    nq = seq // tq
    cb = c.reshape(bsz, nq, tq, 128)
    c_end = cb[:, :, tq - 1, :FOX_HEADS].transpose(0, 2, 1)
    c_start = cb[:, :, 0, :FOX_HEADS].transpose(0, 2, 1)
    gap = c_end[:, :, None, :] - c_start[:, :, :, None]
    skip = gap > (2.0 * qk_bound + FOX_SKIP_NATS)
    jstart = jnp.sum(skip.astype(jnp.int32), axis=-1)
    jstart = jnp.minimum(jstart, jnp.arange(nq, dtype=jnp.int32)[None, None, :])
    return jstart.reshape(-1)


S5_T = 8
S5_BLK = 128
S5_NBLK = BRANCH_W // S5_BLK
S5_KW = S5_T * S5_BLK
S5_BST = S5_NSTATE // S5_NBLK
S5_VMEM_LIMIT_BYTES = 58 * 1024 * 1024


def _s5_expand(src_ref, rep_ref, dst_ref, row_div, col_div):
    n = S5_KW
    for j in range(S5_NBLK):
        for c in range(0, n, 256):
            rg = (lax.broadcasted_iota(jnp.int32, (n, 256), 0) // row_div) % 8
            cg = ((lax.broadcasted_iota(jnp.int32, (n, 256), 1) + c) // col_div) % 8
            rep = _dot(src_ref[j], rep_ref[:, c:c + 256])
            dst_ref[j, :, c:c + 256] = jnp.where(rg == cg, rep, 0.0).astype(BF16)


def _s5_kernel(u_ref, wc_ref, bc_ref, cc_ref, rep_th_ref, rep_rp_ref, pw_ref, d_ref, wglu_ref, bglu_ref,
               o_ref, w_ref, bst_ref, cout_ref, uf_ref, y_ref, hr_ref, hi_ref, cr_ref, ci_ref, *, tt):
    t = pl.program_id(1)
    rows = tt // S5_T

    @pl.when(t == 0)
    def _():
        cr_ref[...] = jnp.zeros(cr_ref.shape, F32)
        ci_ref[...] = jnp.zeros(ci_ref.shape, F32)
        _s5_expand(wc_ref, rep_th_ref, w_ref, S5_GROUP, S5_GROUP)
        _s5_expand(bc_ref, rep_rp_ref, bst_ref, S5_GROUP, S5_STATE)
        _s5_expand(cc_ref, rep_th_ref, cout_ref, S5_STATE, S5_GROUP)

    for j in range(S5_NBLK):
        uf_ref[j] = u_ref[:, j * S5_BLK:(j + 1) * S5_BLK].astype(F32)

    def chunk_rows(j):
        return jnp.concatenate([uf_ref[j, pl.ds(s, rows, stride=S5_T), :] for s in range(S5_T)],
                               axis=1).astype(BF16)

    for j in range(S5_NBLK):
        hin = _dot(chunk_rows(j), bst_ref[j])
        hr_ref[:, j * S5_BST:(j + 1) * S5_BST] = hin[:, :S5_BST]
        hi_ref[:, j * S5_BST:(j + 1) * S5_BST] = hin[:, S5_BST:]

    row8 = lax.broadcasted_iota(jnp.int32, (8, S5_NSTATE), 0)

    def scan_block(r, carry):
        cr, ci = carry
        rws = pl.ds(pl.multiple_of(r * 8, 8), 8)
        xr = hr_ref[rws, :]
        xi = hi_ref[rws, :]
        for n, k in enumerate((1, 2, 4)):
            ar = pw_ref[2 * n]
            ai = pw_ref[2 * n + 1]
            sr = pltpu.roll(xr, k, 0)
            si = pltpu.roll(xi, k, 0)
            xr, xi = xr + ar * sr - ai * si, xi + ar * si + ai * sr
        pr = pw_ref[6]
        pi = pw_ref[7]
        xr, xi = xr + pr * cr - pi * ci, xi + pr * ci + pi * cr
        hr_ref[rws, :] = jnp.where(row8 == 0, cr, pltpu.roll(xr, 1, 0))
        hi_ref[rws, :] = jnp.where(row8 == 0, ci, pltpu.roll(xi, 1, 0))
        return (jnp.broadcast_to(xr[7:8, :], xr.shape), jnp.broadcast_to(xi[7:8, :], xi.shape))

    cr, ci = lax.fori_loop(0, rows // 8, scan_block, (cr_ref[...], ci_ref[...]))
    cr_ref[...] = cr
    ci_ref[...] = ci

    for j in range(S5_NBLK):
        st = slice(j * S5_BST, (j + 1) * S5_BST)
        hb = jnp.concatenate([hr_ref[:, st], hi_ref[:, st]], axis=1).astype(BF16)
        yj = _dot(chunk_rows(j), w_ref[j]) + _dot(hb, cout_ref[j])
        for s in range(S5_T):
            y_ref[j, pl.ds(s, rows, stride=S5_T), :] = yj[:, s * S5_BLK:(s + 1) * S5_BLK]

    y = jnp.concatenate([y_ref[j] + d_ref[:, j * S5_BLK:(j + 1) * S5_BLK] * uf_ref[j]
                         for j in range(S5_NBLK)], axis=1)
    z = 0.5 * y * (1.0 + jnp.tanh(math.sqrt(2.0 / math.pi) * (y + 0.044715 * (y * y * y))))
    gate = _sigmoid(_dot(z.astype(BF16), wglu_ref[...]) + bglu_ref[...])
    o_ref[...] = (z * gate).astype(BF16)


def _s5(u, wc, bc, cc, rep_th, rep_rp, pw, d, wglu, bglu, bsz, seq, tt):
    n = u.shape[0]
    nt = seq // tt
    rowspec = pl.BlockSpec((tt, BRANCH_W), lambda b, t: (b * nt + t, 0))
    compact = _const_spec((S5_NBLK, S5_KW, S5_BLK), single=True)
    expanded = pltpu.VMEM((S5_NBLK, S5_KW, S5_KW), BF16)
    return pl.pallas_call(
        functools.partial(_s5_kernel, tt=tt),
        grid=(bsz, nt),
        in_specs=[rowspec, compact, compact, compact,
                  _const_spec((S5_BLK, S5_KW), single=True), _const_spec((S5_BLK, S5_KW), single=True),
                  _const_spec((8, 8, S5_NSTATE), single=True), _const_spec((1, BRANCH_W)),
                  _const_spec((BRANCH_W, BRANCH_W)), _const_spec((1, BRANCH_W))],
        out_specs=rowspec,
        out_shape=jax.ShapeDtypeStruct((n, BRANCH_W), BF16),
        scratch_shapes=[expanded, expanded, expanded,
                        pltpu.VMEM((S5_NBLK, tt, S5_BLK), F32), pltpu.VMEM((S5_NBLK, tt, S5_BLK), F32),
                        pltpu.VMEM((tt // S5_T, S5_NSTATE), F32), pltpu.VMEM((tt // S5_T, S5_NSTATE), F32),
                        pltpu.VMEM((8, S5_NSTATE), F32), pltpu.VMEM((8, S5_NSTATE), F32)],
        compiler_params=_cparams(("parallel", "arbitrary"), S5_VMEM_LIMIT_BYTES),
        name="s5_mixer",
    )(u, wc, bc, cc, rep_th, rep_rp, pw, d, wglu, bglu)


def _s5_params(a_re, a_im, b_re, b_im, c_re, c_im, log_dt):
    g, p, gs, tt = S5_GROUPS, S5_STATE, S5_GROUP, S5_T
    dt = jnp.exp(log_dt.astype(F32))[:, None]
    ar, ai = a_re.astype(F32), a_im.astype(F32)
    mag = jnp.exp(dt * ar)
    abar_r, abar_i = mag * jnp.cos(dt * ai), mag * jnp.sin(dt * ai)
    inv_den = 1.0 / (ar * ar + ai * ai)
    nr, ni = abar_r - 1.0, abar_i
    coef_r = (nr * ar + ni * ai) * inv_den
    coef_i = (ni * ar - nr * ai) * inv_den
    br, bi = b_re.astype(F32), b_im.astype(F32)
    bbar_r = coef_r[..., None] * br - coef_i[..., None] * bi
    bbar_i = coef_r[..., None] * bi + coef_i[..., None] * br
    cr, ci = c_re.astype(F32), c_im.astype(F32)

    pows = [(jnp.ones_like(abar_r), jnp.zeros_like(abar_i))]
    for _ in range(tt):
        pr, pi = pows[-1]
        pows.append((pr * abar_r - pi * abar_i, pr * abar_i + pi * abar_r))
    pw_r = jnp.stack([x[0] for x in pows])
    pw_i = jnp.stack([x[1] for x in pows])

    cb_r = cr[:, :, :, None] * bbar_r[:, None, :, :] - ci[:, :, :, None] * bbar_i[:, None, :, :]
    cb_i = cr[:, :, :, None] * bbar_i[:, None, :, :] + ci[:, :, :, None] * bbar_r[:, None, :, :]
    kl = (jnp.einsum('lgp,ghpk->lghk', pw_r[:tt], cb_r, precision=HIGHEST)
          - jnp.einsum('lgp,ghpk->lghk', pw_i[:tt], cb_i, precision=HIGHEST))
    nb, gl = S5_NBLK, g // S5_NBLK

    lag = jnp.arange(tt)[None, :] - jnp.arange(tt)[:, None]
    m = jnp.where((lag >= 0)[:, :, None, None, None], kl[jnp.clip(lag, 0, tt - 1)], 0.0)
    m = m.transpose(2, 0, 4, 1, 3).reshape(nb, gl, tt, gs, tt * gs)
    w = m.transpose(0, 2, 1, 3, 4).reshape(nb, S5_KW, S5_BLK)

    e_r, e_i = pw_r[:tt][::-1], pw_i[:tt][::-1]
    bs_r = e_r[:, :, :, None] * bbar_r[None] - e_i[:, :, :, None] * bbar_i[None]
    bs_i = e_r[:, :, :, None] * bbar_i[None] + e_i[:, :, :, None] * bbar_r[None]
    bs = jnp.stack([bs_r, bs_i]).transpose(2, 1, 4, 0, 3).reshape(nb, gl, tt, gs, 2 * p)
    bst = bs.transpose(0, 2, 1, 3, 4).reshape(nb, S5_KW, S5_BLK)

    q_r, q_i = pw_r[1:tt + 1], pw_i[1:tt + 1]
    co_r = cr[None] * q_r[:, :, None, :] - ci[None] * q_i[:, :, None, :]
    co_i = cr[None] * q_i[:, :, None, :] + ci[None] * q_r[:, :, None, :]
    co = jnp.stack([co_r, -co_i]).transpose(2, 0, 4, 1, 3).reshape(nb, gl, 2, p, tt * gs)
    cout = co.transpose(0, 2, 1, 3, 4).reshape(nb, S5_KW, S5_BLK)

    ar1, ai1 = pw_r[tt].reshape(-1), pw_i[tt].reshape(-1)
    apow = [(ar1, ai1)]
    for _ in range(7):
        pr, pi = apow[-1]
        apow.append((pr * ar1 - pi * ai1, pr * ai1 + pi * ar1))
    rows8 = jnp.arange(8)[:, None]
    tabs = []
    for k in (1, 2, 4):
        mask = (rows8 >= k).astype(F32)
        tabs += [mask * apow[k - 1][0][None, :], mask * apow[k - 1][1][None, :]]
    tabs.append(jnp.stack([apow[r][0] for r in range(8)]))
    tabs.append(jnp.stack([apow[r][1] for r in range(8)]))
    pw = jnp.stack(tabs)
    return w.astype(BF16), bst.astype(BF16), cout.astype(BF16), pw


def _hgrn_kernel(q_ref, f_ref, i_ref, g_ref, loglb_ref, log1mlb_ref, onemlb_ref, gain_ref,
                 o_ref, st_ref, *, tt):
    t = pl.program_id(1)

    @pl.when(t == 0)
    def _():
        st_ref[...] = jnp.zeros(st_ref.shape, F32)

    c_sz, sub = HG_CHUNK, HG_SUB
    n_sub = c_sz // sub
    r64 = lax.broadcasted_iota(jnp.int32, (c_sz, c_sz), 0)
    c64 = lax.broadcasted_iota(jnp.int32, (c_sz, c_sz), 1)
    lower = (c64 <= r64).astype(F32)
    row_s = lax.broadcasted_iota(jnp.int32, (sub, 1), 0)
    lane_s = lax.broadcasted_iota(jnp.int32, (sub, c_sz), 1)
    gain = gain_ref[...]

    def head_chunk(rows, hd):
        sl = slice(hd * HG_D, (hd + 1) * HG_D)
        loglb = loglb_ref[:, sl]
        z = f_ref[rows, sl]
        bb = log1mlb_ref[:, sl] + _log_sigmoid(z)
        logf = jnp.maximum(loglb, bb) + jnp.log(1.0 + jnp.exp(-jnp.abs(loglb - bb)))
        key = onemlb_ref[:, sl] * (1.0 / (1.0 + jnp.exp(z)))
        qx = q_ref[rows, sl].astype(F32)
        qf = qx * _sigmoid(qx)
        vb = i_ref[rows, sl]
        b = jnp.dot(lower, logf * LOG2E, precision=HIGHEST, preferred_element_type=F32)
        b_last = b[c_sz - 1:c_sz, :]
        st = st_ref[hd]
        o_inter = _dot_nt((qf * jnp.exp2(b)).astype(BF16), st.astype(BF16))
        kd = (key * jnp.exp2(b_last - b)).astype(BF16)
        st_ref[hd] = st * jnp.exp2(b_last) + _dot_tn(vb, kd)

        srows = []
        for blk in range(n_sub):
            lo = blk * sub
            b_i = b[lo:lo + sub]
            q_i = qf[lo:lo + sub]
            k_i = key[lo:lo + sub]
            sd = jnp.zeros((sub, c_sz), F32)
            for s in range(sub):
                e = jnp.exp2(b_i - b_i[s:s + 1])
                col = jnp.sum(q_i * e * k_i[s:s + 1], axis=-1, keepdims=True)
                sd = jnp.where(lane_s == lo + s, col, sd)
            sd = jnp.where(lane_s <= lo + row_s, sd, 0.0)
            if blk > 0:
                ref = b[lo - 1:lo]
                qt = (q_i * jnp.exp2(b_i - ref)).astype(BF16)
                kt = jnp.concatenate([key[:lo] * jnp.exp2(ref - b[:lo]), jnp.zeros((c_sz - lo, HG_D), F32)],
                                     axis=0).astype(BF16)
                sd = jnp.where(lane_s < lo, _dot_nt(qt, kt), sd)
            srows.append(sd)
        scores = jnp.concatenate(srows, axis=0)
        o = o_inter + _dot(scores.astype(BF16), vb)
        gx = g_ref[rows, sl].astype(F32)
        y = o * lax.rsqrt(jnp.mean(o * o, axis=-1, keepdims=True) + EPS) * gain
        o_ref[rows, sl] = (y * (gx * _sigmoid(gx))).astype(BF16)

    def chunk(c, carry):
        rows = pl.ds(pl.multiple_of(c * c_sz, c_sz), c_sz)
        for hd in range(HG_HEADS):
            head_chunk(rows, hd)
        return carry

    lax.fori_loop(0, tt // c_sz, chunk, 0)


def _hgrn(hq, hf, hi, hg, loglb, log1mlb, onemlb, gain, bsz, seq, tt):
    n = hq.shape[0]
    nt = seq // tt
    spec = pl.BlockSpec((tt, 512), lambda b, t: (b * nt + t, 0))
    return pl.pallas_call(
        functools.partial(_hgrn_kernel, tt=tt),
        grid=(bsz, nt),
        in_specs=[spec, spec, spec, spec, _const_spec((1, 512)), _const_spec((1, 512)),
                  _const_spec((1, 512)), _const_spec((1, HG_D))],
        out_specs=spec,
        out_shape=jax.ShapeDtypeStruct((n, 512), BF16),
        scratch_shapes=[pltpu.VMEM((HG_HEADS, HG_D, HG_D), F32)],
        compiler_params=_cparams(("parallel", "arbitrary")),
        name="hgrn2_mixer",
    )(hq, hf, hi, hg, loglb, log1mlb, onemlb, gain)


def _merge_kernel(x_ref, yf_ref, ys_ref, yh_ref, gl_ref, wb_ref, wo_ref, o_ref):
    m = None
    for n, y_ref in enumerate((yf_ref, ys_ref, yh_ref)):
        gate = _sigmoid(gl_ref[:, n * D_MODEL:(n + 1) * D_MODEL].astype(F32))
        term = gate * _dot(y_ref[...], wb_ref[n * BRANCH_W:(n + 1) * BRANCH_W, :])
        m = term if m is None else m + term
    o_ref[...] = x_ref[...] + _dot(m.astype(BF16), wo_ref[...])


def _merge(x, yf, ys, yh, gl, wb, wo, tm):
    n = x.shape[0]
    row = lambda c: pl.BlockSpec((tm, c), lambda i: (i, 0))
    return pl.pallas_call(
        _merge_kernel,
        grid=(n // tm,),
        in_specs=[row(D_MODEL), row(512), row(512), row(512), row(3 * D_MODEL),
                  _const_spec((3 * BRANCH_W, D_MODEL), single=True),
                  _const_spec((D_MODEL, D_MODEL), single=True)],
        out_specs=row(D_MODEL),
        out_shape=jax.ShapeDtypeStruct((n, D_MODEL), F32),
        compiler_params=_cparams(("parallel",)),
        name="merge_outproj",
    )(x, yf, ys, yh, gl, wb, wo)


def _memkv_kernel(m_ref, g_ref, wk_ref, wv_ref, gk_ref, k_ref, v_ref):
    h = _rms(m_ref[0], g_ref[...]).astype(BF16)
    kk = _dot(h, wk_ref[...])
    for hd in range(X_HEADS):
        sl = slice(hd * X_HD, (hd + 1) * X_HD)
        k_ref[0, :, sl] = _rms(kk[:, sl], gk_ref[...]).astype(BF16)
    v_ref[0] = _dot(h, wv_ref[...]).astype(BF16)


def _memkv(mem, g, wk, wv, gk):
    bsz, nm, _ = mem.shape
    spec = pl.BlockSpec((1, nm, D_MODEL), lambda b: (b, 0, 0))
    return pl.pallas_call(
        _memkv_kernel,
        grid=(bsz,),
        in_specs=[spec, _const_spec((1, D_MODEL)), _const_spec((D_MODEL, D_MODEL)),
                  _const_spec((D_MODEL, D_MODEL)), _const_spec((1, X_HD))],
        out_specs=[spec, spec],
        out_shape=[jax.ShapeDtypeStruct(mem.shape, BF16)] * 2,
        compiler_params=_cparams(("parallel",)),
        name="mem_kv",
    )(mem, g, wk, wv, gk)


def _xattn_kernel(x_ref, g_ref, wq_ref, gq_ref, k_ref, v_ref, wo_ref, o_ref):
    x = x_ref[...]
    h = _rms(x, g_ref[...]).astype(BF16)
    q = _dot(h, wq_ref[...])
    outs = []
    for hd in range(X_HEADS):
        sl = slice(hd * X_HD, (hd + 1) * X_HD)
        qh = (_rms(q[:, sl], gq_ref[...]) * (X_HD ** -0.5)).astype(BF16)
        s = _dot_nt(qh, k_ref[0, :, sl])
        p = jnp.exp(s - jnp.max(s, axis=-1, keepdims=True))
        l = jnp.sum(p, axis=-1, keepdims=True)
        outs.append((_dot(p.astype(BF16), v_ref[0, :, sl]) / l).astype(BF16))
    o_ref[...] = x + _dot(jnp.concatenate(outs, axis=1), wo_ref[...])


def _xattn(x, g, wq, gq, km, vm, wo, seq, tm):
    n = x.shape[0]
    nm = km.shape[1]
    per_b = seq // tm
    row = pl.BlockSpec((tm, D_MODEL), lambda i: (i, 0))
    kv = pl.BlockSpec((1, nm, D_MODEL), lambda i: (i // per_b, 0, 0))
    return pl.pallas_call(
        _xattn_kernel,
        grid=(n // tm,),
        in_specs=[row, _const_spec((1, D_MODEL)), _const_spec((D_MODEL, D_MODEL), single=True),
                  _const_spec((1, X_HD)), kv, kv, _const_spec((D_MODEL, D_MODEL), single=True)],
        out_specs=row,
        out_shape=jax.ShapeDtypeStruct((n, D_MODEL), F32),
        compiler_params=_cparams(("parallel",)),
        name="cross_attention",
    )(x, g, wq, gq, km, vm, wo)


FF_CHUNK = 256


def _ffn_kernel(x_ref, g_ref, wgu_ref, wd_ref, o_ref, act_ref):
    x = x_ref[...]
    h = _rms(x, g_ref[...]).astype(BF16)
    for c in range(0, D_FF, FF_CHUNK):
        a = _dot(h, wgu_ref[:, c:c + FF_CHUNK])
        b = _dot(h, wgu_ref[:, D_FF + c:D_FF + c + FF_CHUNK])
        act_ref[:, c:c + FF_CHUNK] = (a * _sigmoid(a) * b).astype(BF16)
    o_ref[...] = x + _dot(act_ref[...], wd_ref[...])


def _ffn(x, g, wgu, wd, tm):
    n = x.shape[0]
    row = pl.BlockSpec((tm, D_MODEL), lambda i: (i, 0))
    return pl.pallas_call(
        _ffn_kernel,
        grid=(n // tm,),
        in_specs=[row, _const_spec((1, D_MODEL)), _const_spec((D_MODEL, 2 * D_FF), single=True),
                  _const_spec((D_FF, D_MODEL), single=True)],
        out_specs=row,
        out_shape=jax.ShapeDtypeStruct((n, D_MODEL), F32),
        scratch_shapes=[pltpu.VMEM((tm, D_FF), BF16)],
        compiler_params=_cparams(("parallel",)),
        name="swiglu",
    )(x, g, wgu, wd)


def _split_w_in(w_in):
    depth = w_in.shape[0]

    def pad_heads(w):
        w = w.reshape(depth, D_MODEL, FOX_HEADS, FOX_HD)
        w = jnp.pad(w, ((0, 0), (0, 0), (0, 0), (0, FOX_PAD - FOX_HD)))
        return w.reshape(depth, D_MODEL, FOX_QK_W)

    fq, fk, fv = w_in[:, :, 0:512], w_in[:, :, 512:1024], w_in[:, :, 1024:1536]
    ff = w_in[:, :, 1536:1544]
    pad = jnp.zeros((depth, D_MODEL, A_END - A_FF - FOX_HEADS), w_in.dtype)
    w_fox = jnp.concatenate([pad_heads(fq), pad_heads(fk), fv, ff, pad], axis=-1).astype(BF16)
    return w_fox, w_in[:, :, 1544:].astype(BF16)


def _pad_head_vec(v, fill=0.0):
    v = jnp.concatenate([v.astype(F32), jnp.full((FOX_PAD - FOX_HD,), fill, F32)])
    return jnp.tile(v, FOX_HEADS).reshape(1, FOX_QK_W)


def kernel(x, mem, norm_mix, w_in, fox_fbias, fox_qnorm, fox_knorm, s5_a_re, s5_a_im, s5_b_re, s5_b_im, s5_c_re, s5_c_im, s5_d, s5_log_dt, s5_w_glu, s5_b_glu, hg_lb, hg_onorm, w_branch, w_out, norm_x, norm_mem, xq, xk, xv, xo, x_qnorm, x_knorm, norm_ffn, w_gate_up, w_down):
    bsz, seq, _ = x.shape
    depth = w_in.shape[0]
    n = bsz * seq
    tm = min(1024, seq)
    tq = min(512, seq)
    tt = min(256, seq)
    s5_tt = min(2048, seq)
    assert seq % tm == 0 and seq % tq == 0 and seq % tt == 0 and tt % HG_CHUNK == 0
    assert seq % s5_tt == 0 and s5_tt % (8 * S5_T) == 0

    row = lambda v: v.astype(F32).reshape(1, -1)
    w_fox, w_rest = _split_w_in(w_in)
    s5_wglu = s5_w_glu.astype(BF16)
    wb, wo = w_branch.astype(BF16), w_out.astype(BF16)
    wq, wk, wv, wxo = xq.astype(BF16), xk.astype(BF16), xv.astype(BF16), xo.astype(BF16)
    wgu, wd = w_gate_up.astype(BF16), w_down.astype(BF16)

    lb_all = jnp.cumsum(jax.nn.softmax(hg_lb.astype(F32), axis=0), axis=0)
    lb_all = lb_all - lb_all[0:1]

    lane = jnp.arange(256)
    hsum = ((lane[:, None] // FOX_PAD == lane[None, :] // FOX_PAD)
            & (lane[:, None] % FOX_PAD < FOX_HD)).astype(BF16)
    pad_lane = jnp.tile(jnp.arange(FOX_PAD), FOX_HEADS)
    k_lanes = (pad_lane >= FOX_HD) & (pad_lane < FOX_HD + FOX_BIAS_LANES)
    q_lanes = (pad_lane >= FOX_HD + FOX_BIAS_LANES) & (pad_lane < FOX_HD + 2 * FOX_BIAS_LANES)
    qone = k_lanes.astype(F32).reshape(1, FOX_QK_W)
    kone = q_lanes.astype(F32).reshape(1, FOX_QK_W)
    src = jnp.arange(128)
    dst = jnp.arange(2 * FOX_QK_W)
    src_n, src_h = src[:, None] // FOX_HEADS, src[:, None] % FOX_HEADS
    dst_key, dst_lane = dst[None, :] // FOX_QK_W, dst[None, :] % FOX_QK_W
    place = ((src_n < 2 * FOX_BIAS_LANES) & (dst_key == (src_n < FOX_BIAS_LANES))
             & (dst_lane // FOX_PAD == src_h) & (dst_lane % FOX_PAD == FOX_HD + src_n)).astype(BF16)

    col = jnp.arange(S5_KW)
    src = jnp.arange(S5_BLK)
    rep_th = ((src[:, None] // S5_GROUP == col[None, :] // S5_BLK)
              & (src[:, None] % S5_GROUP == col[None, :] % S5_GROUP)).astype(BF16)
    rep_rp = ((src[:, None] // S5_STATE == col[None, :] // S5_BST)
              & (src[:, None] % S5_STATE == col[None, :] % S5_STATE)).astype(BF16)

    xf = x.astype(F32).reshape(n, D_MODEL)
    for l in range(depth):
        gq = _pad_head_vec(fox_qnorm[l]) * (FOX_HD ** -0.5 * LOG2E)
        gk = _pad_head_vec(fox_knorm[l])
        fb = jnp.pad(fox_fbias[l].astype(F32), (0, 128 - FOX_HEADS)).reshape(1, 128)
        qk_bound = 1.01 * FOX_HD ** 0.5 * jnp.max(jnp.abs(fox_qnorm[l])) * jnp.max(jnp.abs(fox_knorm[l]))
        qk_bound2 = qk_bound * LOG2E
        qoff = jnp.full((1, 128), qk_bound2 - FOX_REF_MARGIN, F32)
        q, k, v, c = _inproj_fox(xf, row(norm_mix[l]), w_fox[l], gq, gk, qone, kone, hsum,
                                 fb, qoff, place, seq, tm)
        su, hq, hf, hi, hg, gl = _inproj_rest(xf, row(norm_mix[l]), w_rest[l], tm)

        jstart = _fox_first_blocks(c, qk_bound, bsz, seq, tq)
        bounded = (qk_bound2 <= FOX_BOUNDED_MAX).astype(jnp.int32).reshape(1)
        y_fox = _fox_attention(jstart, bounded, q, k, v, bsz, seq, tq)

        s5_w, s5_bst, s5_cout, s5_pw = _s5_params(s5_a_re[l], s5_a_im[l], s5_b_re[l], s5_b_im[l],
                                                  s5_c_re[l], s5_c_im[l], s5_log_dt[l])
        y_s5 = _s5(su, s5_w, s5_bst, s5_cout, rep_th, rep_rp, s5_pw, row(s5_d[l]), s5_wglu[l],
                   row(s5_b_glu[l]), bsz, seq, s5_tt)

        lb = lb_all[l].reshape(1, -1)
        y_hg = _hgrn(hq, hf, hi, hg, jnp.log(lb), jnp.log1p(-lb), 1.0 - lb, row(hg_onorm[l]), bsz, seq, tt)

        xf = _merge(xf, y_fox, y_s5, y_hg, gl, wb[l], wo[l], tm)

        km, vm = _memkv(mem.astype(F32), row(norm_mem[l]), wk[l], wv[l], row(x_knorm[l]))
        xf = _xattn(xf, row(norm_x[l]), wq[l], row(x_qnorm[l]), km, vm, wxo[l], seq, tm)
        xf = _ffn(xf, row(norm_ffn[l]), wgu[l], wd[l], tm)
    return xf.reshape(bsz, seq, D_MODEL).astype(x.dtype)
```

```python
import functools
import math

import jax
import jax.numpy as jnp
from jax import lax
from jax.experimental import pallas as pl
from jax.experimental.pallas import tpu as pltpu

F32 = jnp.float32
BF16 = jnp.bfloat16
HIGHEST = lax.Precision.HIGHEST

D_MODEL = 1024
BRANCH_W = 512
FOX_HD = 64
FOX_HEADS = 8
S5_GROUP = 16
S5_GROUPS = 32
S5_STATE = 64
S5_NSTATE = S5_GROUPS * S5_STATE
HG_HEADS = 4
HG_D = 128
HG_CHUNK = 64
HG_SUB = 8
X_HEADS = 4
X_HD = 256
D_FF = 2816
EPS = 1e-6

VMEM_LIMIT_BYTES = 56 * 1024 * 1024

FOX_PAD = 128
FOX_QK_W = FOX_HEADS * FOX_PAD
A_FQ, A_FK, A_FV, A_FF, A_END = 0, 1024, 2048, 2560, 2688
B_SU, B_HQ, B_HF, B_HI, B_HG, B_GL, B_END = 0, 512, 1024, 1536, 2048, 2560, 5632
FOX_BIAS_LANES = 3
LOG2E = 1.4426950408889634
FOX_SKIP_NATS = 30.0
FOX_REF_MARGIN = 100.0
FOX_BOUNDED_MAX = 100.0


NEG_BIG = -1e30


def _cparams(sem, vmem_limit_bytes=VMEM_LIMIT_BYTES):
    return pltpu.CompilerParams(dimension_semantics=sem, vmem_limit_bytes=vmem_limit_bytes)


def _rms(xf, g):
    return xf * lax.rsqrt(jnp.mean(xf * xf, axis=-1, keepdims=True) + EPS) * g


def _sigmoid(x):
    return 1.0 / (1.0 + jnp.exp(-x))


def _log_sigmoid(x):
    return jnp.minimum(x, 0.0) - jnp.log(1.0 + jnp.exp(-jnp.abs(x)))


def _dot(a, b):
    return jnp.dot(a, b, preferred_element_type=F32)


def _dot_nt(a, b):
    return lax.dot_general(a, b, (((1,), (1,)), ((), ())), preferred_element_type=F32)


def _dot_tn(a, b):
    return lax.dot_general(a, b, (((0,), (0,)), ((), ())), preferred_element_type=F32)


def _const_spec(shape, single=False, layer=None):
    nd = len(shape)
    if layer is not None:
        return pl.BlockSpec((None,) + tuple(shape), lambda *_: (layer,) + (0,) * nd,
                            pipeline_mode=pl.Buffered(1))
    if single:
        return pl.BlockSpec(shape, lambda *_: (0,) * nd, pipeline_mode=pl.Buffered(1))
    return pl.BlockSpec(shape, lambda *_: (0,) * nd)


def _top16(v):
    bits = lax.bitcast_convert_type(v, jnp.uint32) & jnp.uint32(0xFFFF0000)
    return lax.bitcast_convert_type(bits, F32)


def _split3(v):
    hi = _top16(v)
    r1 = v - hi
    mid = _top16(r1)
    return [hi, mid, r1 - mid]


CUM_ROWS = 256


def _inproj_fox_kernel(x_ref, g_ref, w_ref, gq_ref, gk_ref, qone_ref, kone_ref, hsum_ref, fb_ref, qoff_ref,
                       place_ref, q_ref, k_ref, v_ref, c_ref, carry_ref, *, tiles_per_seq):
    tm = x_ref.shape[0]

    @pl.when(pl.program_id(0) % tiles_per_seq == 0)
    def _():
        carry_ref[...] = jnp.zeros(carry_ref.shape, F32)

    h = _rms(x_ref[...], g_ref[...]).astype(BF16)

    def proj(lo, hi):
        return _dot(h, w_ref[:, lo:hi])

    def sumsq(t):
        return _dot((t * t).astype(BF16), hsum_ref[...])

    def headnorm(t, ss, g):
        return t * lax.rsqrt(ss * (1.0 / FOX_HD) + EPS) * g

    lf = _log_sigmoid(proj(A_FF, A_END) + fb_ref[...])
    r = lax.broadcasted_iota(jnp.int32, (CUM_ROWS, CUM_ROWS), 0)
    cc = lax.broadcasted_iota(jnp.int32, (CUM_ROWS, CUM_ROWS), 1)
    lower = (cc <= r).astype(F32)
    carry = carry_ref[...]
    parts = []
    for r0 in range(0, tm, CUM_ROWS):
        part = jnp.dot(lower, lf[r0:r0 + CUM_ROWS], precision=HIGHEST, preferred_element_type=F32) + carry
        carry = part[CUM_ROWS - 1:CUM_ROWS, :]
        parts.append(part)
    carry_ref[...] = carry
    cs = jnp.concatenate(parts, axis=0)
    c_ref[...] = cs
    d = -(cs * LOG2E)
    lane = lax.broadcasted_iota(jnp.int32, (tm, 128), 1)
    packed = None
    for n, piece in enumerate(_split3(d) + _split3(-d - qoff_ref[...])):
        term = jnp.where(lane < FOX_HEADS, piece, 0.0)
        term = term if n == 0 else pltpu.roll(term, FOX_HEADS * n, 1)
        packed = term if packed is None else packed + term
    bias = _dot(packed.astype(BF16), place_ref[...])
    qbias, kbias = bias[:, :FOX_QK_W], bias[:, FOX_QK_W:]

    chunks = range(0, FOX_QK_W, 256)
    tq = [proj(A_FQ + c, A_FQ + c + 256) for c in chunks]
    tk = [proj(A_FK + c, A_FK + c + 256) for c in chunks]
    sq = [sumsq(t) for t in tq]
    sk = [sumsq(t) for t in tk]
    for n, c in enumerate(chunks):
        sl = slice(c, c + 256)
        q_ref[:, sl] = (headnorm(tq[n], sq[n], gq_ref[:, sl]) + qone_ref[:, sl] + qbias[:, sl]).astype(BF16)
        k_ref[:, sl] = (headnorm(tk[n], sk[n], gk_ref[:, sl]) + kone_ref[:, sl] + kbias[:, sl]).astype(BF16)
    v_ref[...] = proj(A_FV, A_FF).astype(BF16)


def _inproj_fox(x, g, w, layer, gq, gk, qone, kone, hsum, fb, qoff, place, seq, tm):
    n = x.shape[0]
    row = lambda c: pl.BlockSpec((tm, c), lambda i: (i, 0))
    outs = [(FOX_QK_W, BF16)] * 2 + [(512, BF16), (128, F32)]
    vec = _const_spec((1, FOX_QK_W))
    return pl.pallas_call(
        functools.partial(_inproj_fox_kernel, tiles_per_seq=seq // tm),
        grid=(n // tm,),
        in_specs=[row(D_MODEL), _const_spec((1, D_MODEL)), _const_spec((D_MODEL, A_END), layer=layer),
                  vec, vec, vec, vec, _const_spec((256, 256)), _const_spec((1, 128)), _const_spec((1, 128)),
                  _const_spec((128, 2 * FOX_QK_W))],
        out_specs=[row(c) for c, _ in outs],
        out_shape=[jax.ShapeDtypeStruct((n, c), dt) for c, dt in outs],
        scratch_shapes=[pltpu.VMEM((1, 128), F32)],
        compiler_params=_cparams(("arbitrary",)),
        name="inproj_fox",
    )(x, g, w, gq, gk, qone, kone, hsum, fb, qoff, place)


def _inproj_rest_kernel(x_ref, g_ref, w_ref, su_ref, hq_ref, hf_ref, hi_ref, hg_ref, gl_ref):
    h = _rms(x_ref[...], g_ref[...]).astype(BF16)

    def proj(lo, hi):
        return _dot(h, w_ref[:, lo:hi])

    su_ref[...] = proj(B_SU, B_HQ).astype(BF16)
    hq_ref[...] = proj(B_HQ, B_HF).astype(BF16)
    hf_ref[...] = proj(B_HF, B_HI)
    hi_ref[...] = proj(B_HI, B_HG).astype(BF16)
    hg_ref[...] = proj(B_HG, B_GL).astype(BF16)
    for c in range(B_GL, B_END, 512):
        gl_ref[:, c - B_GL:c - B_GL + 512] = proj(c, c + 512).astype(BF16)


def _inproj_rest(x, g, w, layer, tm):
    n = x.shape[0]
    row = lambda c: pl.BlockSpec((tm, c), lambda i: (i, 0))
    outs = [(512, BF16)] * 2 + [(512, F32)] + [(512, BF16)] * 2 + [(3072, BF16)]
    return pl.pallas_call(
        _inproj_rest_kernel,
        grid=(n // tm,),
        in_specs=[row(D_MODEL), _const_spec((1, D_MODEL)), _const_spec((D_MODEL, B_END), layer=layer)],
        out_specs=[row(c) for c, _ in outs],
        out_shape=[jax.ShapeDtypeStruct((n, c), dt) for c, dt in outs],
        compiler_params=_cparams(("parallel",)),
        name="inproj_rest",
    )(x, g, w)


def _fox_kernel(jstart_ref, bounded_ref, q_ref, k_ref, v_ref, o_ref, m_ref, l_ref, acc_ref, *, tq, nq):
    b, p, i = pl.program_id(0), pl.program_id(1), pl.program_id(2)
    nchunk = tq // 128
    m_ref[...] = jnp.full(m_ref.shape, NEG_BIG, F32)
    l_ref[...] = jnp.zeros(l_ref.shape, F32)
    acc_ref[...] = jnp.zeros(acc_ref.shape, F32)

    def logits(j, hh, masked):
        start = pl.multiple_of(j * tq, tq)
        sl = slice(hh * FOX_PAD, (hh + 1) * FOX_PAD)
        s = _dot_nt(q_ref[:, sl], k_ref[pl.ds(start, tq), sl])
        if masked:
            rq = lax.broadcasted_iota(jnp.int32, (tq, tq), 0)
            ck = lax.broadcasted_iota(jnp.int32, (tq, tq), 1)
            s = jnp.where(ck <= rq, s, NEG_BIG)
        return s, v_ref[pl.ds(start, tq), :]

    def lane_partial_sum(pr):
        ls = pr[:, 0:128]
        for c in range(1, nchunk):
            ls = ls + pr[:, c * 128:(c + 1) * 128]
        return ls

    def step_bounded(j, masked):
        for hh in range(2):
            s, vb = logits(j, hh, masked)
            pr = jnp.exp2(s)
            l_ref[hh] += lane_partial_sum(pr)
            acc_ref[hh] += _dot(pr.astype(BF16), vb)

    def step_online(j, masked):
        for hh in range(2):
            s, vb = logits(j, hh, masked)
            mc = s[:, 0:128]
            for c in range(1, nchunk):
                mc = jnp.maximum(mc, s[:, c * 128:(c + 1) * 128])
            m_prev = m_ref[hh]
            m_new = jnp.maximum(m_prev, jnp.max(mc, axis=-1, keepdims=True))
            alpha = jnp.exp2(m_prev - m_new)
            pr = jnp.exp2(s - jnp.concatenate([m_new] * nchunk, axis=1))
            l_ref[hh] = alpha * l_ref[hh] + lane_partial_sum(pr)
            acc_ref[hh] = alpha * acc_ref[hh] + _dot(pr.astype(BF16), vb)
            m_ref[hh] = m_new

    base = ((b * (FOX_HEADS // 2) + p) * 2) * nq + i
    j0 = jnp.minimum(jstart_ref[base], jstart_ref[base + nq])
    count = i - j0

    def sweep(step):
        def body(n, carry):
            step(j0 + 2 * n, False)
            step(j0 + 2 * n + 1, False)
            return carry

        lax.fori_loop(0, count // 2, body, 0)

        @pl.when(count % 2 == 1)
        def _():
            step(i - 1, False)

        step(i, True)

    bounded = bounded_ref[0] == 1

    @pl.when(bounded)
    def _():
        sweep(step_bounded)

    @pl.when(jnp.logical_not(bounded))
    def _():
        sweep(step_online)

    lane = lax.broadcasted_iota(jnp.int32, (tq, 128), 1)
    o0 = acc_ref[0] / jnp.sum(l_ref[0], axis=-1, keepdims=True)
    o1 = acc_ref[1] / jnp.sum(l_ref[1], axis=-1, keepdims=True)
    o_ref[...] = jnp.where(lane < FOX_HD, o0, o1).astype(BF16)


def _fox_attention(jstart, bounded, q, k, v, bsz, seq, tq):
    n = q.shape[0]
    nq = seq // tq
    grid_spec = pltpu.PrefetchScalarGridSpec(
        num_scalar_prefetch=2,
        grid=(bsz, FOX_HEADS // 2, nq),
        in_specs=[pl.BlockSpec((tq, 2 * FOX_PAD), lambda b, p, i, js, bd: (b * nq + i, p)),
                  pl.BlockSpec((seq, 2 * FOX_PAD), lambda b, p, i, js, bd: (b, p)),
                  pl.BlockSpec((seq, 128), lambda b, p, i, js, bd: (b, p))],
        out_specs=pl.BlockSpec((tq, 128), lambda b, p, i, js, bd: (b * nq + i, p)),
        scratch_shapes=[pltpu.VMEM((2, tq, 128), F32), pltpu.VMEM((2, tq, 128), F32),
                        pltpu.VMEM((2, tq, 128), F32)],
    )
    return pl.pallas_call(
        functools.partial(_fox_kernel, tq=tq, nq=nq),
        grid_spec=grid_spec,
        out_shape=jax.ShapeDtypeStruct((n, 512), BF16),
        compiler_params=_cparams(("parallel", "parallel", "arbitrary")),
        name="fox_attention",
    )(jstart, bounded, q, k, v)


def _fox_first_blocks(c, qk_bound, bsz, seq, tq):
    nq = seq // tq
    cb = c.reshape(bsz, nq, tq, 128)
    c_end = cb[:, :, tq - 1, :FOX_HEADS].transpose(0, 2, 1)
    c_start = cb[:, :, 0, :FOX_HEADS].transpose(0, 2, 1)
    gap = c_end[:, :, None, :] - c_start[:, :, :, None]
    skip = gap > (2.0 * qk_bound + FOX_SKIP_NATS)
    jstart = jnp.sum(skip.astype(jnp.int32), axis=-1)
    jstart = jnp.minimum(jstart, jnp.arange(nq, dtype=jnp.int32)[None, None, :])
    return jstart.reshape(-1)


S5_T = 8
S5_BLK = 128
S5_NBLK = BRANCH_W // S5_BLK
S5_KW = S5_T * S5_BLK
S5_BST = S5_NSTATE // S5_NBLK
S5_VMEM_LIMIT_BYTES = 58 * 1024 * 1024


def _s5_expand(src_ref, rep_ref, dst_ref, row_div, col_div):
    n = S5_KW
    for j in range(S5_NBLK):
        for c in range(0, n, 256):
            rg = (lax.broadcasted_iota(jnp.int32, (n, 256), 0) // row_div) % 8
            cg = ((lax.broadcasted_iota(jnp.int32, (n, 256), 1) + c) // col_div) % 8
            rep = _dot(src_ref[j], rep_ref[:, c:c + 256])
            dst_ref[j, :, c:c + 256] = jnp.where(rg == cg, rep, 0.0).astype(BF16)


def _s5_kernel(u_ref, wc_ref, bc_ref, cc_ref, rep_th_ref, rep_rp_ref, pw_ref, d_ref, wglu_ref, bglu_ref,
               o_ref, w_ref, bst_ref, cout_ref, uf_ref, y_ref, hr_ref, hi_ref, cr_ref, ci_ref, *, tt):
    t = pl.program_id(1)
    rows = tt // S5_T

    @pl.when(t == 0)
    def _():
        cr_ref[...] = jnp.zeros(cr_ref.shape, F32)
        ci_ref[...] = jnp.zeros(ci_ref.shape, F32)
        _s5_expand(wc_ref, rep_th_ref, w_ref, S5_GROUP, S5_GROUP)
        _s5_expand(bc_ref, rep_rp_ref, bst_ref, S5_GROUP, S5_STATE)
        _s5_expand(cc_ref, rep_th_ref, cout_ref, S5_STATE, S5_GROUP)

    for j in range(S5_NBLK):
        uf_ref[j] = u_ref[:, j * S5_BLK:(j + 1) * S5_BLK].astype(F32)

    def chunk_rows(j):
        return jnp.concatenate([uf_ref[j, pl.ds(s, rows, stride=S5_T), :] for s in range(S5_T)],
                               axis=1).astype(BF16)

    for j in range(S5_NBLK):
        hin = _dot(chunk_rows(j), bst_ref[j])
        hr_ref[:, j * S5_BST:(j + 1) * S5_BST] = hin[:, :S5_BST]
        hi_ref[:, j * S5_BST:(j + 1) * S5_BST] = hin[:, S5_BST:]

    row8 = lax.broadcasted_iota(jnp.int32, (8, S5_NSTATE), 0)

    def scan_block(r, carry):
        cr, ci = carry
        rws = pl.ds(pl.multiple_of(r * 8, 8), 8)
        xr = hr_ref[rws, :]
        xi = hi_ref[rws, :]
        for n, k in enumerate((1, 2, 4)):
            ar = pw_ref[2 * n]
            ai = pw_ref[2 * n + 1]
            sr = pltpu.roll(xr, k, 0)
            si = pltpu.roll(xi, k, 0)
            xr, xi = xr + ar * sr - ai * si, xi + ar * si + ai * sr
        pr = pw_ref[6]
        pi = pw_ref[7]
        xr, xi = xr + pr * cr - pi * ci, xi + pr * ci + pi * cr
        hr_ref[rws, :] = jnp.where(row8 == 0, cr, pltpu.roll(xr, 1, 0))
        hi_ref[rws, :] = jnp.where(row8 == 0, ci, pltpu.roll(xi, 1, 0))
        return (jnp.broadcast_to(xr[7:8, :], xr.shape), jnp.broadcast_to(xi[7:8, :], xi.shape))

    cr, ci = lax.fori_loop(0, rows // 8, scan_block, (cr_ref[...], ci_ref[...]))
    cr_ref[...] = cr
    ci_ref[...] = ci

    for j in range(S5_NBLK):
        st = slice(j * S5_BST, (j + 1) * S5_BST)
        hb = jnp.concatenate([hr_ref[:, st], hi_ref[:, st]], axis=1).astype(BF16)
        yj = _dot(chunk_rows(j), w_ref[j]) + _dot(hb, cout_ref[j])
        for s in range(S5_T):
            y_ref[j, pl.ds(s, rows, stride=S5_T), :] = yj[:, s * S5_BLK:(s + 1) * S5_BLK]

    y = jnp.concatenate([y_ref[j] + d_ref[:, j * S5_BLK:(j + 1) * S5_BLK] * uf_ref[j]
                         for j in range(S5_NBLK)], axis=1)
    z = 0.5 * y * (1.0 + jnp.tanh(math.sqrt(2.0 / math.pi) * (y + 0.044715 * (y * y * y))))
    gate = _sigmoid(_dot(z.astype(BF16), wglu_ref[...]) + bglu_ref[...])
    o_ref[...] = (z * gate).astype(BF16)


def _s5(u, wc, bc, cc, rep_th, rep_rp, pw, d, wglu, bglu, bsz, seq, tt):
    n = u.shape[0]
    nt = seq // tt
    rowspec = pl.BlockSpec((tt, BRANCH_W), lambda b, t: (b * nt + t, 0))
    compact = _const_spec((S5_NBLK, S5_KW, S5_BLK), single=True)
    expanded = pltpu.VMEM((S5_NBLK, S5_KW, S5_KW), BF16)
    return pl.pallas_call(
        functools.partial(_s5_kernel, tt=tt),
        grid=(bsz, nt),
        in_specs=[rowspec, compact, compact, compact,
                  _const_spec((S5_BLK, S5_KW), single=True), _const_spec((S5_BLK, S5_KW), single=True),
                  _const_spec((8, 8, S5_NSTATE), single=True), _const_spec((1, BRANCH_W)),
                  _const_spec((BRANCH_W, BRANCH_W)), _const_spec((1, BRANCH_W))],
        out_specs=rowspec,
        out_shape=jax.ShapeDtypeStruct((n, BRANCH_W), BF16),
        scratch_shapes=[expanded, expanded, expanded,
                        pltpu.VMEM((S5_NBLK, tt, S5_BLK), F32), pltpu.VMEM((S5_NBLK, tt, S5_BLK), F32),
                        pltpu.VMEM((tt // S5_T, S5_NSTATE), F32), pltpu.VMEM((tt // S5_T, S5_NSTATE), F32),
                        pltpu.VMEM((8, S5_NSTATE), F32), pltpu.VMEM((8, S5_NSTATE), F32)],
        compiler_params=_cparams(("parallel", "arbitrary"), S5_VMEM_LIMIT_BYTES),
        name="s5_mixer",
    )(u, wc, bc, cc, rep_th, rep_rp, pw, d, wglu, bglu)


def _s5_params(a_re, a_im, b_re, b_im, c_re, c_im, log_dt):
    g, p, gs, tt = S5_GROUPS, S5_STATE, S5_GROUP, S5_T
    dt = jnp.exp(log_dt.astype(F32))[:, None]
    ar, ai = a_re.astype(F32), a_im.astype(F32)
    mag = jnp.exp(dt * ar)
    abar_r, abar_i = mag * jnp.cos(dt * ai), mag * jnp.sin(dt * ai)
    inv_den = 1.0 / (ar * ar + ai * ai)
    nr, ni = abar_r - 1.0, abar_i
    coef_r = (nr * ar + ni * ai) * inv_den
    coef_i = (ni * ar - nr * ai) * inv_den
    br, bi = b_re.astype(F32), b_im.astype(F32)
    bbar_r = coef_r[..., None] * br - coef_i[..., None] * bi
    bbar_i = coef_r[..., None] * bi + coef_i[..., None] * br
    cr, ci = c_re.astype(F32), c_im.astype(F32)

    pows = [(jnp.ones_like(abar_r), jnp.zeros_like(abar_i))]
    for _ in range(tt):
        pr, pi = pows[-1]
        pows.append((pr * abar_r - pi * abar_i, pr * abar_i + pi * abar_r))
    pw_r = jnp.stack([x[0] for x in pows])
    pw_i = jnp.stack([x[1] for x in pows])

    cb_r = cr[:, :, :, None] * bbar_r[:, None, :, :] - ci[:, :, :, None] * bbar_i[:, None, :, :]
    cb_i = cr[:, :, :, None] * bbar_i[:, None, :, :] + ci[:, :, :, None] * bbar_r[:, None, :, :]
    kl = (jnp.einsum('lgp,ghpk->lghk', pw_r[:tt], cb_r, precision=HIGHEST)
          - jnp.einsum('lgp,ghpk->lghk', pw_i[:tt], cb_i, precision=HIGHEST))
    nb, gl = S5_NBLK, g // S5_NBLK

    lag = jnp.arange(tt)[None, :] - jnp.arange(tt)[:, None]
    m = jnp.where((lag >= 0)[:, :, None, None, None], kl[jnp.clip(lag, 0, tt - 1)], 0.0)
    m = m.transpose(2, 0, 4, 1, 3).reshape(nb, gl, tt, gs, tt * gs)
    w = m.transpose(0, 2, 1, 3, 4).reshape(nb, S5_KW, S5_BLK)

    e_r, e_i = pw_r[:tt][::-1], pw_i[:tt][::-1]
    bs_r = e_r[:, :, :, None] * bbar_r[None] - e_i[:, :, :, None] * bbar_i[None]
    bs_i = e_r[:, :, :, None] * bbar_i[None] + e_i[:, :, :, None] * bbar_r[None]
    bs = jnp.stack([bs_r, bs_i]).transpose(2, 1, 4, 0, 3).reshape(nb, gl, tt, gs, 2 * p)
    bst = bs.transpose(0, 2, 1, 3, 4).reshape(nb, S5_KW, S5_BLK)

    q_r, q_i = pw_r[1:tt + 1], pw_i[1:tt + 1]
    co_r = cr[None] * q_r[:, :, None, :] - ci[None] * q_i[:, :, None, :]
    co_i = cr[None] * q_i[:, :, None, :] + ci[None] * q_r[:, :, None, :]
    co = jnp.stack([co_r, -co_i]).transpose(2, 0, 4, 1, 3).reshape(nb, gl, 2, p, tt * gs)
    cout = co.transpose(0, 2, 1, 3, 4).reshape(nb, S5_KW, S5_BLK)

    ar1, ai1 = pw_r[tt].reshape(-1), pw_i[tt].reshape(-1)
    apow = [(ar1, ai1)]
    for _ in range(7):
        pr, pi = apow[-1]
        apow.append((pr * ar1 - pi * ai1, pr * ai1 + pi * ar1))
    rows8 = jnp.arange(8)[:, None]
    tabs = []
    for k in (1, 2, 4):
        mask = (rows8 >= k).astype(F32)
        tabs += [mask * apow[k - 1][0][None, :], mask * apow[k - 1][1][None, :]]
    tabs.append(jnp.stack([apow[r][0] for r in range(8)]))
    tabs.append(jnp.stack([apow[r][1] for r in range(8)]))
    pw = jnp.stack(tabs)
    return w.astype(BF16), bst.astype(BF16), cout.astype(BF16), pw


def _hgrn_kernel(q_ref, f_ref, i_ref, g_ref, loglb_ref, log1mlb_ref, onemlb_ref, gain_ref,
                 o_ref, st_ref, b_ref, key_ref, qf_ref, oacc_ref, *, tt):
    t = pl.program_id(1)

    @pl.when(t == 0)
    def _():
        st_ref[...] = jnp.zeros(st_ref.shape, F32)

    c_sz, sub = HG_CHUNK, HG_SUB
    n_sub = c_sz // sub
    row_s = lax.broadcasted_iota(jnp.int32, (sub, 1), 0)
    lane_s = lax.broadcasted_iota(jnp.int32, (sub, c_sz), 1)

    z = f_ref[...]
    loglb = loglb_ref[...]
    bb = log1mlb_ref[...] + _log_sigmoid(z)
    logf = jnp.maximum(loglb, bb) + jnp.log(1.0 + jnp.exp(-jnp.abs(loglb - bb)))
    key_ref[...] = onemlb_ref[...] * (1.0 / (1.0 + jnp.exp(z)))
    qx = q_ref[...].astype(F32)
    qf_ref[...] = qx * _sigmoid(qx)
    rr = lax.broadcasted_iota(jnp.int32, (tt, tt), 0)
    cc = lax.broadcasted_iota(jnp.int32, (tt, tt), 1)
    lower = ((cc <= rr) & (rr // c_sz == cc // c_sz)).astype(F32)
    b_ref[...] = jnp.dot(lower, logf * LOG2E, precision=HIGHEST, preferred_element_type=F32)

    def head_chunk(rows, hd):
        sl = slice(hd * HG_D, (hd + 1) * HG_D)
        key = key_ref[rows, sl]
        qf = qf_ref[rows, sl]
        vb = i_ref[rows, sl]
        b = b_ref[rows, sl]
        b_last = b[c_sz - 1:c_sz, :]
        st = st_ref[hd]
        o_inter = _dot_nt((qf * jnp.exp2(b)).astype(BF16), st.astype(BF16))
        kd = (key * jnp.exp2(b_last - b)).astype(BF16)
        st_ref[hd] = st * jnp.exp2(b_last) + _dot_tn(vb, kd)

        srows = []
        for blk in range(n_sub):
            lo = blk * sub
            b_i = b[lo:lo + sub]
            q_i = qf[lo:lo + sub]
            k_i = key[lo:lo + sub]
            sd = jnp.zeros((sub, c_sz), F32)
            for s in range(sub):
                e = jnp.exp2(b_i - b_i[s:s + 1])
                col = jnp.sum(q_i * e * k_i[s:s + 1], axis=-1, keepdims=True)
                sd = jnp.where(lane_s == lo + s, col, sd)
            sd = jnp.where(lane_s <= lo + row_s, sd, 0.0)
            if blk > 0:
                ref = b[lo - 1:lo]
                qt = (q_i * jnp.exp2(b_i - ref)).astype(BF16)
                kt = jnp.concatenate([key[:lo] * jnp.exp2(ref - b[:lo]), jnp.zeros((c_sz - lo, HG_D), F32)],
                                     axis=0).astype(BF16)
                sd = jnp.where(lane_s < lo, _dot_nt(qt, kt), sd)
            srows.append(sd)
        scores = jnp.concatenate(srows, axis=0)
        oacc_ref[rows, sl] = o_inter + _dot(scores.astype(BF16), vb)

    def chunk(c, carry):
        rows = pl.ds(pl.multiple_of(c * c_sz, c_sz), c_sz)
        for hd in range(HG_HEADS):
            head_chunk(rows, hd)
        return carry

    lax.fori_loop(0, tt // c_sz, chunk, 0)

    for hd in range(HG_HEADS):
        sl = slice(hd * HG_D, (hd + 1) * HG_D)
        o = oacc_ref[:, sl]
        gx = g_ref[:, sl].astype(F32)
        y = o * lax.rsqrt(jnp.mean(o * o, axis=-1, keepdims=True) + EPS) * gain_ref[...]
        o_ref[:, sl] = (y * (gx * _sigmoid(gx))).astype(BF16)


def _hgrn(hq, hf, hi, hg, loglb, log1mlb, onemlb, gain, bsz, seq, tt):
    n = hq.shape[0]
    nt = seq // tt
    spec = pl.BlockSpec((tt, 512), lambda b, t: (b * nt + t, 0))
    return pl.pallas_call(
        functools.partial(_hgrn_kernel, tt=tt),
        grid=(bsz, nt),
        in_specs=[spec, spec, spec, spec, _const_spec((1, 512)), _const_spec((1, 512)),
                  _const_spec((1, 512)), _const_spec((1, HG_D))],
        out_specs=spec,
        out_shape=jax.ShapeDtypeStruct((n, 512), BF16),
        scratch_shapes=[pltpu.VMEM((HG_HEADS, HG_D, HG_D), F32)] + [pltpu.VMEM((tt, 512), F32)] * 4,
        compiler_params=_cparams(("parallel", "arbitrary")),
        name="hgrn2_mixer",
    )(hq, hf, hi, hg, loglb, log1mlb, onemlb, gain)


def _merge_kernel(x_ref, yf_ref, ys_ref, yh_ref, gl_ref, wb_ref, wo_ref, o_ref):
    m = None
    for n, y_ref in enumerate((yf_ref, ys_ref, yh_ref)):
        gate = _sigmoid(gl_ref[:, n * D_MODEL:(n + 1) * D_MODEL].astype(F32))
        term = gate * _dot(y_ref[...], wb_ref[n * BRANCH_W:(n + 1) * BRANCH_W, :])
        m = term if m is None else m + term
    o_ref[...] = x_ref[...] + _dot(m.astype(BF16), wo_ref[...])


def _merge(x, yf, ys, yh, gl, wb, wo, layer, tm):
    n = x.shape[0]
    row = lambda c: pl.BlockSpec((tm, c), lambda i: (i, 0))
    return pl.pallas_call(
        _merge_kernel,
        grid=(n // tm,),
        in_specs=[row(D_MODEL), row(512), row(512), row(512), row(3 * D_MODEL),
                  _const_spec((3 * BRANCH_W, D_MODEL), layer=layer),
                  _const_spec((D_MODEL, D_MODEL), layer=layer)],
        out_specs=row(D_MODEL),
        out_shape=jax.ShapeDtypeStruct((n, D_MODEL), F32),
        compiler_params=_cparams(("parallel",)),
        name="merge_outproj",
    )(x, yf, ys, yh, gl, wb, wo)


def _memkv_kernel(m_ref, g_ref, wk_ref, wv_ref, gk_ref, k_ref, v_ref):
    h = _rms(m_ref[0], g_ref[...]).astype(BF16)
    kk = _dot(h, wk_ref[...])
    for hd in range(X_HEADS):
        sl = slice(hd * X_HD, (hd + 1) * X_HD)
        k_ref[0, :, sl] = _rms(kk[:, sl], gk_ref[...]).astype(BF16)
    v_ref[0] = _dot(h, wv_ref[...]).astype(BF16)


def _memkv(mem, g, wk, wv, gk):
    bsz, nm, _ = mem.shape
    spec = pl.BlockSpec((1, nm, D_MODEL), lambda b: (b, 0, 0))
    return pl.pallas_call(
        _memkv_kernel,
        grid=(bsz,),
        in_specs=[spec, _const_spec((1, D_MODEL)), _const_spec((D_MODEL, D_MODEL)),
                  _const_spec((D_MODEL, D_MODEL)), _const_spec((1, X_HD))],
        out_specs=[spec, spec],
        out_shape=[jax.ShapeDtypeStruct(mem.shape, BF16)] * 2,
        compiler_params=_cparams(("parallel",)),
        name="mem_kv",
    )(mem, g, wk, wv, gk)


def _xattn_kernel(x_ref, g_ref, wq_ref, gq_ref, k_ref, v_ref, wo_ref, o_ref):
    x = x_ref[...]
    h = _rms(x, g_ref[...]).astype(BF16)
    q = _dot(h, wq_ref[...])
    outs = []
    for hd in range(X_HEADS):
        sl = slice(hd * X_HD, (hd + 1) * X_HD)
        qh = (_rms(q[:, sl], gq_ref[...]) * (X_HD ** -0.5)).astype(BF16)
        s = _dot_nt(qh, k_ref[0, :, sl])
        p = jnp.exp(s - jnp.max(s, axis=-1, keepdims=True))
        l = jnp.sum(p, axis=-1, keepdims=True)
        outs.append((_dot(p.astype(BF16), v_ref[0, :, sl]) / l).astype(BF16))
    o_ref[...] = x + _dot(jnp.concatenate(outs, axis=1), wo_ref[...])


def _xattn(x, g, wq, gq, km, vm, wo, layer, seq, tm):
    n = x.shape[0]
    nm = km.shape[1]
    per_b = seq // tm
    row = pl.BlockSpec((tm, D_MODEL), lambda i: (i, 0))
    kv = pl.BlockSpec((1, nm, D_MODEL), lambda i: (i // per_b, 0, 0))
    return pl.pallas_call(
        _xattn_kernel,
        grid=(n // tm,),
        in_specs=[row, _const_spec((1, D_MODEL)), _const_spec((D_MODEL, D_MODEL), layer=layer),
                  _const_spec((1, X_HD)), kv, kv, _const_spec((D_MODEL, D_MODEL), layer=layer)],
        out_specs=row,
        out_shape=jax.ShapeDtypeStruct((n, D_MODEL), F32),
        compiler_params=_cparams(("parallel",)),
        name="cross_attention",
    )(x, g, wq, gq, km, vm, wo)


FF_CHUNK = 256


def _ffn_kernel(x_ref, g_ref, wgu_ref, wd_ref, o_ref, act_ref):
    x = x_ref[...]
    h = _rms(x, g_ref[...]).astype(BF16)
    for c in range(0, D_FF, FF_CHUNK):
        a = _dot(h, wgu_ref[:, c:c + FF_CHUNK])
        b = _dot(h, wgu_ref[:, D_FF + c:D_FF + c + FF_CHUNK])
        act_ref[:, c:c + FF_CHUNK] = (a * _sigmoid(a) * b).astype(BF16)
    o_ref[...] = x + _dot(act_ref[...], wd_ref[...])


def _ffn(x, g, wgu, wd, layer, tm):
    n = x.shape[0]
    row = pl.BlockSpec((tm, D_MODEL), lambda i: (i, 0))
    return pl.pallas_call(
        _ffn_kernel,
        grid=(n // tm,),
        in_specs=[row, _const_spec((1, D_MODEL)), _const_spec((D_MODEL, 2 * D_FF), layer=layer),
                  _const_spec((D_FF, D_MODEL), layer=layer)],
        out_specs=row,
        out_shape=jax.ShapeDtypeStruct((n, D_MODEL), F32),
        scratch_shapes=[pltpu.VMEM((tm, D_FF), BF16)],
        compiler_params=_cparams(("parallel",)),
        name="swiglu",
    )(x, g, wgu, wd)


def _split_w_in(w_in):
    depth = w_in.shape[0]

    def pad_heads(w):
        w = w.reshape(depth, D_MODEL, FOX_HEADS, FOX_HD)
        w = jnp.pad(w, ((0, 0), (0, 0), (0, 0), (0, FOX_PAD - FOX_HD)))
        return w.reshape(depth, D_MODEL, FOX_QK_W)

    fq, fk, fv = w_in[:, :, 0:512], w_in[:, :, 512:1024], w_in[:, :, 1024:1536]
    ff = w_in[:, :, 1536:1544]
    pad = jnp.zeros((depth, D_MODEL, A_END - A_FF - FOX_HEADS), w_in.dtype)
    w_fox = jnp.concatenate([pad_heads(fq), pad_heads(fk), fv, ff, pad], axis=-1).astype(BF16)
    return w_fox, w_in[:, :, 1544:].astype(BF16)


def _pad_head_vec(v, fill=0.0):
    v = jnp.concatenate([v.astype(F32), jnp.full((FOX_PAD - FOX_HD,), fill, F32)])
    return jnp.tile(v, FOX_HEADS).reshape(1, FOX_QK_W)


def kernel(x, mem, norm_mix, w_in, fox_fbias, fox_qnorm, fox_knorm, s5_a_re, s5_a_im, s5_b_re, s5_b_im, s5_c_re, s5_c_im, s5_d, s5_log_dt, s5_w_glu, s5_b_glu, hg_lb, hg_onorm, w_branch, w_out, norm_x, norm_mem, xq, xk, xv, xo, x_qnorm, x_knorm, norm_ffn, w_gate_up, w_down):
    bsz, seq, _ = x.shape
    depth = w_in.shape[0]
    n = bsz * seq
    tm = min(1024, seq)
    tq = min(512, seq)
    tt = min(256, seq)
    s5_tt = min(2048, seq)
    assert seq % tm == 0 and seq % tq == 0 and seq % tt == 0 and tt % HG_CHUNK == 0
    assert seq % s5_tt == 0 and s5_tt % (8 * S5_T) == 0

    row = lambda v: v.astype(F32).reshape(1, -1)
    w_fox, w_rest = _split_w_in(w_in)
    s5_wglu = s5_w_glu.astype(BF16)
    wb, wo = w_branch.astype(BF16), w_out.astype(BF16)
    wq, wk, wv, wxo = xq.astype(BF16), xk.astype(BF16), xv.astype(BF16), xo.astype(BF16)
    wgu, wd = w_gate_up.astype(BF16), w_down.astype(BF16)

    lb_all = jnp.cumsum(jax.nn.softmax(hg_lb.astype(F32), axis=0), axis=0)
    lb_all = lb_all - lb_all[0:1]

    lane = jnp.arange(256)
    hsum = ((lane[:, None] // FOX_PAD == lane[None, :] // FOX_PAD)
            & (lane[:, None] % FOX_PAD < FOX_HD)).astype(BF16)
    pad_lane = jnp.tile(jnp.arange(FOX_PAD), FOX_HEADS)
    k_lanes = (pad_lane >= FOX_HD) & (pad_lane < FOX_HD + FOX_BIAS_LANES)
    q_lanes = (pad_lane >= FOX_HD + FOX_BIAS_LANES) & (pad_lane < FOX_HD + 2 * FOX_BIAS_LANES)
    qone = k_lanes.astype(F32).reshape(1, FOX_QK_W)
    kone = q_lanes.astype(F32).reshape(1, FOX_QK_W)
    src = jnp.arange(128)
    dst = jnp.arange(2 * FOX_QK_W)
    src_n, src_h = src[:, None] // FOX_HEADS, src[:, None] % FOX_HEADS
    dst_key, dst_lane = dst[None, :] // FOX_QK_W, dst[None, :] % FOX_QK_W
    place = ((src_n < 2 * FOX_BIAS_LANES) & (dst_key == (src_n < FOX_BIAS_LANES))
             & (dst_lane // FOX_PAD == src_h) & (dst_lane % FOX_PAD == FOX_HD + src_n)).astype(BF16)

    col = jnp.arange(S5_KW)
    src = jnp.arange(S5_BLK)
    rep_th = ((src[:, None] // S5_GROUP == col[None, :] // S5_BLK)
              & (src[:, None] % S5_GROUP == col[None, :] % S5_GROUP)).astype(BF16)
    rep_rp = ((src[:, None] // S5_STATE == col[None, :] // S5_BST)
              & (src[:, None] % S5_STATE == col[None, :] % S5_STATE)).astype(BF16)

    s5_w, s5_bst, s5_cout, s5_pw = jax.vmap(_s5_params)(s5_a_re, s5_a_im, s5_b_re, s5_b_im,
                                                        s5_c_re, s5_c_im, s5_log_dt)

    xf = x.astype(F32).reshape(n, D_MODEL)
    for l in range(depth):
        gq = _pad_head_vec(fox_qnorm[l]) * (FOX_HD ** -0.5 * LOG2E)
        gk = _pad_head_vec(fox_knorm[l])
        fb = jnp.pad(fox_fbias[l].astype(F32), (0, 128 - FOX_HEADS)).reshape(1, 128)
        qk_bound = 1.01 * FOX_HD ** 0.5 * jnp.max(jnp.abs(fox_qnorm[l])) * jnp.max(jnp.abs(fox_knorm[l]))
        qk_bound2 = qk_bound * LOG2E
        qoff = jnp.full((1, 128), qk_bound2 - FOX_REF_MARGIN, F32)
        q, k, v, c = _inproj_fox(xf, row(norm_mix[l]), w_fox, l, gq, gk, qone, kone, hsum,
                                 fb, qoff, place, seq, tm)
        su, hq, hf, hi, hg, gl = _inproj_rest(xf, row(norm_mix[l]), w_rest, l, tm)

        jstart = _fox_first_blocks(c, qk_bound, bsz, seq, tq)
        bounded = (qk_bound2 <= FOX_BOUNDED_MAX).astype(jnp.int32).reshape(1)
        y_fox = _fox_attention(jstart, bounded, q, k, v, bsz, seq, tq)

        y_s5 = _s5(su, s5_w[l], s5_bst[l], s5_cout[l], rep_th, rep_rp, s5_pw[l], row(s5_d[l]), s5_wglu[l],
                   row(s5_b_glu[l]), bsz, seq, s5_tt)

        lb = lb_all[l].reshape(1, -1)
        y_hg = _hgrn(hq, hf, hi, hg, jnp.log(lb), jnp.log1p(-lb), 1.0 - lb, row(hg_onorm[l]), bsz, seq, tt)

        xf = _merge(xf, y_fox, y_s5, y_hg, gl, wb, wo, l, tm)

        km, vm = _memkv(mem.astype(F32), row(norm_mem[l]), wk[l], wv[l], row(x_knorm[l]))
        xf = _xattn(xf, row(norm_x[l]), wq, row(x_qnorm[l]), km, vm, wxo, l, seq, tm)
        xf = _ffn(xf, row(norm_ffn[l]), wgu, wd, l, tm)
    return xf.reshape(bsz, seq, D_MODEL).astype(x.dtype)
```

```python
import functools
import math

import jax
import jax.numpy as jnp
from jax import lax
from jax.experimental import pallas as pl
from jax.experimental.pallas import tpu as pltpu

F32 = jnp.float32
BF16 = jnp.bfloat16
HIGHEST = lax.Precision.HIGHEST

D_MODEL = 1024
BRANCH_W = 512
FOX_HD = 64
FOX_HEADS = 8
S5_GROUP = 16
S5_GROUPS = 32
S5_STATE = 64
S5_NSTATE = S5_GROUPS * S5_STATE
HG_HEADS = 4
HG_D = 128
HG_CHUNK = 64
HG_SUB = 8
X_HEADS = 4
X_HD = 256
D_FF = 2816
EPS = 1e-6

VMEM_LIMIT_BYTES = 56 * 1024 * 1024

FOX_PAD = 128
FOX_QK_W = FOX_HEADS * FOX_PAD
A_FQ, A_FK, A_FV, A_FF, A_END = 0, 1024, 2048, 2560, 2688
B_SU, B_HQ, B_HF, B_HI, B_HG, B_GL, B_END = 0, 512, 1024, 1536, 2048, 2560, 5632
FOX_BIAS_LANES = 3
LOG2E = 1.4426950408889634
FOX_SKIP_NATS = 30.0
FOX_UNROLL = 2
FOX_REF_MARGIN = 100.0
FOX_BOUNDED_MAX = 100.0


NEG_BIG = -1e30


def _cparams(sem, vmem_limit_bytes=VMEM_LIMIT_BYTES):
    return pltpu.CompilerParams(dimension_semantics=sem, vmem_limit_bytes=vmem_limit_bytes)


def _rms(xf, g):
    return xf * lax.rsqrt(jnp.mean(xf * xf, axis=-1, keepdims=True) + EPS) * g


def _sigmoid(x):
    return 1.0 / (1.0 + jnp.exp(-x))


def _log_sigmoid(x):
    return jnp.minimum(x, 0.0) - jnp.log(1.0 + jnp.exp(-jnp.abs(x)))


def _dot(a, b):
    return jnp.dot(a, b, preferred_element_type=F32)


def _dot_nt(a, b):
    return lax.dot_general(a, b, (((1,), (1,)), ((), ())), preferred_element_type=F32)


def _dot_tn(a, b):
    return lax.dot_general(a, b, (((0,), (0,)), ((), ())), preferred_element_type=F32)


def _const_spec(shape, single=False, layer=None):
    nd = len(shape)
    if layer is not None:
        return pl.BlockSpec((None,) + tuple(shape), lambda *_: (layer,) + (0,) * nd,
                            pipeline_mode=pl.Buffered(1))
    if single:
        return pl.BlockSpec(shape, lambda *_: (0,) * nd, pipeline_mode=pl.Buffered(1))
    return pl.BlockSpec(shape, lambda *_: (0,) * nd)


def _top16(v):
    bits = lax.bitcast_convert_type(v, jnp.uint32) & jnp.uint32(0xFFFF0000)
    return lax.bitcast_convert_type(bits, F32)


def _split3(v):
    hi = _top16(v)
    r1 = v - hi
    mid = _top16(r1)
    return [hi, mid, r1 - mid]


CUM_ROWS = 256


def _inproj_fox_kernel(x_ref, g_ref, w_ref, gq_ref, gk_ref, qone_ref, kone_ref, hsum_ref, fb_ref, qoff_ref,
                       place_ref, q_ref, k_ref, v_ref, c_ref, carry_ref, *, tiles_per_seq):
    tm = x_ref.shape[0]

    @pl.when(pl.program_id(0) % tiles_per_seq == 0)
    def _():
        carry_ref[...] = jnp.zeros(carry_ref.shape, F32)

    h = _rms(x_ref[...], g_ref[...]).astype(BF16)

    def proj(lo, hi):
        return _dot(h, w_ref[:, lo:hi])

    def sumsq(t):
        return _dot((t * t).astype(BF16), hsum_ref[...])

    def headnorm(t, ss, g):
        return t * lax.rsqrt(ss * (1.0 / FOX_HD) + EPS) * g

    lf = _log_sigmoid(proj(A_FF, A_END) + fb_ref[...])
    r = lax.broadcasted_iota(jnp.int32, (CUM_ROWS, CUM_ROWS), 0)
    cc = lax.broadcasted_iota(jnp.int32, (CUM_ROWS, CUM_ROWS), 1)
    lower = (cc <= r).astype(F32)
    carry = carry_ref[...]
    parts = []
    for r0 in range(0, tm, CUM_ROWS):
        part = jnp.dot(lower, lf[r0:r0 + CUM_ROWS], precision=HIGHEST, preferred_element_type=F32) + carry
        carry = part[CUM_ROWS - 1:CUM_ROWS, :]
        parts.append(part)
    carry_ref[...] = carry
    cs = jnp.concatenate(parts, axis=0)
    c_ref[...] = cs
    d = -(cs * LOG2E)
    lane = lax.broadcasted_iota(jnp.int32, (tm, 128), 1)
    packed = None
    for n, piece in enumerate(_split3(d) + _split3(-d - qoff_ref[...])):
        term = jnp.where(lane < FOX_HEADS, piece, 0.0)
        term = term if n == 0 else pltpu.roll(term, FOX_HEADS * n, 1)
        packed = term if packed is None else packed + term
    bias = _dot(packed.astype(BF16), place_ref[...])
    qbias, kbias = bias[:, :FOX_QK_W], bias[:, FOX_QK_W:]

    chunks = range(0, FOX_QK_W, 256)
    tq = [proj(A_FQ + c, A_FQ + c + 256) for c in chunks]
    tk = [proj(A_FK + c, A_FK + c + 256) for c in chunks]
    sq = [sumsq(t) for t in tq]
    sk = [sumsq(t) for t in tk]
    for n, c in enumerate(chunks):
        sl = slice(c, c + 256)
        q_ref[:, sl] = (headnorm(tq[n], sq[n], gq_ref[:, sl]) + qone_ref[:, sl] + qbias[:, sl]).astype(BF16)
        k_ref[:, sl] = (headnorm(tk[n], sk[n], gk_ref[:, sl]) + kone_ref[:, sl] + kbias[:, sl]).astype(BF16)
    v_ref[...] = proj(A_FV, A_FF).astype(BF16)


def _inproj_fox(x, g, w, layer, gq, gk, qone, kone, hsum, fb, qoff, place, seq, tm):
    n = x.shape[0]
    row = lambda c: pl.BlockSpec((tm, c), lambda i: (i, 0))
    outs = [(FOX_QK_W, BF16)] * 2 + [(512, BF16), (128, F32)]
    vec = _const_spec((1, FOX_QK_W))
    return pl.pallas_call(
        functools.partial(_inproj_fox_kernel, tiles_per_seq=seq // tm),
        grid=(n // tm,),
        in_specs=[row(D_MODEL), _const_spec((1, D_MODEL)), _const_spec((D_MODEL, A_END), layer=layer),
                  vec, vec, vec, vec, _const_spec((256, 256)), _const_spec((1, 128)), _const_spec((1, 128)),
                  _const_spec((128, 2 * FOX_QK_W))],
        out_specs=[row(c) for c, _ in outs],
        out_shape=[jax.ShapeDtypeStruct((n, c), dt) for c, dt in outs],
        scratch_shapes=[pltpu.VMEM((1, 128), F32)],
        compiler_params=_cparams(("arbitrary",)),
        name="inproj_fox",
    )(x, g, w, gq, gk, qone, kone, hsum, fb, qoff, place)


def _inproj_rest_kernel(x_ref, g_ref, w_ref, su_ref, hq_ref, hf_ref, hi_ref, hg_ref, gl_ref):
    h = _rms(x_ref[...], g_ref[...]).astype(BF16)

    def proj(lo, hi):
        return _dot(h, w_ref[:, lo:hi])

    su_ref[...] = proj(B_SU, B_HQ).astype(BF16)
    hq_ref[...] = proj(B_HQ, B_HF).astype(BF16)
    hf_ref[...] = proj(B_HF, B_HI)
    hi_ref[...] = proj(B_HI, B_HG).astype(BF16)
    hg_ref[...] = proj(B_HG, B_GL).astype(BF16)
    for c in range(B_GL, B_END, 512):
        gl_ref[:, c - B_GL:c - B_GL + 512] = proj(c, c + 512).astype(BF16)


def _inproj_rest(x, g, w, layer, tm):
    n = x.shape[0]
    row = lambda c: pl.BlockSpec((tm, c), lambda i: (i, 0))
    outs = [(512, BF16)] * 2 + [(512, F32)] + [(512, BF16)] * 2 + [(3072, BF16)]
    return pl.pallas_call(
        _inproj_rest_kernel,
        grid=(n // tm,),
        in_specs=[row(D_MODEL), _const_spec((1, D_MODEL)), _const_spec((D_MODEL, B_END), layer=layer)],
        out_specs=[row(c) for c, _ in outs],
        out_shape=[jax.ShapeDtypeStruct((n, c), dt) for c, dt in outs],
        compiler_params=_cparams(("parallel",)),
        name="inproj_rest",
    )(x, g, w)


def _fox_kernel(jstart_ref, bounded_ref, q_ref, k_ref, v_ref, o_ref, m_ref, l_ref, acc_ref, *, tq, nq):
    b, p, i = pl.program_id(0), pl.program_id(1), pl.program_id(2)
    nchunk = tq // 128
    m_ref[...] = jnp.full(m_ref.shape, NEG_BIG, F32)
    l_ref[...] = jnp.zeros(l_ref.shape, F32)
    acc_ref[...] = jnp.zeros(acc_ref.shape, F32)

    def logits(j, hh, masked):
        start = pl.multiple_of(j * tq, tq)
        sl = slice(hh * FOX_PAD, (hh + 1) * FOX_PAD)
        s = _dot_nt(q_ref[:, sl], k_ref[pl.ds(start, tq), sl])
        if masked:
            rq = lax.broadcasted_iota(jnp.int32, (tq, tq), 0)
            ck = lax.broadcasted_iota(jnp.int32, (tq, tq), 1)
            s = jnp.where(ck <= rq, s, NEG_BIG)
        return s, v_ref[pl.ds(start, tq), :]

    def lane_partial_sum(pr):
        ls = pr[:, 0:128]
        for c in range(1, nchunk):
            ls = ls + pr[:, c * 128:(c + 1) * 128]
        return ls

    def step_bounded(j, masked, heads=(0, 1)):
        for hh in heads:
            s, vb = logits(j, hh, masked)
            pr = jnp.exp2(s)
            l_ref[hh] += lane_partial_sum(pr)
            acc_ref[hh] += _dot(pr.astype(BF16), vb)

    def step_online(j, masked, heads=(0, 1)):
        for hh in heads:
            s, vb = logits(j, hh, masked)
            mc = s[:, 0:128]
            for c in range(1, nchunk):
                mc = jnp.maximum(mc, s[:, c * 128:(c + 1) * 128])
            m_prev = m_ref[hh]
            m_new = jnp.maximum(m_prev, jnp.max(mc, axis=-1, keepdims=True))
            alpha = jnp.exp2(m_prev - m_new)
            pr = jnp.exp2(s - jnp.concatenate([m_new] * nchunk, axis=1))
            l_ref[hh] = alpha * l_ref[hh] + lane_partial_sum(pr)
            acc_ref[hh] = alpha * acc_ref[hh] + _dot(pr.astype(BF16), vb)
            m_ref[hh] = m_new

    base = ((b * (FOX_HEADS // 2) + p) * 2) * nq + i
    first = (jstart_ref[base], jstart_ref[base + nq])
    j0 = jnp.maximum(first[0], first[1])
    count = i - j0

    def sweep(step):
        for hh in range(2):
            def only(n, carry, hh=hh):
                step(first[hh] + n, False, (hh,))
                return carry

            lax.fori_loop(0, j0 - first[hh], only, 0)

        def body(n, carry):
            for u in range(FOX_UNROLL):
                step(j0 + FOX_UNROLL * n + u, False)
            return carry

        def single(n, carry):
            step(i - 1 - n, False)
            return carry

        lax.fori_loop(0, count // FOX_UNROLL, body, 0)
        lax.fori_loop(0, count % FOX_UNROLL, single, 0)
        step(i, True)

    bounded = bounded_ref[0] == 1

    @pl.when(bounded)
    def _():
        sweep(step_bounded)

    @pl.when(jnp.logical_not(bounded))
    def _():
        sweep(step_online)

    lane = lax.broadcasted_iota(jnp.int32, (tq, 128), 1)
    o0 = acc_ref[0] / jnp.sum(l_ref[0], axis=-1, keepdims=True)
    o1 = acc_ref[1] / jnp.sum(l_ref[1], axis=-1, keepdims=True)
    o_ref[...] = jnp.where(lane < FOX_HD, o0, o1).astype(BF16)


def _fox_attention(jstart, bounded, q, k, v, bsz, seq, tq):
    n = q.shape[0]
    nq = seq // tq
    grid_spec = pltpu.PrefetchScalarGridSpec(
        num_scalar_prefetch=2,
        grid=(bsz, FOX_HEADS // 2, nq),
        in_specs=[pl.BlockSpec((tq, 2 * FOX_PAD), lambda b, p, i, js, bd: (b * nq + i, p)),
                  pl.BlockSpec((seq, 2 * FOX_PAD), lambda b, p, i, js, bd: (b, p)),
                  pl.BlockSpec((seq, 128), lambda b, p, i, js, bd: (b, p))],
        out_specs=pl.BlockSpec((tq, 128), lambda b, p, i, js, bd: (b * nq + i, p)),
        scratch_shapes=[pltpu.VMEM((2, tq, 128), F32), pltpu.VMEM((2, tq, 128), F32),
                        pltpu.VMEM((2, tq, 128), F32)],
    )
    return pl.pallas_call(
        functools.partial(_fox_kernel, tq=tq, nq=nq),
        grid_spec=grid_spec,
        out_shape=jax.ShapeDtypeStruct((n, 512), BF16),
        compiler_params=_cparams(("parallel", "parallel", "arbitrary")),
        name="fox_attention",
    )(jstart, bounded, q, k, v)


def _fox_first_blocks(c, qk_bound, bsz, seq, tq):
    nq = seq // tq
    cb = c.reshape(bsz, nq, tq, 128)
    c_end = cb[:, :, tq - 1, :FOX_HEADS].transpose(0, 2, 1)
    c_start = cb[:, :, 0, :FOX_HEADS].transpose(0, 2, 1)
    gap = c_end[:, :, None, :] - c_start[:, :, :, None]
    skip = gap > (2.0 * qk_bound + FOX_SKIP_NATS)
    jstart = jnp.sum(skip.astype(jnp.int32), axis=-1)
    jstart = jnp.minimum(jstart, jnp.arange(nq, dtype=jnp.int32)[None, None, :])
    return jstart.reshape(-1)


S5_T = 8
S5_BLK = 128
S5_NBLK = BRANCH_W // S5_BLK
S5_KW = S5_T * S5_BLK
S5_BST = S5_NSTATE // S5_NBLK
S5_VMEM_LIMIT_BYTES = 58 * 1024 * 1024


def _s5_expand(src_ref, rep_ref, dst_ref, row_div, col_div):
    n = S5_KW
    for j in range(S5_NBLK):
        for c in range(0, n, 256):
            rg = (lax.broadcasted_iota(jnp.int32, (n, 256), 0) // row_div) % 8
            cg = ((lax.broadcasted_iota(jnp.int32, (n, 256), 1) + c) // col_div) % 8
            rep = _dot(src_ref[j], rep_ref[:, c:c + 256])
            dst_ref[j, :, c:c + 256] = jnp.where(rg == cg, rep, 0.0).astype(BF16)


def _s5_kernel(u_ref, wc_ref, bc_ref, cc_ref, rep_th_ref, rep_rp_ref, pw_ref, d_ref, wglu_ref, bglu_ref,
               o_ref, w_ref, bst_ref, cout_ref, uf_ref, y_ref, hr_ref, hi_ref, cr_ref, ci_ref, *, tt):
    t = pl.program_id(1)
    rows = tt // S5_T

    @pl.when(t == 0)
    def _():
        cr_ref[...] = jnp.zeros(cr_ref.shape, F32)
        ci_ref[...] = jnp.zeros(ci_ref.shape, F32)
        _s5_expand(wc_ref, rep_th_ref, w_ref, S5_GROUP, S5_GROUP)
        _s5_expand(bc_ref, rep_rp_ref, bst_ref, S5_GROUP, S5_STATE)
        _s5_expand(cc_ref, rep_th_ref, cout_ref, S5_STATE, S5_GROUP)

    for j in range(S5_NBLK):
        uf_ref[j] = u_ref[:, j * S5_BLK:(j + 1) * S5_BLK].astype(F32)

    def chunk_rows(j):
        return jnp.concatenate([uf_ref[j, pl.ds(s, rows, stride=S5_T), :] for s in range(S5_T)],
                               axis=1).astype(BF16)

    for j in range(S5_NBLK):
        hin = _dot(chunk_rows(j), bst_ref[j])
        hr_ref[:, j * S5_BST:(j + 1) * S5_BST] = hin[:, :S5_BST]
        hi_ref[:, j * S5_BST:(j + 1) * S5_BST] = hin[:, S5_BST:]

    row8 = lax.broadcasted_iota(jnp.int32, (8, S5_NSTATE), 0)

    def scan_block(r, carry):
        cr, ci = carry
        rws = pl.ds(pl.multiple_of(r * 8, 8), 8)
        xr = hr_ref[rws, :]
        xi = hi_ref[rws, :]
        for n, k in enumerate((1, 2, 4)):
            ar = pw_ref[2 * n]
            ai = pw_ref[2 * n + 1]
            sr = pltpu.roll(xr, k, 0)
            si = pltpu.roll(xi, k, 0)
            xr, xi = xr + ar * sr - ai * si, xi + ar * si + ai * sr
        pr = pw_ref[6]
        pi = pw_ref[7]
        xr, xi = xr + pr * cr - pi * ci, xi + pr * ci + pi * cr
        hr_ref[rws, :] = jnp.where(row8 == 0, cr, pltpu.roll(xr, 1, 0))
        hi_ref[rws, :] = jnp.where(row8 == 0, ci, pltpu.roll(xi, 1, 0))
        return (jnp.broadcast_to(xr[7:8, :], xr.shape), jnp.broadcast_to(xi[7:8, :], xi.shape))

    cr, ci = lax.fori_loop(0, rows // 8, scan_block, (cr_ref[...], ci_ref[...]))
    cr_ref[...] = cr
    ci_ref[...] = ci

    for j in range(S5_NBLK):
        st = slice(j * S5_BST, (j + 1) * S5_BST)
        hb = jnp.concatenate([hr_ref[:, st], hi_ref[:, st]], axis=1).astype(BF16)
        yj = _dot(chunk_rows(j), w_ref[j]) + _dot(hb, cout_ref[j])
        for s in range(S5_T):
            y_ref[j, pl.ds(s, rows, stride=S5_T), :] = yj[:, s * S5_BLK:(s + 1) * S5_BLK]

    y = jnp.concatenate([y_ref[j] + d_ref[:, j * S5_BLK:(j + 1) * S5_BLK] * uf_ref[j]
                         for j in range(S5_NBLK)], axis=1)
    z = 0.5 * y * (1.0 + jnp.tanh(math.sqrt(2.0 / math.pi) * (y + 0.044715 * (y * y * y))))
    gate = _sigmoid(_dot(z.astype(BF16), wglu_ref[...]) + bglu_ref[...])
    o_ref[...] = (z * gate).astype(BF16)


def _s5(u, wc, bc, cc, rep_th, rep_rp, pw, d, wglu, bglu, bsz, seq, tt):
    n = u.shape[0]
    nt = seq // tt
    rowspec = pl.BlockSpec((tt, BRANCH_W), lambda b, t: (b * nt + t, 0))
    compact = _const_spec((S5_NBLK, S5_KW, S5_BLK), single=True)
    expanded = pltpu.VMEM((S5_NBLK, S5_KW, S5_KW), BF16)
    return pl.pallas_call(
        functools.partial(_s5_kernel, tt=tt),
        grid=(bsz, nt),
        in_specs=[rowspec, compact, compact, compact,
                  _const_spec((S5_BLK, S5_KW), single=True), _const_spec((S5_BLK, S5_KW), single=True),
                  _const_spec((8, 8, S5_NSTATE), single=True), _const_spec((1, BRANCH_W)),
                  _const_spec((BRANCH_W, BRANCH_W)), _const_spec((1, BRANCH_W))],
        out_specs=rowspec,
        out_shape=jax.ShapeDtypeStruct((n, BRANCH_W), BF16),
        scratch_shapes=[expanded, expanded, expanded,
                        pltpu.VMEM((S5_NBLK, tt, S5_BLK), F32), pltpu.VMEM((S5_NBLK, tt, S5_BLK), F32),
                        pltpu.VMEM((tt // S5_T, S5_NSTATE), F32), pltpu.VMEM((tt // S5_T, S5_NSTATE), F32),
                        pltpu.VMEM((8, S5_NSTATE), F32), pltpu.VMEM((8, S5_NSTATE), F32)],
        compiler_params=_cparams(("parallel", "arbitrary"), S5_VMEM_LIMIT_BYTES),
        name="s5_mixer",
    )(u, wc, bc, cc, rep_th, rep_rp, pw, d, wglu, bglu)


def _s5_params(a_re, a_im, b_re, b_im, c_re, c_im, log_dt):
    g, p, gs, tt = S5_GROUPS, S5_STATE, S5_GROUP, S5_T
    dt = jnp.exp(log_dt.astype(F32))[:, None]
    ar, ai = a_re.astype(F32), a_im.astype(F32)
    mag = jnp.exp(dt * ar)
    abar_r, abar_i = mag * jnp.cos(dt * ai), mag * jnp.sin(dt * ai)
    inv_den = 1.0 / (ar * ar + ai * ai)
    nr, ni = abar_r - 1.0, abar_i
    coef_r = (nr * ar + ni * ai) * inv_den
    coef_i = (ni * ar - nr * ai) * inv_den
    br, bi = b_re.astype(F32), b_im.astype(F32)
    bbar_r = coef_r[..., None] * br - coef_i[..., None] * bi
    bbar_i = coef_r[..., None] * bi + coef_i[..., None] * br
    cr, ci = c_re.astype(F32), c_im.astype(F32)

    pows = [(jnp.ones_like(abar_r), jnp.zeros_like(abar_i))]
    for _ in range(tt):
        pr, pi = pows[-1]
        pows.append((pr * abar_r - pi * abar_i, pr * abar_i + pi * abar_r))
    pw_r = jnp.stack([x[0] for x in pows])
    pw_i = jnp.stack([x[1] for x in pows])

    bt_r, bt_i = bbar_r.transpose(0, 2, 1), bbar_i.transpose(0, 2, 1)
    cb_r = cr[:, :, None, :] * bt_r[:, None, :, :] - ci[:, :, None, :] * bt_i[:, None, :, :]
    cb_i = cr[:, :, None, :] * bt_i[:, None, :, :] + ci[:, :, None, :] * bt_r[:, None, :, :]
    kl = (jnp.einsum('lgp,ghkp->lghk', pw_r[:tt], cb_r, precision=HIGHEST)
          - jnp.einsum('lgp,ghkp->lghk', pw_i[:tt], cb_i, precision=HIGHEST))
    nb, gl = S5_NBLK, g // S5_NBLK

    lag = jnp.arange(tt)[None, :] - jnp.arange(tt)[:, None]
    m = jnp.where((lag >= 0)[:, :, None, None, None], kl[jnp.clip(lag, 0, tt - 1)], 0.0)
    m = m.transpose(2, 0, 4, 1, 3).reshape(nb, gl, tt, gs, tt * gs)
    w = m.transpose(0, 2, 1, 3, 4).reshape(nb, S5_KW, S5_BLK)

    e_r, e_i = pw_r[:tt][::-1], pw_i[:tt][::-1]
    bs_r = e_r[:, :, None, :] * bt_r[None] - e_i[:, :, None, :] * bt_i[None]
    bs_i = e_r[:, :, None, :] * bt_i[None] + e_i[:, :, None, :] * bt_r[None]
    bs = jnp.stack([bs_r, bs_i]).transpose(2, 1, 3, 0, 4).reshape(nb, gl, tt, gs, 2 * p)
    bst = bs.transpose(0, 2, 1, 3, 4).reshape(nb, S5_KW, S5_BLK)

    q_r, q_i = pw_r[1:tt + 1], pw_i[1:tt + 1]
    co_r = cr[None] * q_r[:, :, None, :] - ci[None] * q_i[:, :, None, :]
    co_i = cr[None] * q_i[:, :, None, :] + ci[None] * q_r[:, :, None, :]
    co = jnp.stack([co_r, -co_i]).transpose(2, 0, 4, 1, 3).reshape(nb, gl, 2, p, tt * gs)
    cout = co.transpose(0, 2, 1, 3, 4).reshape(nb, S5_KW, S5_BLK)

    ar1, ai1 = pw_r[tt].reshape(-1), pw_i[tt].reshape(-1)
    apow = [(ar1, ai1)]
    for _ in range(7):
        pr, pi = apow[-1]
        apow.append((pr * ar1 - pi * ai1, pr * ai1 + pi * ar1))
    rows8 = jnp.arange(8)[:, None]
    tabs = []
    for k in (1, 2, 4):
        mask = (rows8 >= k).astype(F32)
        tabs += [mask * apow[k - 1][0][None, :], mask * apow[k - 1][1][None, :]]
    tabs.append(jnp.stack([apow[r][0] for r in range(8)]))
    tabs.append(jnp.stack([apow[r][1] for r in range(8)]))
    pw = jnp.stack(tabs)
    return w.astype(BF16), bst.astype(BF16), cout.astype(BF16), pw


def _hgrn_kernel(q_ref, f_ref, i_ref, g_ref, loglb_ref, log1mlb_ref, onemlb_ref, gain_ref,
                 o_ref, st_ref, b_ref, key_ref, qf_ref, oacc_ref, *, tt):
    t = pl.program_id(1)

    @pl.when(t == 0)
    def _():
        st_ref[...] = jnp.zeros(st_ref.shape, F32)

    c_sz, sub = HG_CHUNK, HG_SUB
    n_sub = c_sz // sub
    row_s = lax.broadcasted_iota(jnp.int32, (sub, 1), 0)
    lane_s = lax.broadcasted_iota(jnp.int32, (sub, c_sz), 1)

    z = f_ref[...]
    loglb = loglb_ref[...]
    bb = log1mlb_ref[...] + _log_sigmoid(z)
    logf = jnp.maximum(loglb, bb) + jnp.log(1.0 + jnp.exp(-jnp.abs(loglb - bb)))
    key_ref[...] = onemlb_ref[...] * (1.0 / (1.0 + jnp.exp(z)))
    qx = q_ref[...].astype(F32)
    qf_ref[...] = qx * _sigmoid(qx)
    rr = lax.broadcasted_iota(jnp.int32, (tt, tt), 0)
    cc = lax.broadcasted_iota(jnp.int32, (tt, tt), 1)
    lower = ((cc <= rr) & (rr // c_sz == cc // c_sz)).astype(F32)
    b_ref[...] = jnp.dot(lower, logf * LOG2E, precision=HIGHEST, preferred_element_type=F32)

    def head_chunk(rows, hd):
        sl = slice(hd * HG_D, (hd + 1) * HG_D)
        key = key_ref[rows, sl]
        qf = qf_ref[rows, sl]
        vb = i_ref[rows, sl]
        b = b_ref[rows, sl]
        b_last = b[c_sz - 1:c_sz, :]
        st = st_ref[hd]
        o_inter = _dot_nt((qf * jnp.exp2(b)).astype(BF16), st.astype(BF16))
        kd = (key * jnp.exp2(b_last - b)).astype(BF16)
        st_ref[hd] = st * jnp.exp2(b_last) + _dot_tn(vb, kd)

        srows = []
        for blk in range(n_sub):
            lo = blk * sub
            b_i = b[lo:lo + sub]
            q_i = qf[lo:lo + sub]
            k_i = key[lo:lo + sub]
            sd = jnp.zeros((sub, c_sz), F32)
            for s in range(sub):
                e = jnp.exp2(b_i - b_i[s:s + 1])
                col = jnp.sum(q_i * e * k_i[s:s + 1], axis=-1, keepdims=True)
                sd = jnp.where(lane_s == lo + s, col, sd)
            sd = jnp.where(lane_s <= lo + row_s, sd, 0.0)
            if blk > 0:
                ref = b[lo - 1:lo]
                qt = (q_i * jnp.exp2(b_i - ref)).astype(BF16)
                kt = jnp.concatenate([key[:lo] * jnp.exp2(ref - b[:lo]), jnp.zeros((c_sz - lo, HG_D), F32)],
                                     axis=0).astype(BF16)
                sd = jnp.where(lane_s < lo, _dot_nt(qt, kt), sd)
            srows.append(sd)
        scores = jnp.concatenate(srows, axis=0)
        oacc_ref[rows, sl] = o_inter + _dot(scores.astype(BF16), vb)

    def chunk(c, carry):
        rows = pl.ds(pl.multiple_of(c * c_sz, c_sz), c_sz)
        for hd in range(HG_HEADS):
            head_chunk(rows, hd)
        return carry

    lax.fori_loop(0, tt // c_sz, chunk, 0)

    for hd in range(HG_HEADS):
        sl = slice(hd * HG_D, (hd + 1) * HG_D)
        o = oacc_ref[:, sl]
        gx = g_ref[:, sl].astype(F32)
        y = o * lax.rsqrt(jnp.mean(o * o, axis=-1, keepdims=True) + EPS) * gain_ref[...]
        o_ref[:, sl] = (y * (gx * _sigmoid(gx))).astype(BF16)


def _hgrn(hq, hf, hi, hg, loglb, log1mlb, onemlb, gain, bsz, seq, tt):
    n = hq.shape[0]
    nt = seq // tt
    spec = pl.BlockSpec((tt, 512), lambda b, t: (b * nt + t, 0))
    return pl.pallas_call(
        functools.partial(_hgrn_kernel, tt=tt),
        grid=(bsz, nt),
        in_specs=[spec, spec, spec, spec, _const_spec((1, 512)), _const_spec((1, 512)),
                  _const_spec((1, 512)), _const_spec((1, HG_D))],
        out_specs=spec,
        out_shape=jax.ShapeDtypeStruct((n, 512), BF16),
        scratch_shapes=[pltpu.VMEM((HG_HEADS, HG_D, HG_D), F32)] + [pltpu.VMEM((tt, 512), F32)] * 4,
        compiler_params=_cparams(("parallel", "arbitrary")),
        name="hgrn2_mixer",
    )(hq, hf, hi, hg, loglb, log1mlb, onemlb, gain)


def _merge_kernel(x_ref, yf_ref, ys_ref, yh_ref, gl_ref, wb_ref, wo_ref, o_ref):
    m = None
    for n, y_ref in enumerate((yf_ref, ys_ref, yh_ref)):
        gate = _sigmoid(gl_ref[:, n * D_MODEL:(n + 1) * D_MODEL].astype(F32))
        term = gate * _dot(y_ref[...], wb_ref[n * BRANCH_W:(n + 1) * BRANCH_W, :])
        m = term if m is None else m + term
    o_ref[...] = x_ref[...] + _dot(m.astype(BF16), wo_ref[...])


def _merge(x, yf, ys, yh, gl, wb, wo, layer, tm):
    n = x.shape[0]
    row = lambda c: pl.BlockSpec((tm, c), lambda i: (i, 0))
    return pl.pallas_call(
        _merge_kernel,
        grid=(n // tm,),
        in_specs=[row(D_MODEL), row(512), row(512), row(512), row(3 * D_MODEL),
                  _const_spec((3 * BRANCH_W, D_MODEL), layer=layer),
                  _const_spec((D_MODEL, D_MODEL), layer=layer)],
        out_specs=row(D_MODEL),
        out_shape=jax.ShapeDtypeStruct((n, D_MODEL), F32),
        compiler_params=_cparams(("parallel",)),
        name="merge_outproj",
    )(x, yf, ys, yh, gl, wb, wo)


def _memkv_kernel(m_ref, g_ref, wk_ref, wv_ref, gk_ref, k_ref, v_ref):
    h = _rms(m_ref[0], g_ref[...]).astype(BF16)
    kk = _dot(h, wk_ref[...])
    for hd in range(X_HEADS):
        sl = slice(hd * X_HD, (hd + 1) * X_HD)
        k_ref[0, :, sl] = _rms(kk[:, sl], gk_ref[...]).astype(BF16)
    v_ref[0] = _dot(h, wv_ref[...]).astype(BF16)


def _memkv(mem, g, wk, wv, gk):
    bsz, nm, _ = mem.shape
    spec = pl.BlockSpec((1, nm, D_MODEL), lambda b: (b, 0, 0))
    return pl.pallas_call(
        _memkv_kernel,
        grid=(bsz,),
        in_specs=[spec, _const_spec((1, D_MODEL)), _const_spec((D_MODEL, D_MODEL)),
                  _const_spec((D_MODEL, D_MODEL)), _const_spec((1, X_HD))],
        out_specs=[spec, spec],
        out_shape=[jax.ShapeDtypeStruct(mem.shape, BF16)] * 2,
        compiler_params=_cparams(("parallel",)),
        name="mem_kv",
    )(mem, g, wk, wv, gk)


def _xattn_kernel(x_ref, g_ref, wq_ref, gq_ref, k_ref, v_ref, wo_ref, o_ref):
    x = x_ref[...]
    h = _rms(x, g_ref[...]).astype(BF16)
    q = _dot(h, wq_ref[...])
    outs = []
    for hd in range(X_HEADS):
        sl = slice(hd * X_HD, (hd + 1) * X_HD)
        qh = (_rms(q[:, sl], gq_ref[...]) * (X_HD ** -0.5)).astype(BF16)
        s = _dot_nt(qh, k_ref[0, :, sl])
        p = jnp.exp(s - jnp.max(s, axis=-1, keepdims=True))
        l = jnp.sum(p, axis=-1, keepdims=True)
        outs.append((_dot(p.astype(BF16), v_ref[0, :, sl]) / l).astype(BF16))
    o_ref[...] = x + _dot(jnp.concatenate(outs, axis=1), wo_ref[...])


def _xattn(x, g, wq, gq, km, vm, wo, layer, seq, tm):
    n = x.shape[0]
    nm = km.shape[1]
    per_b = seq // tm
    row = pl.BlockSpec((tm, D_MODEL), lambda i: (i, 0))
    kv = pl.BlockSpec((1, nm, D_MODEL), lambda i: (i // per_b, 0, 0))
    return pl.pallas_call(
        _xattn_kernel,
        grid=(n // tm,),
        in_specs=[row, _const_spec((1, D_MODEL)), _const_spec((D_MODEL, D_MODEL), layer=layer),
                  _const_spec((1, X_HD)), kv, kv, _const_spec((D_MODEL, D_MODEL), layer=layer)],
        out_specs=row,
        out_shape=jax.ShapeDtypeStruct((n, D_MODEL), F32),
        compiler_params=_cparams(("parallel",)),
        name="cross_attention",
    )(x, g, wq, gq, km, vm, wo)


FF_CHUNK = 256


def _ffn_kernel(x_ref, g_ref, wgu_ref, wd_ref, o_ref, act_ref):
    x = x_ref[...]
    h = _rms(x, g_ref[...]).astype(BF16)
    for c in range(0, D_FF, FF_CHUNK):
        a = _dot(h, wgu_ref[:, c:c + FF_CHUNK])
        b = _dot(h, wgu_ref[:, D_FF + c:D_FF + c + FF_CHUNK])
        act_ref[:, c:c + FF_CHUNK] = (a * _sigmoid(a) * b).astype(BF16)
    o_ref[...] = x + _dot(act_ref[...], wd_ref[...])


def _ffn(x, g, wgu, wd, layer, tm):
    n = x.shape[0]
    row = pl.BlockSpec((tm, D_MODEL), lambda i: (i, 0))
    return pl.pallas_call(
        _ffn_kernel,
        grid=(n // tm,),
        in_specs=[row, _const_spec((1, D_MODEL)), _const_spec((D_MODEL, 2 * D_FF), layer=layer),
                  _const_spec((D_FF, D_MODEL), layer=layer)],
        out_specs=row,
        out_shape=jax.ShapeDtypeStruct((n, D_MODEL), F32),
        scratch_shapes=[pltpu.VMEM((tm, D_FF), BF16)],
        compiler_params=_cparams(("parallel",)),
        name="swiglu",
    )(x, g, wgu, wd)


def _split_w_in(w_in):
    depth = w_in.shape[0]

    def pad_heads(w):
        w = w.reshape(depth, D_MODEL, FOX_HEADS, FOX_HD)
        w = jnp.pad(w, ((0, 0), (0, 0), (0, 0), (0, FOX_PAD - FOX_HD)))
        return w.reshape(depth, D_MODEL, FOX_QK_W)

    fq, fk, fv = w_in[:, :, 0:512], w_in[:, :, 512:1024], w_in[:, :, 1024:1536]
    ff = w_in[:, :, 1536:1544]
    pad = jnp.zeros((depth, D_MODEL, A_END - A_FF - FOX_HEADS), w_in.dtype)
    w_fox = jnp.concatenate([pad_heads(fq), pad_heads(fk), fv, ff, pad], axis=-1).astype(BF16)
    return w_fox, w_in[:, :, 1544:].astype(BF16)


def _pad_head_vec(v, fill=0.0):
    v = jnp.concatenate([v.astype(F32), jnp.full((FOX_PAD - FOX_HD,), fill, F32)])
    return jnp.tile(v, FOX_HEADS).reshape(1, FOX_QK_W)


def kernel(x, mem, norm_mix, w_in, fox_fbias, fox_qnorm, fox_knorm, s5_a_re, s5_a_im, s5_b_re, s5_b_im, s5_c_re, s5_c_im, s5_d, s5_log_dt, s5_w_glu, s5_b_glu, hg_lb, hg_onorm, w_branch, w_out, norm_x, norm_mem, xq, xk, xv, xo, x_qnorm, x_knorm, norm_ffn, w_gate_up, w_down):
    bsz, seq, _ = x.shape
    depth = w_in.shape[0]
    n = bsz * seq
    tm = min(1024, seq)
    tq = min(512, seq)
    tt = min(256, seq)
    s5_tt = min(2048, seq)
    assert seq % tm == 0 and seq % tq == 0 and seq % tt == 0 and tt % HG_CHUNK == 0
    assert seq % s5_tt == 0 and s5_tt % (8 * S5_T) == 0

    row = lambda v: v.astype(F32).reshape(1, -1)
    w_fox, w_rest = _split_w_in(w_in)
    s5_wglu = s5_w_glu.astype(BF16)
    wb, wo = w_branch.astype(BF16), w_out.astype(BF16)
    wq, wk, wv, wxo = xq.astype(BF16), xk.astype(BF16), xv.astype(BF16), xo.astype(BF16)
    wgu, wd = w_gate_up.astype(BF16), w_down.astype(BF16)

    lb_all = jnp.cumsum(jax.nn.softmax(hg_lb.astype(F32), axis=0), axis=0)
    lb_all = lb_all - lb_all[0:1]

    lane = jnp.arange(256)
    hsum = ((lane[:, None] // FOX_PAD == lane[None, :] // FOX_PAD)
            & (lane[:, None] % FOX_PAD < FOX_HD)).astype(BF16)
    pad_lane = jnp.tile(jnp.arange(FOX_PAD), FOX_HEADS)
    k_lanes = (pad_lane >= FOX_HD) & (pad_lane < FOX_HD + FOX_BIAS_LANES)
    q_lanes = (pad_lane >= FOX_HD + FOX_BIAS_LANES) & (pad_lane < FOX_HD + 2 * FOX_BIAS_LANES)
    qone = k_lanes.astype(F32).reshape(1, FOX_QK_W)
    kone = q_lanes.astype(F32).reshape(1, FOX_QK_W)
    src = jnp.arange(128)
    dst = jnp.arange(2 * FOX_QK_W)
    src_n, src_h = src[:, None] // FOX_HEADS, src[:, None] % FOX_HEADS
    dst_key, dst_lane = dst[None, :] // FOX_QK_W, dst[None, :] % FOX_QK_W
    place = ((src_n < 2 * FOX_BIAS_LANES) & (dst_key == (src_n < FOX_BIAS_LANES))
             & (dst_lane // FOX_PAD == src_h) & (dst_lane % FOX_PAD == FOX_HD + src_n)).astype(BF16)

    col = jnp.arange(S5_KW)
    src = jnp.arange(S5_BLK)
    rep_th = ((src[:, None] // S5_GROUP == col[None, :] // S5_BLK)
              & (src[:, None] % S5_GROUP == col[None, :] % S5_GROUP)).astype(BF16)
    rep_rp = ((src[:, None] // S5_STATE == col[None, :] // S5_BST)
              & (src[:, None] % S5_STATE == col[None, :] % S5_STATE)).astype(BF16)

    s5_w, s5_bst, s5_cout, s5_pw = jax.vmap(_s5_params)(s5_a_re, s5_a_im, s5_b_re, s5_b_im,
                                                        s5_c_re, s5_c_im, s5_log_dt)

    xf = x.astype(F32).reshape(n, D_MODEL)
    for l in range(depth):
        gq = _pad_head_vec(fox_qnorm[l]) * (FOX_HD ** -0.5 * LOG2E)
        gk = _pad_head_vec(fox_knorm[l])
        fb = jnp.pad(fox_fbias[l].astype(F32), (0, 128 - FOX_HEADS)).reshape(1, 128)
        qk_bound = 1.01 * FOX_HD ** 0.5 * jnp.max(jnp.abs(fox_qnorm[l])) * jnp.max(jnp.abs(fox_knorm[l]))
        qk_bound2 = qk_bound * LOG2E
        qoff = jnp.full((1, 128), qk_bound2 - FOX_REF_MARGIN, F32)
        q, k, v, c = _inproj_fox(xf, row(norm_mix[l]), w_fox, l, gq, gk, qone, kone, hsum,
                                 fb, qoff, place, seq, tm)
        su, hq, hf, hi, hg, gl = _inproj_rest(xf, row(norm_mix[l]), w_rest, l, tm)

        jstart = _fox_first_blocks(c, qk_bound, bsz, seq, tq)
        bounded = (qk_bound2 <= FOX_BOUNDED_MAX).astype(jnp.int32).reshape(1)
        y_fox = _fox_attention(jstart, bounded, q, k, v, bsz, seq, tq)

        y_s5 = _s5(su, s5_w[l], s5_bst[l], s5_cout[l], rep_th, rep_rp, s5_pw[l], row(s5_d[l]), s5_wglu[l],
                   row(s5_b_glu[l]), bsz, seq, s5_tt)

        lb = lb_all[l].reshape(1, -1)
        y_hg = _hgrn(hq, hf, hi, hg, jnp.log(lb), jnp.log1p(-lb), 1.0 - lb, row(hg_onorm[l]), bsz, seq, tt)

        xf = _merge(xf, y_fox, y_s5, y_hg, gl, wb, wo, l, tm)

        km, vm = _memkv(mem.astype(F32), row(norm_mem[l]), wk[l], wv[l], row(x_knorm[l]))
        xf = _xattn(xf, row(norm_x[l]), wq, row(x_qnorm[l]), km, vm, wxo, l, seq, tm)
        xf = _ffn(xf, row(norm_ffn[l]), wgu, wd, l, tm)
    return xf.reshape(bsz, seq, D_MODEL).astype(x.dtype)
```

```python
import functools
import math

import jax
import jax.numpy as jnp
from jax import lax
from jax.experimental import pallas as pl
from jax.experimental.pallas import tpu as pltpu

F32 = jnp.float32
BF16 = jnp.bfloat16
HIGHEST = lax.Precision.HIGHEST

D_MODEL = 1024
BRANCH_W = 512
FOX_HD = 64
FOX_HEADS = 8
S5_GROUP = 16
S5_GROUPS = 32
S5_STATE = 64
S5_NSTATE = S5_GROUPS * S5_STATE
HG_HEADS = 4
HG_D = 128
HG_CHUNK = 64
HG_SUB = 8
X_HEADS = 4
X_HD = 256
D_FF = 2816
EPS = 1e-6

VMEM_LIMIT_BYTES = 56 * 1024 * 1024

FOX_PAD = 128
FOX_QK_W = FOX_HEADS * FOX_PAD
A_FQ, A_FK, A_FV, A_FF, A_END = 0, 512, 1024, 1536, 1664
B_SU, B_HQ, B_HF, B_HI, B_HG, B_GL, B_END = 0, 512, 1024, 1536, 2048, 2560, 5632
FOX_BIAS_LANES = 3
LOG2E = 1.4426950408889634
FOX_SKIP_NATS = 30.0
FOX_UNROLL = 2
FOX_REF_MARGIN = 100.0
FOX_BOUNDED_MAX = 100.0


NEG_BIG = -1e30


def _cparams(sem, vmem_limit_bytes=VMEM_LIMIT_BYTES):
    return pltpu.CompilerParams(dimension_semantics=sem, vmem_limit_bytes=vmem_limit_bytes)


def _rms(xf, g):
    return xf * lax.rsqrt(jnp.mean(xf * xf, axis=-1, keepdims=True) + EPS) * g


def _sigmoid(x):
    return 0.5 + 0.5 * jnp.tanh(0.5 * x)


def _log_sigmoid(x):
    return jnp.minimum(x, 0.0) - jnp.log(1.0 + jnp.exp(-jnp.abs(x)))


def _dot(a, b):
    return jnp.dot(a, b, preferred_element_type=F32)


def _dot_nt(a, b):
    return lax.dot_general(a, b, (((1,), (1,)), ((), ())), preferred_element_type=F32)


def _dot_tn(a, b):
    return lax.dot_general(a, b, (((0,), (0,)), ((), ())), preferred_element_type=F32)


def _const_spec(shape, single=False, layer=None):
    nd = len(shape)
    if layer is not None:
        return pl.BlockSpec((None,) + tuple(shape), lambda *_: (layer,) + (0,) * nd,
                            pipeline_mode=pl.Buffered(1))
    if single:
        return pl.BlockSpec(shape, lambda *_: (0,) * nd, pipeline_mode=pl.Buffered(1))
    return pl.BlockSpec(shape, lambda *_: (0,) * nd)


def _top16(v):
    bits = lax.bitcast_convert_type(v, jnp.uint32) & jnp.uint32(0xFFFF0000)
    return lax.bitcast_convert_type(bits, F32)


def _split3(v):
    hi = _top16(v)
    r1 = v - hi
    mid = _top16(r1)
    return [hi, mid, r1 - mid]


CUM_ROWS = 256


def _inproj_fox_kernel(x_ref, g_ref, w_ref, gq_ref, gk_ref, qone_ref, kone_ref, hsum_ref, fb_ref, qoff_ref,
                       place_ref, q_ref, k_ref, v_ref, c_ref, carry_ref, *, tiles_per_seq):
    tm = x_ref.shape[0]

    @pl.when(pl.program_id(0) % tiles_per_seq == 0)
    def _():
        carry_ref[...] = jnp.zeros(carry_ref.shape, F32)

    h = _rms(x_ref[...], g_ref[...]).astype(BF16)

    def proj(lo, hi):
        return _dot(h, w_ref[:, lo:hi])

    def sumsq(t):
        return _dot((t * t).astype(BF16), hsum_ref[...])

    def headnorm(t, ss, g):
        return t * lax.rsqrt(ss * (1.0 / FOX_HD) + EPS) * g

    lf = _log_sigmoid(proj(A_FF, A_END) + fb_ref[...])
    r = lax.broadcasted_iota(jnp.int32, (CUM_ROWS, CUM_ROWS), 0)
    cc = lax.broadcasted_iota(jnp.int32, (CUM_ROWS, CUM_ROWS), 1)
    lower = (cc <= r).astype(F32)
    carry = carry_ref[...]
    parts = []
    for r0 in range(0, tm, CUM_ROWS):
        part = jnp.dot(lower, lf[r0:r0 + CUM_ROWS], precision=HIGHEST, preferred_element_type=F32) + carry
        carry = part[CUM_ROWS - 1:CUM_ROWS, :]
        parts.append(part)
    carry_ref[...] = carry
    cs = jnp.concatenate(parts, axis=0)
    c_ref[...] = cs
    d = -(cs * LOG2E)
    lane = lax.broadcasted_iota(jnp.int32, (tm, 128), 1)
    packed = None
    for n, piece in enumerate(_split3(d) + _split3(-d - qoff_ref[...])):
        term = jnp.where(lane < FOX_HEADS, piece, 0.0)
        term = term if n == 0 else pltpu.roll(term, FOX_HEADS * n, 1)
        packed = term if packed is None else packed + term
    bias = _dot(packed.astype(BF16), place_ref[...])
    qbias, kbias = bias[:, :FOX_QK_W], bias[:, FOX_QK_W:]

    chunks = range(0, BRANCH_W, 256)
    tq = [proj(A_FQ + c, A_FQ + c + 256) for c in chunks]
    tk = [proj(A_FK + c, A_FK + c + 256) for c in chunks]
    sq = [sumsq(t) for t in tq]
    sk = [sumsq(t) for t in tk]

    def store_padded(dst_ref, t, extra_ref, bias_vals, blk):
        for odd in range(2):
            head = 2 * blk + odd
            sl = slice(head * FOX_PAD, (head + 1) * FOX_PAD)
            src = pltpu.roll(t, FOX_HD, 1) if odd else t
            dst_ref[:, sl] = (jnp.where(lane < FOX_HD, src, 0.0) + extra_ref[:, sl] + bias_vals[:, sl]).astype(BF16)

    for n, c in enumerate(chunks):
        qn = headnorm(tq[n], sq[n], gq_ref[:, c:c + 256])
        kn = headnorm(tk[n], sk[n], gk_ref[:, c:c + 256])
        for half in range(2):
            blk = 2 * n + half
            store_padded(q_ref, qn[:, half * 128:(half + 1) * 128], qone_ref, qbias, blk)
            store_padded(k_ref, kn[:, half * 128:(half + 1) * 128], kone_ref, kbias, blk)
    v_ref[...] = proj(A_FV, A_FF).astype(BF16)


def _inproj_fox(x, g, w, layer, gq, gk, qone, kone, hsum, fb, qoff, place, seq, tm):
    n = x.shape[0]
    row = lambda c: pl.BlockSpec((tm, c), lambda i: (i, 0))
    outs = [(FOX_QK_W, BF16)] * 2 + [(512, BF16), (128, F32)]
    vec = _const_spec((1, FOX_QK_W))
    gain = _const_spec((1, BRANCH_W))
    return pl.pallas_call(
        functools.partial(_inproj_fox_kernel, tiles_per_seq=seq // tm),
        grid=(n // tm,),
        in_specs=[row(D_MODEL), _const_spec((1, D_MODEL)), _const_spec((D_MODEL, A_END), layer=layer),
                  gain, gain, vec, vec, _const_spec((256, 256)), _const_spec((1, 128)), _const_spec((1, 128)),
                  _const_spec((128, 2 * FOX_QK_W))],
        out_specs=[row(c) for c, _ in outs],
        out_shape=[jax.ShapeDtypeStruct((n, c), dt) for c, dt in outs],
        scratch_shapes=[pltpu.VMEM((1, 128), F32)],
        compiler_params=_cparams(("arbitrary",)),
        name="inproj_fox",
    )(x, g, w, gq, gk, qone, kone, hsum, fb, qoff, place)


def _inproj_rest_kernel(x_ref, g_ref, w_ref, su_ref, hq_ref, hf_ref, hi_ref, hg_ref, gl_ref):
    h = _rms(x_ref[...], g_ref[...]).astype(BF16)

    def proj(lo, hi):
        return _dot(h, w_ref[:, lo:hi])

    su_ref[...] = proj(B_SU, B_HQ).astype(BF16)
    hq_ref[...] = proj(B_HQ, B_HF).astype(BF16)
    hf_ref[...] = proj(B_HF, B_HI)
    hi_ref[...] = proj(B_HI, B_HG).astype(BF16)
    hg_ref[...] = proj(B_HG, B_GL).astype(BF16)
    for c in range(B_GL, B_END, 512):
        gl_ref[:, c - B_GL:c - B_GL + 512] = proj(c, c + 512).astype(BF16)


def _inproj_rest(x, g, w, layer, tm):
    n = x.shape[0]
    row = lambda c: pl.BlockSpec((tm, c), lambda i: (i, 0))
    outs = [(512, BF16)] * 2 + [(512, F32)] + [(512, BF16)] * 2 + [(3072, BF16)]
    return pl.pallas_call(
        _inproj_rest_kernel,
        grid=(n // tm,),
        in_specs=[row(D_MODEL), _const_spec((1, D_MODEL)), _const_spec((D_MODEL, B_END), layer=layer)],
        out_specs=[row(c) for c, _ in outs],
        out_shape=[jax.ShapeDtypeStruct((n, c), dt) for c, dt in outs],
        compiler_params=_cparams(("parallel",)),
        name="inproj_rest",
    )(x, g, w)


def _fox_kernel(jstart_ref, bounded_ref, q_ref, k_ref, v_ref, o_ref, m_ref, l_ref, acc_ref, *, tq, nq):
    b, p, i = pl.program_id(0), pl.program_id(1), pl.program_id(2)
    nchunk = tq // 128
    m_ref[...] = jnp.full(m_ref.shape, NEG_BIG, F32)
    l_ref[...] = jnp.zeros(l_ref.shape, F32)
    acc_ref[...] = jnp.zeros(acc_ref.shape, F32)

    def logits(j, hh, masked):
        start = pl.multiple_of(j * tq, tq)
        sl = slice(hh * FOX_PAD, (hh + 1) * FOX_PAD)
        s = _dot_nt(q_ref[:, sl], k_ref[pl.ds(start, tq), sl])
        if masked:
            rq = lax.broadcasted_iota(jnp.int32, (tq, tq), 0)
            ck = lax.broadcasted_iota(jnp.int32, (tq, tq), 1)
            s = jnp.where(ck <= rq, s, NEG_BIG)
        return s, v_ref[pl.ds(start, tq), :]

    def lane_partial_sum(pr):
        ls = pr[:, 0:128]
        for c in range(1, nchunk):
            ls = ls + pr[:, c * 128:(c + 1) * 128]
        return ls

    def step_bounded(j, masked, heads=(0, 1)):
        for hh in heads:
            s, vb = logits(j, hh, masked)
            pr = jnp.exp2(s)
            l_ref[hh] += lane_partial_sum(pr)
            acc_ref[hh] += _dot(pr.astype(BF16), vb)

    def step_online(j, masked, heads=(0, 1)):
        for hh in heads:
            s, vb = logits(j, hh, masked)
            mc = s[:, 0:128]
            for c in range(1, nchunk):
                mc = jnp.maximum(mc, s[:, c * 128:(c + 1) * 128])
            m_prev = m_ref[hh]
            m_new = jnp.maximum(m_prev, jnp.max(mc, axis=-1, keepdims=True))
            alpha = jnp.exp2(m_prev - m_new)
            pr = jnp.exp2(s - jnp.concatenate([m_new] * nchunk, axis=1))
            l_ref[hh] = alpha * l_ref[hh] + lane_partial_sum(pr)
            acc_ref[hh] = alpha * acc_ref[hh] + _dot(pr.astype(BF16), vb)
            m_ref[hh] = m_new

    base = ((b * (FOX_HEADS // 2) + p) * 2) * nq + i
    first = (jstart_ref[base], jstart_ref[base + nq])
    j0 = jnp.maximum(first[0], first[1])
    count = i - j0

    def sweep(step):
        for hh in range(2):
            def only(n, carry, hh=hh):
                step(first[hh] + n, False, (hh,))
                return carry

            lax.fori_loop(0, j0 - first[hh], only, 0)

        def body(n, carry):
            for u in range(FOX_UNROLL):
                step(j0 + FOX_UNROLL * n + u, False)
            return carry

        def single(n, carry):
            step(i - 1 - n, False)
            return carry

        lax.fori_loop(0, count // FOX_UNROLL, body, 0)
        lax.fori_loop(0, count % FOX_UNROLL, single, 0)
        step(i, True)

    bounded = bounded_ref[0] == 1

    @pl.when(bounded)
    def _():
        sweep(step_bounded)

    @pl.when(jnp.logical_not(bounded))
    def _():
        sweep(step_online)

    lane = lax.broadcasted_iota(jnp.int32, (tq, 128), 1)
    o0 = acc_ref[0] / jnp.sum(l_ref[0], axis=-1, keepdims=True)
    o1 = acc_ref[1] / jnp.sum(l_ref[1], axis=-1, keepdims=True)
    o_ref[...] = jnp.where(lane < FOX_HD, o0, o1).astype(BF16)


def _fox_attention(jstart, bounded, q, k, v, bsz, seq, tq):
    n = q.shape[0]
    nq = seq // tq
    grid_spec = pltpu.PrefetchScalarGridSpec(
        num_scalar_prefetch=2,
        grid=(bsz, FOX_HEADS // 2, nq),
        in_specs=[pl.BlockSpec((tq, 2 * FOX_PAD), lambda b, p, i, js, bd: (b * nq + i, p)),
                  pl.BlockSpec((seq, 2 * FOX_PAD), lambda b, p, i, js, bd: (b, p)),
                  pl.BlockSpec((seq, 128), lambda b, p, i, js, bd: (b, p))],
        out_specs=pl.BlockSpec((tq, 128), lambda b, p, i, js, bd: (b * nq + i, p)),
        scratch_shapes=[pltpu.VMEM((2, tq, 128), F32), pltpu.VMEM((2, tq, 128), F32),
                        pltpu.VMEM((2, tq, 128), F32)],
    )
    return pl.pallas_call(
        functools.partial(_fox_kernel, tq=tq, nq=nq),
        grid_spec=grid_spec,
        out_shape=jax.ShapeDtypeStruct((n, 512), BF16),
        compiler_params=_cparams(("parallel", "parallel", "arbitrary")),
        name="fox_attention",
    )(jstart, bounded, q, k, v)


def _fox_first_blocks(c, qk_bound, bsz, seq, tq):
    nq = seq // tq
    cb = c.reshape(bsz, nq, tq, 128)
    c_end = cb[:, :, tq - 1, :FOX_HEADS].transpose(0, 2, 1)
    c_start = cb[:, :, 0, :FOX_HEADS].transpose(0, 2, 1)
    gap = c_end[:, :, None, :] - c_start[:, :, :, None]
    skip = gap > (2.0 * qk_bound + FOX_SKIP_NATS)
    jstart = jnp.sum(skip.astype(jnp.int32), axis=-1)
    jstart = jnp.minimum(jstart, jnp.arange(nq, dtype=jnp.int32)[None, None, :])
    return jstart.reshape(-1)


S5_T = 8
S5_BLK = 128
S5_NBLK = BRANCH_W // S5_BLK
S5_KW = S5_T * S5_BLK
S5_BST = S5_NSTATE // S5_NBLK
S5_VMEM_LIMIT_BYTES = 58 * 1024 * 1024


def _s5_expand(src_ref, rep_ref, dst_ref, row_div, col_div):
    n = S5_KW
    for j in range(S5_NBLK):
        for c in range(0, n, 256):
            rg = (lax.broadcasted_iota(jnp.int32, (n, 256), 0) // row_div) % 8
            cg = ((lax.broadcasted_iota(jnp.int32, (n, 256), 1) + c) // col_div) % 8
            rep = _dot(src_ref[j], rep_ref[:, c:c + 256])
            dst_ref[j, :, c:c + 256] = jnp.where(rg == cg, rep, 0.0).astype(BF16)


def _s5_kernel(u_ref, wc_ref, bc_ref, cc_ref, rep_th_ref, rep_rp_ref, pw_ref, d_ref, wglu_ref, bglu_ref,
               o_ref, w_ref, bst_ref, cout_ref, uf_ref, y_ref, hr_ref, hi_ref, cr_ref, ci_ref, *, tt):
    t = pl.program_id(1)
    rows = tt // S5_T

    @pl.when(t == 0)
    def _():
        cr_ref[...] = jnp.zeros(cr_ref.shape, F32)
        ci_ref[...] = jnp.zeros(ci_ref.shape, F32)
        _s5_expand(wc_ref, rep_th_ref, w_ref, S5_GROUP, S5_GROUP)
        _s5_expand(bc_ref, rep_rp_ref, bst_ref, S5_GROUP, S5_STATE)
        _s5_expand(cc_ref, rep_th_ref, cout_ref, S5_STATE, S5_GROUP)

    for j in range(S5_NBLK):
        uf_ref[j] = u_ref[:, j * S5_BLK:(j + 1) * S5_BLK].astype(F32)

    def chunk_rows(j):
        return jnp.concatenate([uf_ref[j, pl.ds(s, rows, stride=S5_T), :] for s in range(S5_T)],
                               axis=1).astype(BF16)

    for j in range(S5_NBLK):
        hin = _dot(chunk_rows(j), bst_ref[j])
        hr_ref[:, j * S5_BST:(j + 1) * S5_BST] = hin[:, :S5_BST]
        hi_ref[:, j * S5_BST:(j + 1) * S5_BST] = hin[:, S5_BST:]

    row8 = lax.broadcasted_iota(jnp.int32, (8, S5_NSTATE), 0)

    def scan_block(r, carry):
        cr, ci = carry
        rws = pl.ds(pl.multiple_of(r * 8, 8), 8)
        xr = hr_ref[rws, :]
        xi = hi_ref[rws, :]
        for n, k in enumerate((1, 2, 4)):
            ar = pw_ref[2 * n]
            ai = pw_ref[2 * n + 1]
            sr = pltpu.roll(xr, k, 0)
            si = pltpu.roll(xi, k, 0)
            xr, xi = xr + ar * sr - ai * si, xi + ar * si + ai * sr
        pr = pw_ref[6]
        pi = pw_ref[7]
        xr, xi = xr + pr * cr - pi * ci, xi + pr * ci + pi * cr
        hr_ref[rws, :] = jnp.where(row8 == 0, cr, pltpu.roll(xr, 1, 0))
        hi_ref[rws, :] = jnp.where(row8 == 0, ci, pltpu.roll(xi, 1, 0))
        return (jnp.broadcast_to(xr[7:8, :], xr.shape), jnp.broadcast_to(xi[7:8, :], xi.shape))

    cr, ci = lax.fori_loop(0, rows // 8, scan_block, (cr_ref[...], ci_ref[...]))
    cr_ref[...] = cr
    ci_ref[...] = ci

    for j in range(S5_NBLK):
        st = slice(j * S5_BST, (j + 1) * S5_BST)
        hb = jnp.concatenate([hr_ref[:, st], hi_ref[:, st]], axis=1).astype(BF16)
        yj = _dot(chunk_rows(j), w_ref[j]) + _dot(hb, cout_ref[j])
        for s in range(S5_T):
            y_ref[j, pl.ds(s, rows, stride=S5_T), :] = yj[:, s * S5_BLK:(s + 1) * S5_BLK]

    y = jnp.concatenate([y_ref[j] + d_ref[:, j * S5_BLK:(j + 1) * S5_BLK] * uf_ref[j]
                         for j in range(S5_NBLK)], axis=1)
    z = 0.5 * y * (1.0 + jnp.tanh(math.sqrt(2.0 / math.pi) * (y + 0.044715 * (y * y * y))))
    gate = _sigmoid(_dot(z.astype(BF16), wglu_ref[...]) + bglu_ref[...])
    o_ref[...] = (z * gate).astype(BF16)


def _s5(u, wc, bc, cc, rep_th, rep_rp, pw, d, wglu, bglu, bsz, seq, tt):
    n = u.shape[0]
    nt = seq // tt
    rowspec = pl.BlockSpec((tt, BRANCH_W), lambda b, t: (b * nt + t, 0))
    compact = _const_spec((S5_NBLK, S5_KW, S5_BLK), single=True)
    expanded = pltpu.VMEM((S5_NBLK, S5_KW, S5_KW), BF16)
    return pl.pallas_call(
        functools.partial(_s5_kernel, tt=tt),
        grid=(bsz, nt),
        in_specs=[rowspec, compact, compact, compact,
                  _const_spec((S5_BLK, S5_KW), single=True), _const_spec((S5_BLK, S5_KW), single=True),
                  _const_spec((8, 8, S5_NSTATE), single=True), _const_spec((1, BRANCH_W)),
                  _const_spec((BRANCH_W, BRANCH_W)), _const_spec((1, BRANCH_W))],
        out_specs=rowspec,
        out_shape=jax.ShapeDtypeStruct((n, BRANCH_W), BF16),
        scratch_shapes=[expanded, expanded, expanded,
                        pltpu.VMEM((S5_NBLK, tt, S5_BLK), F32), pltpu.VMEM((S5_NBLK, tt, S5_BLK), F32),
                        pltpu.VMEM((tt // S5_T, S5_NSTATE), F32), pltpu.VMEM((tt // S5_T, S5_NSTATE), F32),
                        pltpu.VMEM((8, S5_NSTATE), F32), pltpu.VMEM((8, S5_NSTATE), F32)],
        compiler_params=_cparams(("parallel", "arbitrary"), S5_VMEM_LIMIT_BYTES),
        name="s5_mixer",
    )(u, wc, bc, cc, rep_th, rep_rp, pw, d, wglu, bglu)


def _s5_params(a_re, a_im, b_re, b_im, c_re, c_im, log_dt):
    g, p, gs, tt = S5_GROUPS, S5_STATE, S5_GROUP, S5_T
    dt = jnp.exp(log_dt.astype(F32))[:, None]
    ar, ai = a_re.astype(F32), a_im.astype(F32)
    mag = jnp.exp(dt * ar)
    abar_r, abar_i = mag * jnp.cos(dt * ai), mag * jnp.sin(dt * ai)
    inv_den = 1.0 / (ar * ar + ai * ai)
    nr, ni = abar_r - 1.0, abar_i
    coef_r = (nr * ar + ni * ai) * inv_den
    coef_i = (ni * ar - nr * ai) * inv_den
    br, bi = b_re.astype(F32), b_im.astype(F32)
    bbar_r = coef_r[..., None] * br - coef_i[..., None] * bi
    bbar_i = coef_r[..., None] * bi + coef_i[..., None] * br
    cr, ci = c_re.astype(F32), c_im.astype(F32)

    pows = [(jnp.ones_like(abar_r), jnp.zeros_like(abar_i))]
    for _ in range(tt):
        pr, pi = pows[-1]
        pows.append((pr * abar_r - pi * abar_i, pr * abar_i + pi * abar_r))
    pw_r = jnp.stack([x[0] for x in pows])
    pw_i = jnp.stack([x[1] for x in pows])

    bt_r, bt_i = bbar_r.transpose(0, 2, 1), bbar_i.transpose(0, 2, 1)
    cb_r = cr[:, :, None, :] * bt_r[:, None, :, :] - ci[:, :, None, :] * bt_i[:, None, :, :]
    cb_i = cr[:, :, None, :] * bt_i[:, None, :, :] + ci[:, :, None, :] * bt_r[:, None, :, :]
    kl = jnp.sum(pw_r[:tt, :, None, None, :] * cb_r[None] - pw_i[:tt, :, None, None, :] * cb_i[None],
                 axis=-1)
    nb, gl = S5_NBLK, g // S5_NBLK

    lag = jnp.arange(tt)[None, :] - jnp.arange(tt)[:, None]
    m = jnp.where((lag >= 0)[:, :, None, None, None], kl[jnp.clip(lag, 0, tt - 1)], 0.0)
    m = m.transpose(2, 0, 4, 1, 3).reshape(nb, gl, tt, gs, tt * gs)
    w = m.transpose(0, 2, 1, 3, 4).reshape(nb, S5_KW, S5_BLK)

    e_r, e_i = pw_r[:tt][::-1], pw_i[:tt][::-1]
    bs_r = e_r[:, :, None, :] * bt_r[None] - e_i[:, :, None, :] * bt_i[None]
    bs_i = e_r[:, :, None, :] * bt_i[None] + e_i[:, :, None, :] * bt_r[None]
    bs = jnp.stack([bs_r, bs_i]).transpose(2, 1, 3, 0, 4).reshape(nb, gl, tt, gs, 2 * p)
    bst = bs.transpose(0, 2, 1, 3, 4).reshape(nb, S5_KW, S5_BLK)

    q_r, q_i = pw_r[1:tt + 1], pw_i[1:tt + 1]
    co_r = cr[None] * q_r[:, :, None, :] - ci[None] * q_i[:, :, None, :]
    co_i = cr[None] * q_i[:, :, None, :] + ci[None] * q_r[:, :, None, :]
    co = jnp.stack([co_r, -co_i]).transpose(2, 0, 4, 1, 3).reshape(nb, gl, 2, p, tt * gs)
    cout = co.transpose(0, 2, 1, 3, 4).reshape(nb, S5_KW, S5_BLK)

    ar1, ai1 = pw_r[tt].reshape(-1), pw_i[tt].reshape(-1)
    apow = [(ar1, ai1)]
    for _ in range(7):
        pr, pi = apow[-1]
        apow.append((pr * ar1 - pi * ai1, pr * ai1 + pi * ar1))
    rows8 = jnp.arange(8)[:, None]
    tabs = []
    for k in (1, 2, 4):
        mask = (rows8 >= k).astype(F32)
        tabs += [mask * apow[k - 1][0][None, :], mask * apow[k - 1][1][None, :]]
    tabs.append(jnp.stack([apow[r][0] for r in range(8)]))
    tabs.append(jnp.stack([apow[r][1] for r in range(8)]))
    pw = jnp.stack(tabs)
    return w.astype(BF16), bst.astype(BF16), cout.astype(BF16), pw


def _hgrn_kernel(q_ref, f_ref, i_ref, g_ref, loglb_ref, log1mlb_ref, onemlb_ref, gain_ref,
                 o_ref, st_ref, b_ref, key_ref, qf_ref, oacc_ref, *, tt):
    t = pl.program_id(1)

    @pl.when(t == 0)
    def _():
        st_ref[...] = jnp.zeros(st_ref.shape, F32)

    c_sz, sub = HG_CHUNK, HG_SUB
    n_sub = c_sz // sub
    row_s = lax.broadcasted_iota(jnp.int32, (sub, 1), 0)
    lane_s = lax.broadcasted_iota(jnp.int32, (sub, c_sz), 1)

    z = f_ref[...]
    loglb = loglb_ref[...]
    bb = log1mlb_ref[...] + _log_sigmoid(z)
    logf = jnp.maximum(loglb, bb) + jnp.log(1.0 + jnp.exp(-jnp.abs(loglb - bb)))
    key_ref[...] = onemlb_ref[...] * (1.0 / (1.0 + jnp.exp(z)))
    qx = q_ref[...].astype(F32)
    qf_ref[...] = qx * _sigmoid(qx)
    rr = lax.broadcasted_iota(jnp.int32, (tt, tt), 0)
    cc = lax.broadcasted_iota(jnp.int32, (tt, tt), 1)
    lower = ((cc <= rr) & (rr // c_sz == cc // c_sz)).astype(F32)
    b_ref[...] = jnp.dot(lower, logf * LOG2E, precision=HIGHEST, preferred_element_type=F32)

    def head_chunk(rows, hd):
        sl = slice(hd * HG_D, (hd + 1) * HG_D)
        key = key_ref[rows, sl]
        qf = qf_ref[rows, sl]
        vb = i_ref[rows, sl]
        b = b_ref[rows, sl]
        b_last = b[c_sz - 1:c_sz, :]
        st = st_ref[hd]
        o_inter = _dot_nt((qf * jnp.exp2(b)).astype(BF16), st.astype(BF16))
        kd = (key * jnp.exp2(b_last - b)).astype(BF16)
        st_ref[hd] = st * jnp.exp2(b_last) + _dot_tn(vb, kd)

        srows = []
        for blk in range(n_sub):
            lo = blk * sub
            b_i = b[lo:lo + sub]
            q_i = qf[lo:lo + sub]
            k_i = key[lo:lo + sub]
            sd = jnp.zeros((sub, c_sz), F32)
            for s in range(sub):
                e = jnp.exp2(b_i - b_i[s:s + 1])
                col = jnp.sum(q_i * e * k_i[s:s + 1], axis=-1, keepdims=True)
                sd = jnp.where(lane_s == lo + s, col, sd)
            sd = jnp.where(lane_s <= lo + row_s, sd, 0.0)
            if blk > 0:
                ref = b[lo - 1:lo]
                qt = (q_i * jnp.exp2(b_i - ref)).astype(BF16)
                kt = jnp.concatenate([key[:lo] * jnp.exp2(ref - b[:lo]), jnp.zeros((c_sz - lo, HG_D), F32)],
                                     axis=0).astype(BF16)
                sd = jnp.where(lane_s < lo, _dot_nt(qt, kt), sd)
            srows.append(sd)
        scores = jnp.concatenate(srows, axis=0)
        oacc_ref[rows, sl] = o_inter + _dot(scores.astype(BF16), vb)

    def chunk(c, carry):
        rows = pl.ds(pl.multiple_of(c * c_sz, c_sz), c_sz)
        for hd in range(HG_HEADS):
            head_chunk(rows, hd)
        return carry

    lax.fori_loop(0, tt // c_sz, chunk, 0)

    for hd in range(HG_HEADS):
        sl = slice(hd * HG_D, (hd + 1) * HG_D)
        o = oacc_ref[:, sl]
        gx = g_ref[:, sl].astype(F32)
        y = o * lax.rsqrt(jnp.mean(o * o, axis=-1, keepdims=True) + EPS) * gain_ref[...]
        o_ref[:, sl] = (y * (gx * _sigmoid(gx))).astype(BF16)


def _hgrn(hq, hf, hi, hg, loglb, log1mlb, onemlb, gain, bsz, seq, tt):
    n = hq.shape[0]
    nt = seq // tt
    spec = pl.BlockSpec((tt, 512), lambda b, t: (b * nt + t, 0))
    return pl.pallas_call(
        functools.partial(_hgrn_kernel, tt=tt),
        grid=(bsz, nt),
        in_specs=[spec, spec, spec, spec, _const_spec((1, 512)), _const_spec((1, 512)),
                  _const_spec((1, 512)), _const_spec((1, HG_D))],
        out_specs=spec,
        out_shape=jax.ShapeDtypeStruct((n, 512), BF16),
        scratch_shapes=[pltpu.VMEM((HG_HEADS, HG_D, HG_D), F32)] + [pltpu.VMEM((tt, 512), F32)] * 4,
        compiler_params=_cparams(("parallel", "arbitrary")),
        name="hgrn2_mixer",
    )(hq, hf, hi, hg, loglb, log1mlb, onemlb, gain)


def _merge_kernel(x_ref, yf_ref, ys_ref, yh_ref, gl_ref, wb_ref, wo_ref, o_ref):
    m = None
    for n, y_ref in enumerate((yf_ref, ys_ref, yh_ref)):
        gate = _sigmoid(gl_ref[:, n * D_MODEL:(n + 1) * D_MODEL].astype(F32))
        term = gate * _dot(y_ref[...], wb_ref[n * BRANCH_W:(n + 1) * BRANCH_W, :])
        m = term if m is None else m + term
    o_ref[...] = x_ref[...] + _dot(m.astype(BF16), wo_ref[...])


def _merge(x, yf, ys, yh, gl, wb, wo, layer, tm):
    n = x.shape[0]
    row = lambda c: pl.BlockSpec((tm, c), lambda i: (i, 0))
    return pl.pallas_call(
        _merge_kernel,
        grid=(n // tm,),
        in_specs=[row(D_MODEL), row(512), row(512), row(512), row(3 * D_MODEL),
                  _const_spec((3 * BRANCH_W, D_MODEL), layer=layer),
                  _const_spec((D_MODEL, D_MODEL), layer=layer)],
        out_specs=row(D_MODEL),
        out_shape=jax.ShapeDtypeStruct((n, D_MODEL), F32),
        compiler_params=_cparams(("parallel",)),
        name="merge_outproj",
    )(x, yf, ys, yh, gl, wb, wo)


def _memkv_kernel(m_ref, g_ref, wk_ref, wv_ref, gk_ref, k_ref, v_ref):
    h = _rms(m_ref[0], g_ref[...]).astype(BF16)
    kk = _dot(h, wk_ref[...])
    for hd in range(X_HEADS):
        sl = slice(hd * X_HD, (hd + 1) * X_HD)
        k_ref[0, :, sl] = _rms(kk[:, sl], gk_ref[...]).astype(BF16)
    v_ref[0] = _dot(h, wv_ref[...]).astype(BF16)


def _memkv(mem, g, wk, wv, gk):
    bsz, nm, _ = mem.shape
    spec = pl.BlockSpec((1, nm, D_MODEL), lambda b: (b, 0, 0))
    return pl.pallas_call(
        _memkv_kernel,
        grid=(bsz,),
        in_specs=[spec, _const_spec((1, D_MODEL)), _const_spec((D_MODEL, D_MODEL)),
                  _const_spec((D_MODEL, D_MODEL)), _const_spec((1, X_HD))],
        out_specs=[spec, spec],
        out_shape=[jax.ShapeDtypeStruct(mem.shape, BF16)] * 2,
        compiler_params=_cparams(("parallel",)),
        name="mem_kv",
    )(mem, g, wk, wv, gk)


def _xattn_kernel(x_ref, g_ref, wq_ref, gq_ref, k_ref, v_ref, wo_ref, o_ref):
    x = x_ref[...]
    h = _rms(x, g_ref[...]).astype(BF16)
    q = _dot(h, wq_ref[...])
    outs = []
    for hd in range(X_HEADS):
        sl = slice(hd * X_HD, (hd + 1) * X_HD)
        qh = (_rms(q[:, sl], gq_ref[...]) * (X_HD ** -0.5)).astype(BF16)
        s = _dot_nt(qh, k_ref[0, :, sl])
        p = jnp.exp(s - jnp.max(s, axis=-1, keepdims=True))
        l = jnp.sum(p, axis=-1, keepdims=True)
        outs.append((_dot(p.astype(BF16), v_ref[0, :, sl]) / l).astype(BF16))
    o_ref[...] = x + _dot(jnp.concatenate(outs, axis=1), wo_ref[...])


def _xattn(x, g, wq, gq, km, vm, wo, layer, seq, tm):
    n = x.shape[0]
    nm = km.shape[1]
    per_b = seq // tm
    row = pl.BlockSpec((tm, D_MODEL), lambda i: (i, 0))
    kv = pl.BlockSpec((1, nm, D_MODEL), lambda i: (i // per_b, 0, 0))
    return pl.pallas_call(
        _xattn_kernel,
        grid=(n // tm,),
        in_specs=[row, _const_spec((1, D_MODEL)), _const_spec((D_MODEL, D_MODEL), layer=layer),
                  _const_spec((1, X_HD)), kv, kv, _const_spec((D_MODEL, D_MODEL), layer=layer)],
        out_specs=row,
        out_shape=jax.ShapeDtypeStruct((n, D_MODEL), F32),
        compiler_params=_cparams(("parallel",)),
        name="cross_attention",
    )(x, g, wq, gq, km, vm, wo)


FF_CHUNK = 256


def _ffn_kernel(x_ref, g_ref, wgu_ref, wd_ref, o_ref, act_ref):
    x = x_ref[...]
    h = _rms(x, g_ref[...]).astype(BF16)
    for c in range(0, D_FF, FF_CHUNK):
        a = _dot(h, wgu_ref[:, c:c + FF_CHUNK])
        b = _dot(h, wgu_ref[:, D_FF + c:D_FF + c + FF_CHUNK])
        act_ref[:, c:c + FF_CHUNK] = (a * _sigmoid(a) * b).astype(BF16)
    o_ref[...] = x + _dot(act_ref[...], wd_ref[...])


def _ffn(x, g, wgu, wd, layer, tm):
    n = x.shape[0]
    row = pl.BlockSpec((tm, D_MODEL), lambda i: (i, 0))
    return pl.pallas_call(
        _ffn_kernel,
        grid=(n // tm,),
        in_specs=[row, _const_spec((1, D_MODEL)), _const_spec((D_MODEL, 2 * D_FF), layer=layer),
                  _const_spec((D_FF, D_MODEL), layer=layer)],
        out_specs=row,
        out_shape=jax.ShapeDtypeStruct((n, D_MODEL), F32),
        scratch_shapes=[pltpu.VMEM((tm, D_FF), BF16)],
        compiler_params=_cparams(("parallel",)),
        name="swiglu",
    )(x, g, wgu, wd)


def _split_w_in(w_in):
    depth = w_in.shape[0]
    pad = jnp.zeros((depth, D_MODEL, A_END - A_FF - FOX_HEADS), w_in.dtype)
    w_fox = jnp.concatenate([w_in[:, :, :1544], pad], axis=-1).astype(BF16)
    return w_fox, w_in[:, :, 1544:].astype(BF16)


def _head_vec(v):
    return jnp.tile(v.astype(F32), FOX_HEADS).reshape(1, BRANCH_W)


def kernel(x, mem, norm_mix, w_in, fox_fbias, fox_qnorm, fox_knorm, s5_a_re, s5_a_im, s5_b_re, s5_b_im, s5_c_re, s5_c_im, s5_d, s5_log_dt, s5_w_glu, s5_b_glu, hg_lb, hg_onorm, w_branch, w_out, norm_x, norm_mem, xq, xk, xv, xo, x_qnorm, x_knorm, norm_ffn, w_gate_up, w_down):
    bsz, seq, _ = x.shape
    depth = w_in.shape[0]
    n = bsz * seq
    tm = min(1024, seq)
    tq = min(512, seq)
    tt = min(256, seq)
    s5_tt = min(2048, seq)
    assert seq % tm == 0 and seq % tq == 0 and seq % tt == 0 and tt % HG_CHUNK == 0
    assert seq % s5_tt == 0 and s5_tt % (8 * S5_T) == 0

    row = lambda v: v.astype(F32).reshape(1, -1)
    w_fox, w_rest = _split_w_in(w_in)
    s5_wglu = s5_w_glu.astype(BF16)
    wb, wo = w_branch.astype(BF16), w_out.astype(BF16)
    wq, wk, wv, wxo = xq.astype(BF16), xk.astype(BF16), xv.astype(BF16), xo.astype(BF16)
    wgu, wd = w_gate_up.astype(BF16), w_down.astype(BF16)

    lb_all = jnp.cumsum(jax.nn.softmax(hg_lb.astype(F32), axis=0), axis=0)
    lb_all = lb_all - lb_all[0:1]

    lane = jnp.arange(256)
    hsum = (lane[:, None] // FOX_HD == lane[None, :] // FOX_HD).astype(BF16)
    pad_lane = jnp.tile(jnp.arange(FOX_PAD), FOX_HEADS)
    k_lanes = (pad_lane >= FOX_HD) & (pad_lane < FOX_HD + FOX_BIAS_LANES)
    q_lanes = (pad_lane >= FOX_HD + FOX_BIAS_LANES) & (pad_lane < FOX_HD + 2 * FOX_BIAS_LANES)
    qone = k_lanes.astype(F32).reshape(1, FOX_QK_W)
    kone = q_lanes.astype(F32).reshape(1, FOX_QK_W)
    src = jnp.arange(128)
    dst = jnp.arange(2 * FOX_QK_W)
    src_n, src_h = src[:, None] // FOX_HEADS, src[:, None] % FOX_HEADS
    dst_key, dst_lane = dst[None, :] // FOX_QK_W, dst[None, :] % FOX_QK_W
    place = ((src_n < 2 * FOX_BIAS_LANES) & (dst_key == (src_n < FOX_BIAS_LANES))
             & (dst_lane // FOX_PAD == src_h) & (dst_lane % FOX_PAD == FOX_HD + src_n)).astype(BF16)

    col = jnp.arange(S5_KW)
    src = jnp.arange(S5_BLK)
    rep_th = ((src[:, None] // S5_GROUP == col[None, :] // S5_BLK)
              & (src[:, None] % S5_GROUP == col[None, :] % S5_GROUP)).astype(BF16)
    rep_rp = ((src[:, None] // S5_STATE == col[None, :] // S5_BST)
              & (src[:, None] % S5_STATE == col[None, :] % S5_STATE)).astype(BF16)

    s5_w, s5_bst, s5_cout, s5_pw = jax.vmap(_s5_params)(s5_a_re, s5_a_im, s5_b_re, s5_b_im,
                                                        s5_c_re, s5_c_im, s5_log_dt)

    xf = x.astype(F32).reshape(n, D_MODEL)
    for l in range(depth):
        gq = _head_vec(fox_qnorm[l]) * (FOX_HD ** -0.5 * LOG2E)
        gk = _head_vec(fox_knorm[l])
        fb = jnp.pad(fox_fbias[l].astype(F32), (0, 128 - FOX_HEADS)).reshape(1, 128)
        qk_bound = 1.01 * FOX_HD ** 0.5 * jnp.max(jnp.abs(fox_qnorm[l])) * jnp.max(jnp.abs(fox_knorm[l]))
        qk_bound2 = qk_bound * LOG2E
        qoff = jnp.full((1, 128), qk_bound2 - FOX_REF_MARGIN, F32)
        q, k, v, c = _inproj_fox(xf, row(norm_mix[l]), w_fox, l, gq, gk, qone, kone, hsum,
                                 fb, qoff, place, seq, tm)
        su, hq, hf, hi, hg, gl = _inproj_rest(xf, row(norm_mix[l]), w_rest, l, tm)

        jstart = _fox_first_blocks(c, qk_bound, bsz, seq, tq)
        bounded = (qk_bound2 <= FOX_BOUNDED_MAX).astype(jnp.int32).reshape(1)
        y_fox = _fox_attention(jstart, bounded, q, k, v, bsz, seq, tq)

        y_s5 = _s5(su, s5_w[l], s5_bst[l], s5_cout[l], rep_th, rep_rp, s5_pw[l], row(s5_d[l]), s5_wglu[l],
                   row(s5_b_glu[l]), bsz, seq, s5_tt)

        lb = lb_all[l].reshape(1, -1)
        y_hg = _hgrn(hq, hf, hi, hg, jnp.log(lb), jnp.log1p(-lb), 1.0 - lb, row(hg_onorm[l]), bsz, seq, tt)

        xf = _merge(xf, y_fox, y_s5, y_hg, gl, wb, wo, l, tm)

        km, vm = _memkv(mem.astype(F32), row(norm_mem[l]), wk[l], wv[l], row(x_knorm[l]))
        xf = _xattn(xf, row(norm_x[l]), wq, row(x_qnorm[l]), km, vm, wxo, l, seq, tm)
        xf = _ffn(xf, row(norm_ffn[l]), wgu, wd, l, tm)
    return xf.reshape(bsz, seq, D_MODEL).astype(x.dtype)
```

```python
import functools
import math

import jax
import jax.numpy as jnp
from jax import lax
from jax.experimental import pallas as pl
from jax.experimental.pallas import tpu as pltpu

F32 = jnp.float32
BF16 = jnp.bfloat16
HIGHEST = lax.Precision.HIGHEST

D_MODEL = 1024
BRANCH_W = 512
FOX_HD = 64
FOX_HEADS = 8
S5_GROUP = 16
S5_GROUPS = 32
S5_STATE = 64
S5_NSTATE = S5_GROUPS * S5_STATE
HG_HEADS = 4
HG_D = 128
HG_CHUNK = 64
HG_SUB = 8
X_HEADS = 4
X_HD = 256
D_FF = 2816
EPS = 1e-6

VMEM_LIMIT_BYTES = 56 * 1024 * 1024

FOX_PAD = 128
FOX_QK_W = FOX_HEADS * FOX_PAD
A_FQ, A_FK, A_FV, A_FF, A_END = 0, 512, 1024, 1536, 1664
B_SU, B_HQ, B_HF, B_HI, B_HG, B_GL, B_END = 0, 512, 1024, 1536, 2048, 2560, 5632
FOX_BIAS_LANES = 3
LOG2E = 1.4426950408889634
FOX_SKIP_NATS = 30.0
FOX_UNROLL = 2
FOX_REF_MARGIN = 100.0
FOX_BOUNDED_MAX = 100.0


NEG_BIG = -1e30


def _cparams(sem, vmem_limit_bytes=VMEM_LIMIT_BYTES):
    return pltpu.CompilerParams(dimension_semantics=sem, vmem_limit_bytes=vmem_limit_bytes)


def _rms(xf, g):
    return xf * lax.rsqrt(jnp.mean(xf * xf, axis=-1, keepdims=True) + EPS) * g


def _sigmoid(x):
    return 0.5 + 0.5 * jnp.tanh(0.5 * x)


def _log_sigmoid(x):
    return jnp.minimum(x, 0.0) - jnp.log(1.0 + jnp.exp(-jnp.abs(x)))


def _dot(a, b):
    return jnp.dot(a, b, preferred_element_type=F32)


def _dot_nt(a, b):
    return lax.dot_general(a, b, (((1,), (1,)), ((), ())), preferred_element_type=F32)


def _dot_tn(a, b):
    return lax.dot_general(a, b, (((0,), (0,)), ((), ())), preferred_element_type=F32)


def _const_spec(shape, single=False, layer=None):
    nd = len(shape)
    if layer is not None:
        return pl.BlockSpec((None,) + tuple(shape), lambda *_: (layer,) + (0,) * nd,
                            pipeline_mode=pl.Buffered(1))
    if single:
        return pl.BlockSpec(shape, lambda *_: (0,) * nd, pipeline_mode=pl.Buffered(1))
    return pl.BlockSpec(shape, lambda *_: (0,) * nd)


def _top16(v):
    bits = lax.bitcast_convert_type(v, jnp.uint32) & jnp.uint32(0xFFFF0000)
    return lax.bitcast_convert_type(bits, F32)


def _split3(v):
    hi = _top16(v)
    r1 = v - hi
    mid = _top16(r1)
    return [hi, mid, r1 - mid]


CUM_ROWS = 256


def _inproj_fox_kernel(x_ref, g_ref, w_ref, gq_ref, gk_ref, qone_ref, kone_ref, hsum_ref, fb_ref, qoff_ref,
                       place_ref, q_ref, k_ref, v_ref, c_ref, carry_ref, *, tiles_per_seq):
    tm = x_ref.shape[0]

    @pl.when(pl.program_id(0) % tiles_per_seq == 0)
    def _():
        carry_ref[...] = jnp.zeros(carry_ref.shape, F32)

    h = _rms(x_ref[...], g_ref[...]).astype(BF16)

    def proj(lo, hi):
        return _dot(h, w_ref[:, lo:hi])

    def sumsq(t):
        return _dot((t * t).astype(BF16), hsum_ref[...])

    def headnorm(t, ss, g):
        return t * lax.rsqrt(ss * (1.0 / FOX_HD) + EPS) * g

    lf = _log_sigmoid(proj(A_FF, A_END) + fb_ref[...])
    r = lax.broadcasted_iota(jnp.int32, (CUM_ROWS, CUM_ROWS), 0)
    cc = lax.broadcasted_iota(jnp.int32, (CUM_ROWS, CUM_ROWS), 1)
    lower = (cc <= r).astype(F32)
    carry = carry_ref[...]
    parts = []
    for r0 in range(0, tm, CUM_ROWS):
        part = jnp.dot(lower, lf[r0:r0 + CUM_ROWS], precision=HIGHEST, preferred_element_type=F32) + carry
        carry = part[CUM_ROWS - 1:CUM_ROWS, :]
        parts.append(part)
    carry_ref[...] = carry
    cs = jnp.concatenate(parts, axis=0)
    c_ref[...] = cs
    d = -(cs * LOG2E)
    lane = lax.broadcasted_iota(jnp.int32, (tm, 128), 1)
    packed = None
    for n, piece in enumerate(_split3(d) + _split3(-d - qoff_ref[...])):
        term = jnp.where(lane < FOX_HEADS, piece, 0.0)
        term = term if n == 0 else pltpu.roll(term, FOX_HEADS * n, 1)
        packed = term if packed is None else packed + term
    bias = _dot(packed.astype(BF16), place_ref[...])
    qbias, kbias = bias[:, :FOX_QK_W], bias[:, FOX_QK_W:]

    chunks = range(0, BRANCH_W, 256)
    tq = [proj(A_FQ + c, A_FQ + c + 256) for c in chunks]
    tk = [proj(A_FK + c, A_FK + c + 256) for c in chunks]
    sq = [sumsq(t) for t in tq]
    sk = [sumsq(t) for t in tk]

    def store_padded(dst_ref, t, extra_ref, bias_vals, blk):
        for odd in range(2):
            head = 2 * blk + odd
            sl = slice(head * FOX_PAD, (head + 1) * FOX_PAD)
            src = pltpu.roll(t, FOX_HD, 1) if odd else t
            dst_ref[:, sl] = (jnp.where(lane < FOX_HD, src, 0.0) + extra_ref[:, sl] + bias_vals[:, sl]).astype(BF16)

    for n, c in enumerate(chunks):
        qn = headnorm(tq[n], sq[n], gq_ref[:, c:c + 256])
        kn = headnorm(tk[n], sk[n], gk_ref[:, c:c + 256])
        for half in range(2):
            blk = 2 * n + half
            store_padded(q_ref, qn[:, half * 128:(half + 1) * 128], qone_ref, qbias, blk)
            store_padded(k_ref, kn[:, half * 128:(half + 1) * 128], kone_ref, kbias, blk)
    v_ref[...] = proj(A_FV, A_FF).astype(BF16)


def _inproj_fox(x, g, w, layer, gq, gk, qone, kone, hsum, fb, qoff, place, seq, tm):
    n = x.shape[0]
    row = lambda c: pl.BlockSpec((tm, c), lambda i: (i, 0))
    outs = [(FOX_QK_W, BF16)] * 2 + [(512, BF16), (128, F32)]
    vec = _const_spec((1, FOX_QK_W))
    gain = _const_spec((1, BRANCH_W))
    return pl.pallas_call(
        functools.partial(_inproj_fox_kernel, tiles_per_seq=seq // tm),
        grid=(n // tm,),
        in_specs=[row(D_MODEL), _const_spec((1, D_MODEL)), _const_spec((D_MODEL, A_END), layer=layer),
                  gain, gain, vec, vec, _const_spec((256, 256)), _const_spec((1, 128)), _const_spec((1, 128)),
                  _const_spec((128, 2 * FOX_QK_W))],
        out_specs=[row(c) for c, _ in outs],
        out_shape=[jax.ShapeDtypeStruct((n, c), dt) for c, dt in outs],
        scratch_shapes=[pltpu.VMEM((1, 128), F32)],
        compiler_params=_cparams(("arbitrary",)),
        name="inproj_fox",
    )(x, g, w, gq, gk, qone, kone, hsum, fb, qoff, place)


def _inproj_rest_kernel(x_ref, g_ref, w_ref, su_ref, hq_ref, hf_ref, hi_ref, hg_ref, gl_ref):
    h = _rms(x_ref[...], g_ref[...]).astype(BF16)

    def proj(lo, hi):
        return _dot(h, w_ref[:, lo:hi])

    su_ref[...] = proj(B_SU, B_HQ).astype(BF16)
    hq_ref[...] = proj(B_HQ, B_HF).astype(BF16)
    hf_ref[...] = proj(B_HF, B_HI)
    hi_ref[...] = proj(B_HI, B_HG).astype(BF16)
    hg_ref[...] = proj(B_HG, B_GL).astype(BF16)
    for c in range(B_GL, B_END, 512):
        gl_ref[:, c - B_GL:c - B_GL + 512] = proj(c, c + 512).astype(BF16)


def _inproj_rest(x, g, w, layer, tm):
    n = x.shape[0]
    row = lambda c: pl.BlockSpec((tm, c), lambda i: (i, 0))
    outs = [(512, BF16)] * 2 + [(512, F32)] + [(512, BF16)] * 2 + [(3072, BF16)]
    return pl.pallas_call(
        _inproj_rest_kernel,
        grid=(n // tm,),
        in_specs=[row(D_MODEL), _const_spec((1, D_MODEL)), _const_spec((D_MODEL, B_END), layer=layer)],
        out_specs=[row(c) for c, _ in outs],
        out_shape=[jax.ShapeDtypeStruct((n, c), dt) for c, dt in outs],
        compiler_params=_cparams(("parallel",)),
        name="inproj_rest",
    )(x, g, w)


def _fox_kernel(jstart_ref, bounded_ref, q_ref, k_ref, v_ref, o_ref, m_ref, l_ref, acc_ref, *, tq, nq):
    b, p, i = pl.program_id(0), pl.program_id(1), pl.program_id(2)
    nchunk = tq // 128
    m_ref[...] = jnp.full(m_ref.shape, NEG_BIG, F32)
    l_ref[...] = jnp.zeros(l_ref.shape, F32)
    acc_ref[...] = jnp.zeros(acc_ref.shape, F32)

    def logits(j, hh, masked):
        start = pl.multiple_of(j * tq, tq)
        sl = slice(hh * FOX_PAD, (hh + 1) * FOX_PAD)
        s = _dot_nt(q_ref[:, sl], k_ref[pl.ds(start, tq), sl])
        if masked:
            rq = lax.broadcasted_iota(jnp.int32, (tq, tq), 0)
            ck = lax.broadcasted_iota(jnp.int32, (tq, tq), 1)
            s = jnp.where(ck <= rq, s, NEG_BIG)
        return s, v_ref[pl.ds(start, tq), :]

    def lane_partial_sum(pr):
        ls = pr[:, 0:128]
        for c in range(1, nchunk):
            ls = ls + pr[:, c * 128:(c + 1) * 128]
        return ls

    def step_bounded(j, masked, heads=(0, 1)):
        for hh in heads:
            s, vb = logits(j, hh, masked)
            pr = jnp.exp2(s)
            l_ref[hh] += lane_partial_sum(pr)
            acc_ref[hh] += _dot(pr.astype(BF16), vb)

    def step_online(j, masked, heads=(0, 1)):
        for hh in heads:
            s, vb = logits(j, hh, masked)
            mc = s[:, 0:128]
            for c in range(1, nchunk):
                mc = jnp.maximum(mc, s[:, c * 128:(c + 1) * 128])
            m_prev = m_ref[hh]
            m_new = jnp.maximum(m_prev, jnp.max(mc, axis=-1, keepdims=True))
            alpha = jnp.exp2(m_prev - m_new)
            pr = jnp.exp2(s - jnp.concatenate([m_new] * nchunk, axis=1))
            l_ref[hh] = alpha * l_ref[hh] + lane_partial_sum(pr)
            acc_ref[hh] = alpha * acc_ref[hh] + _dot(pr.astype(BF16), vb)
            m_ref[hh] = m_new

    base = ((b * (FOX_HEADS // 2) + p) * 2) * nq + i
    first = (jstart_ref[base], jstart_ref[base + nq])
    j0 = jnp.maximum(first[0], first[1])
    count = i - j0

    def sweep(step):
        for hh in range(2):
            def only(n, carry, hh=hh):
                step(first[hh] + n, False, (hh,))
                return carry

            lax.fori_loop(0, j0 - first[hh], only, 0)

        def body(n, carry):
            for u in range(FOX_UNROLL):
                step(j0 + FOX_UNROLL * n + u, False)
            return carry

        def single(n, carry):
            step(i - 1 - n, False)
            return carry

        lax.fori_loop(0, count // FOX_UNROLL, body, 0)
        lax.fori_loop(0, count % FOX_UNROLL, single, 0)
        step(i, True)

    bounded = bounded_ref[0] == 1

    @pl.when(bounded)
    def _():
        sweep(step_bounded)

    @pl.when(jnp.logical_not(bounded))
    def _():
        sweep(step_online)

    lane = lax.broadcasted_iota(jnp.int32, (tq, 128), 1)
    o0 = acc_ref[0] / jnp.sum(l_ref[0], axis=-1, keepdims=True)
    o1 = acc_ref[1] / jnp.sum(l_ref[1], axis=-1, keepdims=True)
    o_ref[...] = jnp.where(lane < FOX_HD, o0, o1).astype(BF16)


def _fox_attention(jstart, bounded, q, k, v, bsz, seq, tq):
    n = q.shape[0]
    nq = seq // tq
    grid_spec = pltpu.PrefetchScalarGridSpec(
        num_scalar_prefetch=2,
        grid=(bsz, FOX_HEADS // 2, nq),
        in_specs=[pl.BlockSpec((tq, 2 * FOX_PAD), lambda b, p, i, js, bd: (b * nq + i, p)),
                  pl.BlockSpec((seq, 2 * FOX_PAD), lambda b, p, i, js, bd: (b, p)),
                  pl.BlockSpec((seq, 128), lambda b, p, i, js, bd: (b, p))],
        out_specs=pl.BlockSpec((tq, 128), lambda b, p, i, js, bd: (b * nq + i, p)),
        scratch_shapes=[pltpu.VMEM((2, tq, 128), F32), pltpu.VMEM((2, tq, 128), F32),
                        pltpu.VMEM((2, tq, 128), F32)],
    )
    return pl.pallas_call(
        functools.partial(_fox_kernel, tq=tq, nq=nq),
        grid_spec=grid_spec,
        out_shape=jax.ShapeDtypeStruct((n, 512), BF16),
        compiler_params=_cparams(("parallel", "parallel", "arbitrary")),
        name="fox_attention",
    )(jstart, bounded, q, k, v)


def _fox_first_blocks(c, qk_bound, bsz, seq, tq):
    nq = seq // tq
    cb = c.reshape(bsz, nq, tq, 128)
    c_end = cb[:, :, tq - 1, :FOX_HEADS].transpose(0, 2, 1)
    c_start = cb[:, :, 0, :FOX_HEADS].transpose(0, 2, 1)
    gap = c_end[:, :, None, :] - c_start[:, :, :, None]
    skip = gap > (2.0 * qk_bound + FOX_SKIP_NATS)
    jstart = jnp.sum(skip.astype(jnp.int32), axis=-1)
    jstart = jnp.minimum(jstart, jnp.arange(nq, dtype=jnp.int32)[None, None, :])
    return jstart.reshape(-1)


S5_T = 8
S5_BLK = 128
S5_NBLK = BRANCH_W // S5_BLK
S5_KW = S5_T * S5_BLK
S5_BST = S5_NSTATE // S5_NBLK
S5_VMEM_LIMIT_BYTES = 58 * 1024 * 1024


def _s5_expand(src_ref, rep_ref, dst_ref, row_div, col_div):
    n = S5_KW
    for j in range(S5_NBLK):
        for c in range(0, n, 256):
            rg = (lax.broadcasted_iota(jnp.int32, (n, 256), 0) // row_div) % 8
            cg = ((lax.broadcasted_iota(jnp.int32, (n, 256), 1) + c) // col_div) % 8
            rep = _dot(src_ref[j], rep_ref[:, c:c + 256])
            dst_ref[j, :, c:c + 256] = jnp.where(rg == cg, rep, 0.0).astype(BF16)


def _s5_kernel(u_ref, wc_ref, bc_ref, cc_ref, rep_th_ref, rep_rp_ref, pw_ref, d_ref, wglu_ref, bglu_ref,
               o_ref, w_ref, bst_ref, cout_ref, uf_ref, y_ref, hr_ref, hi_ref, cr_ref, ci_ref, *, tt):
    t = pl.program_id(1)
    rows = tt // S5_T

    @pl.when(t == 0)
    def _():
        cr_ref[...] = jnp.zeros(cr_ref.shape, F32)
        ci_ref[...] = jnp.zeros(ci_ref.shape, F32)
        _s5_expand(wc_ref, rep_th_ref, w_ref, S5_GROUP, S5_GROUP)
        _s5_expand(bc_ref, rep_rp_ref, bst_ref, S5_GROUP, S5_STATE)
        _s5_expand(cc_ref, rep_th_ref, cout_ref, S5_STATE, S5_GROUP)

    for j in range(S5_NBLK):
        uf_ref[j] = u_ref[:, j * S5_BLK:(j + 1) * S5_BLK].astype(F32)

    def chunk_rows(j):
        return jnp.concatenate([uf_ref[j, pl.ds(s, rows, stride=S5_T), :] for s in range(S5_T)],
                               axis=1).astype(BF16)

    for j in range(S5_NBLK):
        hin = _dot(chunk_rows(j), bst_ref[j])
        hr_ref[:, j * S5_BST:(j + 1) * S5_BST] = hin[:, :S5_BST]
        hi_ref[:, j * S5_BST:(j + 1) * S5_BST] = hin[:, S5_BST:]

    row8 = lax.broadcasted_iota(jnp.int32, (8, S5_NSTATE), 0)

    def scan_block(r, carry):
        cr, ci = carry
        rws = pl.ds(pl.multiple_of(r * 8, 8), 8)
        xr = hr_ref[rws, :]
        xi = hi_ref[rws, :]
        for n, k in enumerate((1, 2, 4)):
            ar = pw_ref[2 * n]
            ai = pw_ref[2 * n + 1]
            sr = pltpu.roll(xr, k, 0)
            si = pltpu.roll(xi, k, 0)
            xr, xi = xr + ar * sr - ai * si, xi + ar * si + ai * sr
        pr = pw_ref[6]
        pi = pw_ref[7]
        xr, xi = xr + pr * cr - pi * ci, xi + pr * ci + pi * cr
        hr_ref[rws, :] = jnp.where(row8 == 0, cr, pltpu.roll(xr, 1, 0))
        hi_ref[rws, :] = jnp.where(row8 == 0, ci, pltpu.roll(xi, 1, 0))
        return (jnp.broadcast_to(xr[7:8, :], xr.shape), jnp.broadcast_to(xi[7:8, :], xi.shape))

    cr, ci = lax.fori_loop(0, rows // 8, scan_block, (cr_ref[...], ci_ref[...]))
    cr_ref[...] = cr
    ci_ref[...] = ci

    for j in range(S5_NBLK):
        st = slice(j * S5_BST, (j + 1) * S5_BST)
        hb = jnp.concatenate([hr_ref[:, st], hi_ref[:, st]], axis=1).astype(BF16)
        yj = _dot(chunk_rows(j), w_ref[j]) + _dot(hb, cout_ref[j])
        for s in range(S5_T):
            y_ref[j, pl.ds(s, rows, stride=S5_T), :] = yj[:, s * S5_BLK:(s + 1) * S5_BLK]

    y = jnp.concatenate([y_ref[j] + d_ref[:, j * S5_BLK:(j + 1) * S5_BLK] * uf_ref[j]
                         for j in range(S5_NBLK)], axis=1)
    z = 0.5 * y * (1.0 + jnp.tanh(math.sqrt(2.0 / math.pi) * (y + 0.044715 * (y * y * y))))
    gate = _sigmoid(_dot(z.astype(BF16), wglu_ref[...]) + bglu_ref[...])
    o_ref[...] = (z * gate).astype(BF16)


def _s5(u, wc, bc, cc, rep_th, rep_rp, pw, d, wglu, bglu, bsz, seq, tt):
    n = u.shape[0]
    nt = seq // tt
    rowspec = pl.BlockSpec((tt, BRANCH_W), lambda b, t: (b * nt + t, 0))
    compact = _const_spec((S5_NBLK, S5_KW, S5_BLK), single=True)
    expanded = pltpu.VMEM((S5_NBLK, S5_KW, S5_KW), BF16)
    return pl.pallas_call(
        functools.partial(_s5_kernel, tt=tt),
        grid=(bsz, nt),
        in_specs=[rowspec, compact, compact, compact,
                  _const_spec((S5_BLK, S5_KW), single=True), _const_spec((S5_BLK, S5_KW), single=True),
                  _const_spec((8, 8, S5_NSTATE), single=True), _const_spec((1, BRANCH_W)),
                  _const_spec((BRANCH_W, BRANCH_W)), _const_spec((1, BRANCH_W))],
        out_specs=rowspec,
        out_shape=jax.ShapeDtypeStruct((n, BRANCH_W), BF16),
        scratch_shapes=[expanded, expanded, expanded,
                        pltpu.VMEM((S5_NBLK, tt, S5_BLK), F32), pltpu.VMEM((S5_NBLK, tt, S5_BLK), F32),
                        pltpu.VMEM((tt // S5_T, S5_NSTATE), F32), pltpu.VMEM((tt // S5_T, S5_NSTATE), F32),
                        pltpu.VMEM((8, S5_NSTATE), F32), pltpu.VMEM((8, S5_NSTATE), F32)],
        compiler_params=_cparams(("parallel", "arbitrary"), S5_VMEM_LIMIT_BYTES),
        name="s5_mixer",
    )(u, wc, bc, cc, rep_th, rep_rp, pw, d, wglu, bglu)


def _s5_params(a_re, a_im, b_re, b_im, c_re, c_im, log_dt):
    g, p, gs, tt = S5_GROUPS, S5_STATE, S5_GROUP, S5_T
    dt = jnp.exp(log_dt.astype(F32))[:, None]
    ar, ai = a_re.astype(F32), a_im.astype(F32)
    mag = jnp.exp(dt * ar)
    abar_r, abar_i = mag * jnp.cos(dt * ai), mag * jnp.sin(dt * ai)
    inv_den = 1.0 / (ar * ar + ai * ai)
    nr, ni = abar_r - 1.0, abar_i
    coef_r = (nr * ar + ni * ai) * inv_den
    coef_i = (ni * ar - nr * ai) * inv_den
    br, bi = b_re.astype(F32), b_im.astype(F32)
    bbar_r = coef_r[..., None] * br - coef_i[..., None] * bi
    bbar_i = coef_r[..., None] * bi + coef_i[..., None] * br
    cr, ci = c_re.astype(F32), c_im.astype(F32)

    pows = [(jnp.ones_like(abar_r), jnp.zeros_like(abar_i))]
    for _ in range(tt):
        pr, pi = pows[-1]
        pows.append((pr * abar_r - pi * abar_i, pr * abar_i + pi * abar_r))
    pw_r = jnp.stack([x[0] for x in pows])
    pw_i = jnp.stack([x[1] for x in pows])

    bt_r, bt_i = bbar_r.transpose(0, 2, 1), bbar_i.transpose(0, 2, 1)
    cb_r = cr[:, :, None, :] * bt_r[:, None, :, :] - ci[:, :, None, :] * bt_i[:, None, :, :]
    cb_i = cr[:, :, None, :] * bt_i[:, None, :, :] + ci[:, :, None, :] * bt_r[:, None, :, :]
    kl = jnp.sum(pw_r[:tt, :, None, None, :] * cb_r[None] - pw_i[:tt, :, None, None, :] * cb_i[None],
                 axis=-1)
    nb, gl = S5_NBLK, g // S5_NBLK

    lag = jnp.arange(tt)[None, :] - jnp.arange(tt)[:, None]
    m = jnp.where((lag >= 0)[:, :, None, None, None], kl[jnp.clip(lag, 0, tt - 1)], 0.0)
    m = m.transpose(2, 0, 4, 1, 3).reshape(nb, gl, tt, gs, tt * gs)
    w = m.transpose(0, 2, 1, 3, 4).reshape(nb, S5_KW, S5_BLK)

    e_r, e_i = pw_r[:tt][::-1], pw_i[:tt][::-1]
    bs_r = e_r[:, :, None, :] * bt_r[None] - e_i[:, :, None, :] * bt_i[None]
    bs_i = e_r[:, :, None, :] * bt_i[None] + e_i[:, :, None, :] * bt_r[None]
    bs = jnp.stack([bs_r, bs_i]).transpose(2, 1, 3, 0, 4).reshape(nb, gl, tt, gs, 2 * p)
    bst = bs.transpose(0, 2, 1, 3, 4).reshape(nb, S5_KW, S5_BLK)

    q_r, q_i = pw_r[1:tt + 1], pw_i[1:tt + 1]
    co_r = cr[None] * q_r[:, :, None, :] - ci[None] * q_i[:, :, None, :]
    co_i = cr[None] * q_i[:, :, None, :] + ci[None] * q_r[:, :, None, :]
    co = jnp.stack([co_r, -co_i]).transpose(2, 0, 4, 1, 3).reshape(nb, gl, 2, p, tt * gs)
    cout = co.transpose(0, 2, 1, 3, 4).reshape(nb, S5_KW, S5_BLK)

    ar1, ai1 = pw_r[tt].reshape(-1), pw_i[tt].reshape(-1)
    apow = [(ar1, ai1)]
    for _ in range(7):
        pr, pi = apow[-1]
        apow.append((pr * ar1 - pi * ai1, pr * ai1 + pi * ar1))
    rows8 = jnp.arange(8)[:, None]
    tabs = []
    for k in (1, 2, 4):
        mask = (rows8 >= k).astype(F32)
        tabs += [mask * apow[k - 1][0][None, :], mask * apow[k - 1][1][None, :]]
    tabs.append(jnp.stack([apow[r][0] for r in range(8)]))
    tabs.append(jnp.stack([apow[r][1] for r in range(8)]))
    pw = jnp.stack(tabs)
    return w.astype(BF16), bst.astype(BF16), cout.astype(BF16), pw


def _hgrn_kernel(q_ref, f_ref, i_ref, g_ref, loglb_ref, log1mlb_ref, onemlb_ref, gain_ref,
                 o_ref, st_ref, b_ref, key_ref, qf_ref, oacc_ref, *, tt):
    t = pl.program_id(1)

    @pl.when(t == 0)
    def _():
        st_ref[...] = jnp.zeros(st_ref.shape, F32)

    c_sz, sub = HG_CHUNK, HG_SUB
    n_sub = c_sz // sub
    row_s = lax.broadcasted_iota(jnp.int32, (sub, 1), 0)
    lane_s = lax.broadcasted_iota(jnp.int32, (sub, c_sz), 1)

    z = f_ref[...]
    loglb = loglb_ref[...]
    bb = log1mlb_ref[...] + _log_sigmoid(z)
    logf = jnp.maximum(loglb, bb) + jnp.log(1.0 + jnp.exp(-jnp.abs(loglb - bb)))
    key_ref[...] = onemlb_ref[...] * _sigmoid(-z)
    qx = q_ref[...].astype(F32)
    qf_ref[...] = qx * _sigmoid(qx)
    rr = lax.broadcasted_iota(jnp.int32, (tt, tt), 0)
    cc = lax.broadcasted_iota(jnp.int32, (tt, tt), 1)
    lower = ((cc <= rr) & (rr // c_sz == cc // c_sz)).astype(F32)
    b_ref[...] = jnp.dot(lower, logf * LOG2E, precision=HIGHEST, preferred_element_type=F32)

    def head_chunk(rows, hd):
        sl = slice(hd * HG_D, (hd + 1) * HG_D)
        key = key_ref[rows, sl]
        qf = qf_ref[rows, sl]
        vb = i_ref[rows, sl]
        b = b_ref[rows, sl]
        b_last = b[c_sz - 1:c_sz, :]
        st = st_ref[hd]
        o_inter = _dot_nt((qf * jnp.exp2(b)).astype(BF16), st.astype(BF16))
        kd = (key * jnp.exp2(b_last - b)).astype(BF16)
        st_ref[hd] = st * jnp.exp2(b_last) + _dot_tn(vb, kd)

        srows = []
        for blk in range(n_sub):
            lo = blk * sub
            b_i = b[lo:lo + sub]
            q_i = qf[lo:lo + sub]
            k_i = key[lo:lo + sub]
            sd = jnp.zeros((sub, c_sz), F32)
            for s in range(sub):
                e = jnp.exp2(b_i - b_i[s:s + 1])
                col = jnp.sum(q_i * e * k_i[s:s + 1], axis=-1, keepdims=True)
                sd = jnp.where(lane_s == lo + s, col, sd)
            sd = jnp.where(lane_s <= lo + row_s, sd, 0.0)
            if blk > 0:
                ref = b[lo - 1:lo]
                qt = (q_i * jnp.exp2(b_i - ref)).astype(BF16)
                kt = jnp.concatenate([key[:lo] * jnp.exp2(ref - b[:lo]), jnp.zeros((c_sz - lo, HG_D), F32)],
                                     axis=0).astype(BF16)
                sd = jnp.where(lane_s < lo, _dot_nt(qt, kt), sd)
            srows.append(sd)
        scores = jnp.concatenate(srows, axis=0)
        oacc_ref[rows, sl] = o_inter + _dot(scores.astype(BF16), vb)

    def chunk(c, carry):
        rows = pl.ds(pl.multiple_of(c * c_sz, c_sz), c_sz)
        for hd in range(HG_HEADS):
            head_chunk(rows, hd)
        return carry

    lax.fori_loop(0, tt // c_sz, chunk, 0, unroll=True)

    for hd in range(HG_HEADS):
        sl = slice(hd * HG_D, (hd + 1) * HG_D)
        o = oacc_ref[:, sl]
        gx = g_ref[:, sl].astype(F32)
        y = o * lax.rsqrt(jnp.mean(o * o, axis=-1, keepdims=True) + EPS) * gain_ref[...]
        o_ref[:, sl] = (y * (gx * _sigmoid(gx))).astype(BF16)


def _hgrn(hq, hf, hi, hg, loglb, log1mlb, onemlb, gain, bsz, seq, tt):
    n = hq.shape[0]
    nt = seq // tt
    spec = pl.BlockSpec((tt, 512), lambda b, t: (b * nt + t, 0))
    return pl.pallas_call(
        functools.partial(_hgrn_kernel, tt=tt),
        grid=(bsz, nt),
        in_specs=[spec, spec, spec, spec, _const_spec((1, 512)), _const_spec((1, 512)),
                  _const_spec((1, 512)), _const_spec((1, HG_D))],
        out_specs=spec,
        out_shape=jax.ShapeDtypeStruct((n, 512), BF16),
        scratch_shapes=[pltpu.VMEM((HG_HEADS, HG_D, HG_D), F32)] + [pltpu.VMEM((tt, 512), F32)] * 4,
        compiler_params=_cparams(("parallel", "arbitrary")),
        name="hgrn2_mixer",
    )(hq, hf, hi, hg, loglb, log1mlb, onemlb, gain)


def _merge_kernel(x_ref, yf_ref, ys_ref, yh_ref, gl_ref, wb_ref, wo_ref, o_ref):
    m = None
    for n, y_ref in enumerate((yf_ref, ys_ref, yh_ref)):
        gate = _sigmoid(gl_ref[:, n * D_MODEL:(n + 1) * D_MODEL].astype(F32))
        term = gate * _dot(y_ref[...], wb_ref[n * BRANCH_W:(n + 1) * BRANCH_W, :])
        m = term if m is None else m + term
    o_ref[...] = x_ref[...] + _dot(m.astype(BF16), wo_ref[...])


def _merge(x, yf, ys, yh, gl, wb, wo, layer, tm):
    n = x.shape[0]
    row = lambda c: pl.BlockSpec((tm, c), lambda i: (i, 0))
    return pl.pallas_call(
        _merge_kernel,
        grid=(n // tm,),
        in_specs=[row(D_MODEL), row(512), row(512), row(512), row(3 * D_MODEL),
                  _const_spec((3 * BRANCH_W, D_MODEL), layer=layer),
                  _const_spec((D_MODEL, D_MODEL), layer=layer)],
        out_specs=row(D_MODEL),
        out_shape=jax.ShapeDtypeStruct((n, D_MODEL), F32),
        compiler_params=_cparams(("parallel",)),
        name="merge_outproj",
    )(x, yf, ys, yh, gl, wb, wo)


def _memkv_kernel(m_ref, g_ref, wk_ref, wv_ref, gk_ref, k_ref, v_ref):
    h = _rms(m_ref[0], g_ref[...]).astype(BF16)
    kk = _dot(h, wk_ref[...])
    for hd in range(X_HEADS):
        sl = slice(hd * X_HD, (hd + 1) * X_HD)
        k_ref[0, :, sl] = _rms(kk[:, sl], gk_ref[...]).astype(BF16)
    v_ref[0] = _dot(h, wv_ref[...]).astype(BF16)


def _memkv(mem, g, wk, wv, gk):
    bsz, nm, _ = mem.shape
    spec = pl.BlockSpec((1, nm, D_MODEL), lambda b: (b, 0, 0))
    return pl.pallas_call(
        _memkv_kernel,
        grid=(bsz,),
        in_specs=[spec, _const_spec((1, D_MODEL)), _const_spec((D_MODEL, D_MODEL)),
                  _const_spec((D_MODEL, D_MODEL)), _const_spec((1, X_HD))],
        out_specs=[spec, spec],
        out_shape=[jax.ShapeDtypeStruct(mem.shape, BF16)] * 2,
        compiler_params=_cparams(("parallel",)),
        name="mem_kv",
    )(mem, g, wk, wv, gk)


def _xattn_kernel(x_ref, g_ref, wq_ref, gq_ref, k_ref, v_ref, wo_ref, o_ref):
    x = x_ref[...]
    h = _rms(x, g_ref[...]).astype(BF16)
    q = _dot(h, wq_ref[...])
    outs = []
    for hd in range(X_HEADS):
        sl = slice(hd * X_HD, (hd + 1) * X_HD)
        qh = (_rms(q[:, sl], gq_ref[...]) * (X_HD ** -0.5)).astype(BF16)
        s = _dot_nt(qh, k_ref[0, :, sl])
        p = jnp.exp(s - jnp.max(s, axis=-1, keepdims=True))
        l = jnp.sum(p, axis=-1, keepdims=True)
        outs.append((_dot(p.astype(BF16), v_ref[0, :, sl]) / l).astype(BF16))
    o_ref[...] = x + _dot(jnp.concatenate(outs, axis=1), wo_ref[...])


def _xattn(x, g, wq, gq, km, vm, wo, layer, seq, tm):
    n = x.shape[0]
    nm = km.shape[1]
    per_b = seq // tm
    row = pl.BlockSpec((tm, D_MODEL), lambda i: (i, 0))
    kv = pl.BlockSpec((1, nm, D_MODEL), lambda i: (i // per_b, 0, 0))
    return pl.pallas_call(
        _xattn_kernel,
        grid=(n // tm,),
        in_specs=[row, _const_spec((1, D_MODEL)), _const_spec((D_MODEL, D_MODEL), layer=layer),
                  _const_spec((1, X_HD)), kv, kv, _const_spec((D_MODEL, D_MODEL), layer=layer)],
        out_specs=row,
        out_shape=jax.ShapeDtypeStruct((n, D_MODEL), F32),
        compiler_params=_cparams(("parallel",)),
        name="cross_attention",
    )(x, g, wq, gq, km, vm, wo)


FF_CHUNK = 256


def _ffn_kernel(x_ref, g_ref, wgu_ref, wd_ref, o_ref, act_ref):
    x = x_ref[...]
    h = _rms(x, g_ref[...]).astype(BF16)
    for c in range(0, D_FF, FF_CHUNK):
        a = _dot(h, wgu_ref[:, c:c + FF_CHUNK])
        b = _dot(h, wgu_ref[:, D_FF + c:D_FF + c + FF_CHUNK])
        act_ref[:, c:c + FF_CHUNK] = (a * _sigmoid(a) * b).astype(BF16)
    o_ref[...] = x + _dot(act_ref[...], wd_ref[...])


def _ffn(x, g, wgu, wd, layer, tm):
    n = x.shape[0]
    row = pl.BlockSpec((tm, D_MODEL), lambda i: (i, 0))
    return pl.pallas_call(
        _ffn_kernel,
        grid=(n // tm,),
        in_specs=[row, _const_spec((1, D_MODEL)), _const_spec((D_MODEL, 2 * D_FF), layer=layer),
                  _const_spec((D_FF, D_MODEL), layer=layer)],
        out_specs=row,
        out_shape=jax.ShapeDtypeStruct((n, D_MODEL), F32),
        scratch_shapes=[pltpu.VMEM((tm, D_FF), BF16)],
        compiler_params=_cparams(("parallel",)),
        name="swiglu",
    )(x, g, wgu, wd)


def _split_w_in(w_in):
    depth = w_in.shape[0]
    pad = jnp.zeros((depth, D_MODEL, A_END - A_FF - FOX_HEADS), w_in.dtype)
    w_fox = jnp.concatenate([w_in[:, :, :1544], pad], axis=-1).astype(BF16)
    return w_fox, w_in[:, :, 1544:].astype(BF16)


def _head_vec(v):
    return jnp.tile(v.astype(F32), FOX_HEADS).reshape(1, BRANCH_W)


def kernel(x, mem, norm_mix, w_in, fox_fbias, fox_qnorm, fox_knorm, s5_a_re, s5_a_im, s5_b_re, s5_b_im, s5_c_re, s5_c_im, s5_d, s5_log_dt, s5_w_glu, s5_b_glu, hg_lb, hg_onorm, w_branch, w_out, norm_x, norm_mem, xq, xk, xv, xo, x_qnorm, x_knorm, norm_ffn, w_gate_up, w_down):
    bsz, seq, _ = x.shape
    depth = w_in.shape[0]
    n = bsz * seq
    tm = min(1024, seq)
    tq = min(512, seq)
    tt = min(256, seq)
    s5_tt = min(2048, seq)
    assert seq % tm == 0 and seq % tq == 0 and seq % tt == 0 and tt % HG_CHUNK == 0
    assert seq % s5_tt == 0 and s5_tt % (8 * S5_T) == 0

    row = lambda v: v.astype(F32).reshape(1, -1)
    w_fox, w_rest = _split_w_in(w_in)
    s5_wglu = s5_w_glu.astype(BF16)
    wb, wo = w_branch.astype(BF16), w_out.astype(BF16)
    wq, wk, wv, wxo = xq.astype(BF16), xk.astype(BF16), xv.astype(BF16), xo.astype(BF16)
    wgu, wd = w_gate_up.astype(BF16), w_down.astype(BF16)

    lb_all = jnp.cumsum(jax.nn.softmax(hg_lb.astype(F32), axis=0), axis=0)
    lb_all = lb_all - lb_all[0:1]

    lane = jnp.arange(256)
    hsum = (lane[:, None] // FOX_HD == lane[None, :] // FOX_HD).astype(BF16)
    pad_lane = jnp.tile(jnp.arange(FOX_PAD), FOX_HEADS)
    k_lanes = (pad_lane >= FOX_HD) & (pad_lane < FOX_HD + FOX_BIAS_LANES)
    q_lanes = (pad_lane >= FOX_HD + FOX_BIAS_LANES) & (pad_lane < FOX_HD + 2 * FOX_BIAS_LANES)
    qone = k_lanes.astype(F32).reshape(1, FOX_QK_W)
    kone = q_lanes.astype(F32).reshape(1, FOX_QK_W)
    src = jnp.arange(128)
    dst = jnp.arange(2 * FOX_QK_W)
    src_n, src_h = src[:, None] // FOX_HEADS, src[:, None] % FOX_HEADS
    dst_key, dst_lane = dst[None, :] // FOX_QK_W, dst[None, :] % FOX_QK_W
    place = ((src_n < 2 * FOX_BIAS_LANES) & (dst_key == (src_n < FOX_BIAS_LANES))
             & (dst_lane // FOX_PAD == src_h) & (dst_lane % FOX_PAD == FOX_HD + src_n)).astype(BF16)

    col = jnp.arange(S5_KW)
    src = jnp.arange(S5_BLK)
    rep_th = ((src[:, None] // S5_GROUP == col[None, :] // S5_BLK)
              & (src[:, None] % S5_GROUP == col[None, :] % S5_GROUP)).astype(BF16)
    rep_rp = ((src[:, None] // S5_STATE == col[None, :] // S5_BST)
              & (src[:, None] % S5_STATE == col[None, :] % S5_STATE)).astype(BF16)

    s5_w, s5_bst, s5_cout, s5_pw = jax.vmap(_s5_params)(s5_a_re, s5_a_im, s5_b_re, s5_b_im,
                                                        s5_c_re, s5_c_im, s5_log_dt)

    xf = x.astype(F32).reshape(n, D_MODEL)
    for l in range(depth):
        gq = _head_vec(fox_qnorm[l]) * (FOX_HD ** -0.5 * LOG2E)
        gk = _head_vec(fox_knorm[l])
        fb = jnp.pad(fox_fbias[l].astype(F32), (0, 128 - FOX_HEADS)).reshape(1, 128)
        qk_bound = 1.01 * FOX_HD ** 0.5 * jnp.max(jnp.abs(fox_qnorm[l])) * jnp.max(jnp.abs(fox_knorm[l]))
        qk_bound2 = qk_bound * LOG2E
        qoff = jnp.full((1, 128), qk_bound2 - FOX_REF_MARGIN, F32)
        q, k, v, c = _inproj_fox(xf, row(norm_mix[l]), w_fox, l, gq, gk, qone, kone, hsum,
                                 fb, qoff, place, seq, tm)
        su, hq, hf, hi, hg, gl = _inproj_rest(xf, row(norm_mix[l]), w_rest, l, tm)

        jstart = _fox_first_blocks(c, qk_bound, bsz, seq, tq)
        bounded = (qk_bound2 <= FOX_BOUNDED_MAX).astype(jnp.int32).reshape(1)
        y_fox = _fox_attention(jstart, bounded, q, k, v, bsz, seq, tq)

        y_s5 = _s5(su, s5_w[l], s5_bst[l], s5_cout[l], rep_th, rep_rp, s5_pw[l], row(s5_d[l]), s5_wglu[l],
                   row(s5_b_glu[l]), bsz, seq, s5_tt)

        lb = lb_all[l].reshape(1, -1)
        y_hg = _hgrn(hq, hf, hi, hg, jnp.log(lb), jnp.log1p(-lb), 1.0 - lb, row(hg_onorm[l]), bsz, seq, tt)

        xf = _merge(xf, y_fox, y_s5, y_hg, gl, wb, wo, l, tm)

        km, vm = _memkv(mem.astype(F32), row(norm_mem[l]), wk[l], wv[l], row(x_knorm[l]))
        xf = _xattn(xf, row(norm_x[l]), wq, row(x_qnorm[l]), km, vm, wxo, l, seq, tm)
        xf = _ffn(xf, row(norm_ffn[l]), wgu, wd, l, tm)
    return xf.reshape(bsz, seq, D_MODEL).astype(x.dtype)
```

```python
import functools
import math

import jax
import jax.numpy as jnp
from jax import lax
from jax.experimental import pallas as pl
from jax.experimental.pallas import tpu as pltpu

F32 = jnp.float32
BF16 = jnp.bfloat16
HIGHEST = lax.Precision.HIGHEST

D_MODEL = 1024
BRANCH_W = 512
FOX_HD = 64
FOX_HEADS = 8
S5_GROUP = 16
S5_GROUPS = 32
S5_STATE = 64
S5_NSTATE = S5_GROUPS * S5_STATE
HG_HEADS = 4
HG_D = 128
HG_CHUNK = 64
HG_SUB = 8
X_HEADS = 4
X_HD = 256
D_FF = 2816
EPS = 1e-6

VMEM_LIMIT_BYTES = 56 * 1024 * 1024

FOX_PAD = 128
FOX_QK_W = FOX_HEADS * FOX_PAD
A_FQ, A_FK, A_FV, A_FF, A_END = 0, 512, 1024, 1536, 1664
B_SU, B_HQ, B_HF, B_HI, B_HG, B_GL, B_END = 0, 512, 1024, 1536, 2048, 2560, 5632
FOX_BIAS_LANES = 3
LOG2E = 1.4426950408889634
FOX_SKIP_NATS = 30.0
FOX_UNROLL = 2
FOX_REF_MARGIN = 100.0
FOX_BOUNDED_MAX = 100.0


NEG_BIG = -1e30


def _cparams(sem, vmem_limit_bytes=VMEM_LIMIT_BYTES):
    return pltpu.CompilerParams(dimension_semantics=sem, vmem_limit_bytes=vmem_limit_bytes)


def _rms(xf, g):
    return xf * lax.rsqrt(jnp.mean(xf * xf, axis=-1, keepdims=True) + EPS) * g


def _sigmoid(x):
    return 0.5 + 0.5 * jnp.tanh(0.5 * x)


def _log_sigmoid(x):
    return jnp.minimum(x, 0.0) - jnp.log(1.0 + jnp.exp(-jnp.abs(x)))


def _dot(a, b):
    return jnp.dot(a, b, preferred_element_type=F32)


def _dot_nt(a, b):
    return lax.dot_general(a, b, (((1,), (1,)), ((), ())), preferred_element_type=F32)


def _dot_tn(a, b):
    return lax.dot_general(a, b, (((0,), (0,)), ((), ())), preferred_element_type=F32)


def _const_spec(shape, single=False, layer=None):
    nd = len(shape)
    if layer is not None:
        return pl.BlockSpec((None,) + tuple(shape), lambda *_: (layer,) + (0,) * nd,
                            pipeline_mode=pl.Buffered(1))
    if single:
        return pl.BlockSpec(shape, lambda *_: (0,) * nd, pipeline_mode=pl.Buffered(1))
    return pl.BlockSpec(shape, lambda *_: (0,) * nd)


def _top16(v):
    bits = lax.bitcast_convert_type(v, jnp.uint32) & jnp.uint32(0xFFFF0000)
    return lax.bitcast_convert_type(bits, F32)


def _split3(v):
    hi = _top16(v)
    r1 = v - hi
    mid = _top16(r1)
    return [hi, mid, r1 - mid]


CUM_ROWS = 256


def _inproj_fox_kernel(x_ref, g_ref, w_ref, gq_ref, gk_ref, qone_ref, kone_ref, hsum_ref, fb_ref, qoff_ref,
                       place_ref, q_ref, k_ref, v_ref, c_ref, carry_ref, *, tiles_per_seq):
    tm = x_ref.shape[0]

    @pl.when(pl.program_id(0) % tiles_per_seq == 0)
    def _():
        carry_ref[...] = jnp.zeros(carry_ref.shape, F32)

    h = _rms(x_ref[...], g_ref[...]).astype(BF16)

    def proj(lo, hi):
        return _dot(h, w_ref[:, lo:hi])

    def sumsq(t):
        return _dot((t * t).astype(BF16), hsum_ref[...])

    def headnorm(t, ss, g):
        return t * lax.rsqrt(ss * (1.0 / FOX_HD) + EPS) * g

    lf = _log_sigmoid(proj(A_FF, A_END) + fb_ref[...])
    r = lax.broadcasted_iota(jnp.int32, (CUM_ROWS, CUM_ROWS), 0)
    cc = lax.broadcasted_iota(jnp.int32, (CUM_ROWS, CUM_ROWS), 1)
    lower = (cc <= r).astype(F32)
    carry = carry_ref[...]
    parts = []
    for r0 in range(0, tm, CUM_ROWS):
        part = jnp.dot(lower, lf[r0:r0 + CUM_ROWS], precision=HIGHEST, preferred_element_type=F32) + carry
        carry = part[CUM_ROWS - 1:CUM_ROWS, :]
        parts.append(part)
    carry_ref[...] = carry
    cs = jnp.concatenate(parts, axis=0)
    c_ref[...] = cs
    d = -(cs * LOG2E)
    lane = lax.broadcasted_iota(jnp.int32, (tm, 128), 1)
    packed = None
    for n, piece in enumerate(_split3(d) + _split3(-d - qoff_ref[...])):
        term = jnp.where(lane < FOX_HEADS, piece, 0.0)
        term = term if n == 0 else pltpu.roll(term, FOX_HEADS * n, 1)
        packed = term if packed is None else packed + term
    bias = _dot(packed.astype(BF16), place_ref[...])
    qbias, kbias = bias[:, :FOX_QK_W], bias[:, FOX_QK_W:]

    chunks = range(0, BRANCH_W, 256)
    tq = [proj(A_FQ + c, A_FQ + c + 256) for c in chunks]
    tk = [proj(A_FK + c, A_FK + c + 256) for c in chunks]
    sq = [sumsq(t) for t in tq]
    sk = [sumsq(t) for t in tk]

    def store_padded(dst_ref, t, extra_ref, bias_vals, blk):
        for odd in range(2):
            head = 2 * blk + odd
            sl = slice(head * FOX_PAD, (head + 1) * FOX_PAD)
            src = pltpu.roll(t, FOX_HD, 1) if odd else t
            dst_ref[:, sl] = (jnp.where(lane < FOX_HD, src, 0.0) + extra_ref[:, sl] + bias_vals[:, sl]).astype(BF16)

    for n, c in enumerate(chunks):
        qn = headnorm(tq[n], sq[n], gq_ref[:, c:c + 256])
        kn = headnorm(tk[n], sk[n], gk_ref[:, c:c + 256])
        for half in range(2):
            blk = 2 * n + half
            store_padded(q_ref, qn[:, half * 128:(half + 1) * 128], qone_ref, qbias, blk)
            store_padded(k_ref, kn[:, half * 128:(half + 1) * 128], kone_ref, kbias, blk)
    v_ref[...] = proj(A_FV, A_FF).astype(BF16)


def _inproj_fox(x, g, w, layer, gq, gk, qone, kone, hsum, fb, qoff, place, seq, tm):
    n = x.shape[0]
    row = lambda c: pl.BlockSpec((tm, c), lambda i: (i, 0))
    outs = [(FOX_QK_W, BF16)] * 2 + [(512, BF16), (128, F32)]
    vec = _const_spec((1, FOX_QK_W))
    gain = _const_spec((1, BRANCH_W))
    return pl.pallas_call(
        functools.partial(_inproj_fox_kernel, tiles_per_seq=seq // tm),
        grid=(n // tm,),
        in_specs=[row(D_MODEL), _const_spec((1, D_MODEL)), _const_spec((D_MODEL, A_END), layer=layer),
                  gain, gain, vec, vec, _const_spec((256, 256)), _const_spec((1, 128)), _const_spec((1, 128)),
                  _const_spec((128, 2 * FOX_QK_W))],
        out_specs=[row(c) for c, _ in outs],
        out_shape=[jax.ShapeDtypeStruct((n, c), dt) for c, dt in outs],
        scratch_shapes=[pltpu.VMEM((1, 128), F32)],
        compiler_params=_cparams(("arbitrary",)),
        name="inproj_fox",
    )(x, g, w, gq, gk, qone, kone, hsum, fb, qoff, place)


def _inproj_rest_kernel(x_ref, g_ref, w_ref, su_ref, hq_ref, hf_ref, hi_ref, hg_ref, gl_ref):
    h = _rms(x_ref[...], g_ref[...]).astype(BF16)

    def proj(lo, hi):
        return _dot(h, w_ref[:, lo:hi])

    su_ref[...] = proj(B_SU, B_HQ).astype(BF16)
    hq_ref[...] = proj(B_HQ, B_HF).astype(BF16)
    hf_ref[...] = proj(B_HF, B_HI)
    hi_ref[...] = proj(B_HI, B_HG).astype(BF16)
    hg_ref[...] = proj(B_HG, B_GL).astype(BF16)
    for c in range(B_GL, B_END, 512):
        gl_ref[:, c - B_GL:c - B_GL + 512] = proj(c, c + 512).astype(BF16)


def _inproj_rest(x, g, w, layer, tm):
    n = x.shape[0]
    row = lambda c: pl.BlockSpec((tm, c), lambda i: (i, 0))
    outs = [(512, BF16)] * 2 + [(512, F32)] + [(512, BF16)] * 2 + [(3072, BF16)]
    return pl.pallas_call(
        _inproj_rest_kernel,
        grid=(n // tm,),
        in_specs=[row(D_MODEL), _const_spec((1, D_MODEL)), _const_spec((D_MODEL, B_END), layer=layer)],
        out_specs=[row(c) for c, _ in outs],
        out_shape=[jax.ShapeDtypeStruct((n, c), dt) for c, dt in outs],
        compiler_params=_cparams(("parallel",)),
        name="inproj_rest",
    )(x, g, w)


def _fox_kernel(jstart_ref, bounded_ref, q_ref, k_ref, v_ref, o_ref, m_ref, l_ref, acc_ref, *, tq, nq):
    b, p, i = pl.program_id(0), pl.program_id(1), pl.program_id(2)
    nchunk = tq // 128
    l_ref[...] = jnp.zeros(l_ref.shape, F32)
    acc_ref[...] = jnp.zeros(acc_ref.shape, F32)

    def logits(j, hh, masked):
        start = pl.multiple_of(j * tq, tq)
        sl = slice(hh * FOX_PAD, (hh + 1) * FOX_PAD)
        s = _dot_nt(q_ref[:, sl], k_ref[pl.ds(start, tq), sl])
        if masked:
            rq = lax.broadcasted_iota(jnp.int32, (tq, tq), 0)
            ck = lax.broadcasted_iota(jnp.int32, (tq, tq), 1)
            s = jnp.where(ck <= rq, s, NEG_BIG)
        return s, v_ref[pl.ds(start, tq), :]

    def lane_partial_sum(pr):
        ls = pr[:, 0:128]
        for c in range(1, nchunk):
            ls = ls + pr[:, c * 128:(c + 1) * 128]
        return ls

    def step_bounded(j, masked, heads=(0, 1)):
        for hh in heads:
            s, vb = logits(j, hh, masked)
            pr = jnp.exp2(s)
            l_ref[hh] += lane_partial_sum(pr)
            acc_ref[hh] += _dot(pr.astype(BF16), vb)

    def step_online(j, masked, heads=(0, 1)):
        for hh in heads:
            s, vb = logits(j, hh, masked)
            mc = s[:, 0:128]
            for c in range(1, nchunk):
                mc = jnp.maximum(mc, s[:, c * 128:(c + 1) * 128])
            m_prev = m_ref[hh]
            m_new = jnp.maximum(m_prev, jnp.max(mc, axis=-1, keepdims=True))
            alpha = jnp.exp2(m_prev - m_new)
            pr = jnp.exp2(s - jnp.concatenate([m_new] * nchunk, axis=1))
            l_ref[hh] = alpha * l_ref[hh] + lane_partial_sum(pr)
            acc_ref[hh] = alpha * acc_ref[hh] + _dot(pr.astype(BF16), vb)
            m_ref[hh] = m_new

    base = ((b * (FOX_HEADS // 2) + p) * 2) * nq + i
    first = (jstart_ref[base], jstart_ref[base + nq])
    j0 = jnp.maximum(first[0], first[1])
    count = i - j0

    def sweep(step):
        for hh in range(2):
            def only(n, carry, hh=hh):
                step(first[hh] + n, False, (hh,))
                return carry

            lax.fori_loop(0, j0 - first[hh], only, 0)

        def body(n, carry):
            for u in range(FOX_UNROLL):
                step(j0 + FOX_UNROLL * n + u, False)
            return carry

        def single(n, carry):
            step(i - 1 - n, False)
            return carry

        lax.fori_loop(0, count // FOX_UNROLL, body, 0)
        lax.fori_loop(0, count % FOX_UNROLL, single, 0)
        step(i, True)

    bounded = bounded_ref[0] == 1

    @pl.when(bounded)
    def _():
        sweep(step_bounded)

    @pl.when(jnp.logical_not(bounded))
    def _():
        m_ref[...] = jnp.full(m_ref.shape, NEG_BIG, F32)
        sweep(step_online)

    lane = lax.broadcasted_iota(jnp.int32, (tq, 128), 1)
    o0 = acc_ref[0] / jnp.sum(l_ref[0], axis=-1, keepdims=True)
    o1 = acc_ref[1] / jnp.sum(l_ref[1], axis=-1, keepdims=True)
    o_ref[...] = jnp.where(lane < FOX_HD, o0, o1).astype(BF16)


def _fox_attention(jstart, bounded, q, k, v, bsz, seq, tq):
    n = q.shape[0]
    nq = seq // tq
    grid_spec = pltpu.PrefetchScalarGridSpec(
        num_scalar_prefetch=2,
        grid=(bsz, FOX_HEADS // 2, nq),
        in_specs=[pl.BlockSpec((tq, 2 * FOX_PAD), lambda b, p, i, js, bd: (b * nq + i, p)),
                  pl.BlockSpec((seq, 2 * FOX_PAD), lambda b, p, i, js, bd: (b, p)),
                  pl.BlockSpec((seq, 128), lambda b, p, i, js, bd: (b, p))],
        out_specs=pl.BlockSpec((tq, 128), lambda b, p, i, js, bd: (b * nq + i, p)),
        scratch_shapes=[pltpu.VMEM((2, tq, 128), F32), pltpu.VMEM((2, tq, 128), F32),
                        pltpu.VMEM((2, tq, 128), F32)],
    )
    return pl.pallas_call(
        functools.partial(_fox_kernel, tq=tq, nq=nq),
        grid_spec=grid_spec,
        out_shape=jax.ShapeDtypeStruct((n, 512), BF16),
        compiler_params=_cparams(("parallel", "parallel", "arbitrary")),
        name="fox_attention",
    )(jstart, bounded, q, k, v)


def _fox_first_blocks(c, qk_bound, bsz, seq, tq):
    nq = seq // tq
    cb = c.reshape(bsz, nq, tq, 128)
    c_end = cb[:, :, tq - 1, :FOX_HEADS].transpose(0, 2, 1)
    c_start = cb[:, :, 0, :FOX_HEADS].transpose(0, 2, 1)
    gap = c_end[:, :, None, :] - c_start[:, :, :, None]
    skip = gap > (2.0 * qk_bound + FOX_SKIP_NATS)
    jstart = jnp.sum(skip.astype(jnp.int32), axis=-1)
    jstart = jnp.minimum(jstart, jnp.arange(nq, dtype=jnp.int32)[None, None, :])
    return jstart.reshape(-1)


S5_T = 8
S5_BLK = 128
S5_NBLK = BRANCH_W // S5_BLK
S5_KW = S5_T * S5_BLK
S5_BST = S5_NSTATE // S5_NBLK
S5_VMEM_LIMIT_BYTES = 58 * 1024 * 1024


def _s5_expand(src_ref, rep_ref, dst_ref, row_div, col_div):
    n = S5_KW
    for j in range(S5_NBLK):
        for c in range(0, n, 256):
            rg = (lax.broadcasted_iota(jnp.int32, (n, 256), 0) // row_div) % 8
            cg = ((lax.broadcasted_iota(jnp.int32, (n, 256), 1) + c) // col_div) % 8
            rep = _dot(src_ref[j], rep_ref[:, c:c + 256])
            dst_ref[j, :, c:c + 256] = jnp.where(rg == cg, rep, 0.0).astype(BF16)


def _s5_kernel(u_ref, wc_ref, bc_ref, cc_ref, rep_th_ref, rep_rp_ref, pw_ref, d_ref, wglu_ref, bglu_ref,
               o_ref, w_ref, bst_ref, cout_ref, uf_ref, y_ref, hr_ref, hi_ref, cr_ref, ci_ref, *, tt):
    t = pl.program_id(1)
    rows = tt // S5_T

    @pl.when(t == 0)
    def _():
        cr_ref[...] = jnp.zeros(cr_ref.shape, F32)
        ci_ref[...] = jnp.zeros(ci_ref.shape, F32)
        _s5_expand(wc_ref, rep_th_ref, w_ref, S5_GROUP, S5_GROUP)
        _s5_expand(bc_ref, rep_rp_ref, bst_ref, S5_GROUP, S5_STATE)
        _s5_expand(cc_ref, rep_th_ref, cout_ref, S5_STATE, S5_GROUP)

    for j in range(S5_NBLK):
        uf_ref[j] = u_ref[:, j * S5_BLK:(j + 1) * S5_BLK].astype(F32)

    def chunk_rows(j):
        return jnp.concatenate([uf_ref[j, pl.ds(s, rows, stride=S5_T), :] for s in range(S5_T)],
                               axis=1).astype(BF16)

    for j in range(S5_NBLK):
        hin = _dot(chunk_rows(j), bst_ref[j])
        hr_ref[:, j * S5_BST:(j + 1) * S5_BST] = hin[:, :S5_BST]
        hi_ref[:, j * S5_BST:(j + 1) * S5_BST] = hin[:, S5_BST:]

    row8 = lax.broadcasted_iota(jnp.int32, (8, S5_NSTATE), 0)

    def scan_block(r, carry):
        cr, ci = carry
        rws = pl.ds(pl.multiple_of(r * 8, 8), 8)
        xr = hr_ref[rws, :]
        xi = hi_ref[rws, :]
        for n, k in enumerate((1, 2, 4)):
            ar = pw_ref[2 * n]
            ai = pw_ref[2 * n + 1]
            sr = pltpu.roll(xr, k, 0)
            si = pltpu.roll(xi, k, 0)
            xr, xi = xr + ar * sr - ai * si, xi + ar * si + ai * sr
        pr = pw_ref[6]
        pi = pw_ref[7]
        xr, xi = xr + pr * cr - pi * ci, xi + pr * ci + pi * cr
        hr_ref[rws, :] = jnp.where(row8 == 0, cr, pltpu.roll(xr, 1, 0))
        hi_ref[rws, :] = jnp.where(row8 == 0, ci, pltpu.roll(xi, 1, 0))
        return (jnp.broadcast_to(xr[7:8, :], xr.shape), jnp.broadcast_to(xi[7:8, :], xi.shape))

    cr, ci = lax.fori_loop(0, rows // 8, scan_block, (cr_ref[...], ci_ref[...]))
    cr_ref[...] = cr
    ci_ref[...] = ci

    for j in range(S5_NBLK):
        st = slice(j * S5_BST, (j + 1) * S5_BST)
        hb = jnp.concatenate([hr_ref[:, st], hi_ref[:, st]], axis=1).astype(BF16)
        yj = _dot(chunk_rows(j), w_ref[j]) + _dot(hb, cout_ref[j])
        for s in range(S5_T):
            y_ref[j, pl.ds(s, rows, stride=S5_T), :] = yj[:, s * S5_BLK:(s + 1) * S5_BLK]

    y = jnp.concatenate([y_ref[j] + d_ref[:, j * S5_BLK:(j + 1) * S5_BLK] * uf_ref[j]
                         for j in range(S5_NBLK)], axis=1)
    z = 0.5 * y * (1.0 + jnp.tanh(math.sqrt(2.0 / math.pi) * (y + 0.044715 * (y * y * y))))
    gate = _sigmoid(_dot(z.astype(BF16), wglu_ref[...]) + bglu_ref[...])
    o_ref[...] = (z * gate).astype(BF16)


def _s5(u, wc, bc, cc, rep_th, rep_rp, pw, d, wglu, bglu, bsz, seq, tt):
    n = u.shape[0]
    nt = seq // tt
    rowspec = pl.BlockSpec((tt, BRANCH_W), lambda b, t: (b * nt + t, 0))
    compact = _const_spec((S5_NBLK, S5_KW, S5_BLK), single=True)
    expanded = pltpu.VMEM((S5_NBLK, S5_KW, S5_KW), BF16)
    return pl.pallas_call(
        functools.partial(_s5_kernel, tt=tt),
        grid=(bsz, nt),
        in_specs=[rowspec, compact, compact, compact,
                  _const_spec((S5_BLK, S5_KW), single=True), _const_spec((S5_BLK, S5_KW), single=True),
                  _const_spec((8, 8, S5_NSTATE), single=True), _const_spec((1, BRANCH_W)),
                  _const_spec((BRANCH_W, BRANCH_W)), _const_spec((1, BRANCH_W))],
        out_specs=rowspec,
        out_shape=jax.ShapeDtypeStruct((n, BRANCH_W), BF16),
        scratch_shapes=[expanded, expanded, expanded,
                        pltpu.VMEM((S5_NBLK, tt, S5_BLK), F32), pltpu.VMEM((S5_NBLK, tt, S5_BLK), F32),
                        pltpu.VMEM((tt // S5_T, S5_NSTATE), F32), pltpu.VMEM((tt // S5_T, S5_NSTATE), F32),
                        pltpu.VMEM((8, S5_NSTATE), F32), pltpu.VMEM((8, S5_NSTATE), F32)],
        compiler_params=_cparams(("parallel", "arbitrary"), S5_VMEM_LIMIT_BYTES),
        name="s5_mixer",
    )(u, wc, bc, cc, rep_th, rep_rp, pw, d, wglu, bglu)


def _s5_params(a_re, a_im, b_re, b_im, c_re, c_im, log_dt):
    g, p, gs, tt = S5_GROUPS, S5_STATE, S5_GROUP, S5_T
    dt = jnp.exp(log_dt.astype(F32))[:, None]
    ar, ai = a_re.astype(F32), a_im.astype(F32)
    mag = jnp.exp(dt * ar)
    abar_r, abar_i = mag * jnp.cos(dt * ai), mag * jnp.sin(dt * ai)
    inv_den = 1.0 / (ar * ar + ai * ai)
    nr, ni = abar_r - 1.0, abar_i
    coef_r = (nr * ar + ni * ai) * inv_den
    coef_i = (ni * ar - nr * ai) * inv_den
    br, bi = b_re.astype(F32), b_im.astype(F32)
    bbar_r = coef_r[..., None] * br - coef_i[..., None] * bi
    bbar_i = coef_r[..., None] * bi + coef_i[..., None] * br
    cr, ci = c_re.astype(F32), c_im.astype(F32)

    pows = [(jnp.ones_like(abar_r), jnp.zeros_like(abar_i))]
    for _ in range(tt):
        pr, pi = pows[-1]
        pows.append((pr * abar_r - pi * abar_i, pr * abar_i + pi * abar_r))
    pw_r = jnp.stack([x[0] for x in pows])
    pw_i = jnp.stack([x[1] for x in pows])

    bt_r, bt_i = bbar_r.transpose(0, 2, 1), bbar_i.transpose(0, 2, 1)
    cb_r = cr[:, :, None, :] * bt_r[:, None, :, :] - ci[:, :, None, :] * bt_i[:, None, :, :]
    cb_i = cr[:, :, None, :] * bt_i[:, None, :, :] + ci[:, :, None, :] * bt_r[:, None, :, :]
    kl = jnp.sum(pw_r[:tt, :, None, None, :] * cb_r[None] - pw_i[:tt, :, None, None, :] * cb_i[None],
                 axis=-1)
    nb, gl = S5_NBLK, g // S5_NBLK

    lag = jnp.arange(tt)[None, :] - jnp.arange(tt)[:, None]
    m = jnp.where((lag >= 0)[:, :, None, None, None], kl[jnp.clip(lag, 0, tt - 1)], 0.0)
    m = m.transpose(2, 0, 4, 1, 3).reshape(nb, gl, tt, gs, tt * gs)
    w = m.transpose(0, 2, 1, 3, 4).reshape(nb, S5_KW, S5_BLK)

    e_r, e_i = pw_r[:tt][::-1], pw_i[:tt][::-1]
    bs_r = e_r[:, :, None, :] * bt_r[None] - e_i[:, :, None, :] * bt_i[None]
    bs_i = e_r[:, :, None, :] * bt_i[None] + e_i[:, :, None, :] * bt_r[None]
    bs = jnp.stack([bs_r, bs_i]).transpose(2, 1, 3, 0, 4).reshape(nb, gl, tt, gs, 2 * p)
    bst = bs.transpose(0, 2, 1, 3, 4).reshape(nb, S5_KW, S5_BLK)

    q_r, q_i = pw_r[1:tt + 1], pw_i[1:tt + 1]
    co_r = cr[None] * q_r[:, :, None, :] - ci[None] * q_i[:, :, None, :]
    co_i = cr[None] * q_i[:, :, None, :] + ci[None] * q_r[:, :, None, :]
    co = jnp.stack([co_r, -co_i]).transpose(2, 0, 4, 1, 3).reshape(nb, gl, 2, p, tt * gs)
    cout = co.transpose(0, 2, 1, 3, 4).reshape(nb, S5_KW, S5_BLK)

    ar1, ai1 = pw_r[tt].reshape(-1), pw_i[tt].reshape(-1)
    apow = [(ar1, ai1)]
    for _ in range(7):
        pr, pi = apow[-1]
        apow.append((pr * ar1 - pi * ai1, pr * ai1 + pi * ar1))
    rows8 = jnp.arange(8)[:, None]
    tabs = []
    for k in (1, 2, 4):
        mask = (rows8 >= k).astype(F32)
        tabs += [mask * apow[k - 1][0][None, :], mask * apow[k - 1][1][None, :]]
    tabs.append(jnp.stack([apow[r][0] for r in range(8)]))
    tabs.append(jnp.stack([apow[r][1] for r in range(8)]))
    pw = jnp.stack(tabs)
    return w.astype(BF16), bst.astype(BF16), cout.astype(BF16), pw


def _hgrn_kernel(q_ref, f_ref, i_ref, g_ref, loglb_ref, log1mlb_ref, onemlb_ref, gain_ref,
                 o_ref, st_ref, b_ref, key_ref, qf_ref, oacc_ref, *, tt):
    t = pl.program_id(1)

    @pl.when(t == 0)
    def _():
        st_ref[...] = jnp.zeros(st_ref.shape, F32)

    c_sz, sub = HG_CHUNK, HG_SUB
    n_sub = c_sz // sub
    row_s = lax.broadcasted_iota(jnp.int32, (sub, 1), 0)
    lane_s = lax.broadcasted_iota(jnp.int32, (sub, c_sz), 1)

    z = f_ref[...]
    loglb = loglb_ref[...]
    bb = log1mlb_ref[...] + _log_sigmoid(z)
    logf = jnp.maximum(loglb, bb) + jnp.log(1.0 + jnp.exp(-jnp.abs(loglb - bb)))
    key_ref[...] = onemlb_ref[...] * _sigmoid(-z)
    qx = q_ref[...].astype(F32)
    qf_ref[...] = qx * _sigmoid(qx)
    rr = lax.broadcasted_iota(jnp.int32, (tt, tt), 0)
    cc = lax.broadcasted_iota(jnp.int32, (tt, tt), 1)
    lower = ((cc <= rr) & (rr // c_sz == cc // c_sz)).astype(F32)
    b_ref[...] = jnp.dot(lower, logf * LOG2E, precision=HIGHEST, preferred_element_type=F32)

    def head_chunk(rows, hd):
        sl = slice(hd * HG_D, (hd + 1) * HG_D)
        key = key_ref[rows, sl]
        qf = qf_ref[rows, sl]
        vb = i_ref[rows, sl]
        b = b_ref[rows, sl]
        b_last = b[c_sz - 1:c_sz, :]
        st = st_ref[hd]
        o_inter = _dot_nt((qf * jnp.exp2(b)).astype(BF16), st.astype(BF16))
        kd = (key * jnp.exp2(b_last - b)).astype(BF16)
        st_ref[hd] = st * jnp.exp2(b_last) + _dot_tn(vb, kd)

        srows = []
        for blk in range(n_sub):
            lo = blk * sub
            b_i = b[lo:lo + sub]
            q_i = qf[lo:lo + sub]
            k_i = key[lo:lo + sub]
            sd = jnp.zeros((sub, c_sz), F32)
            for s in range(sub):
                e = jnp.exp2(b_i - b_i[s:s + 1])
                col = jnp.sum(q_i * e * k_i[s:s + 1], axis=-1, keepdims=True)
                sd = jnp.where(lane_s == lo + s, col, sd)
            sd = jnp.where(lane_s <= lo + row_s, sd, 0.0)
            if blk > 0:
                ref = b[lo - 1:lo]
                qt = (q_i * jnp.exp2(b_i - ref)).astype(BF16)
                kt = jnp.concatenate([key[:lo] * jnp.exp2(ref - b[:lo]), jnp.zeros((c_sz - lo, HG_D), F32)],
                                     axis=0).astype(BF16)
                sd = jnp.where(lane_s < lo, _dot_nt(qt, kt), sd)
            srows.append(sd)
        scores = jnp.concatenate(srows, axis=0)
        oacc_ref[rows, sl] = o_inter + _dot(scores.astype(BF16), vb)

    def chunk(c, carry):
        rows = pl.ds(pl.multiple_of(c * c_sz, c_sz), c_sz)
        for hd in range(HG_HEADS):
            head_chunk(rows, hd)
        return carry

    lax.fori_loop(0, tt // c_sz, chunk, 0, unroll=True)

    for hd in range(HG_HEADS):
        sl = slice(hd * HG_D, (hd + 1) * HG_D)
        o = oacc_ref[:, sl]
        gx = g_ref[:, sl].astype(F32)
        y = o * lax.rsqrt(jnp.mean(o * o, axis=-1, keepdims=True) + EPS) * gain_ref[...]
        o_ref[:, sl] = (y * (gx * _sigmoid(gx))).astype(BF16)


def _hgrn(hq, hf, hi, hg, loglb, log1mlb, onemlb, gain, bsz, seq, tt):
    n = hq.shape[0]
    nt = seq // tt
    spec = pl.BlockSpec((tt, 512), lambda b, t: (b * nt + t, 0))
    return pl.pallas_call(
        functools.partial(_hgrn_kernel, tt=tt),
        grid=(bsz, nt),
        in_specs=[spec, spec, spec, spec, _const_spec((1, 512)), _const_spec((1, 512)),
                  _const_spec((1, 512)), _const_spec((1, HG_D))],
        out_specs=spec,
        out_shape=jax.ShapeDtypeStruct((n, 512), BF16),
        scratch_shapes=[pltpu.VMEM((HG_HEADS, HG_D, HG_D), F32)] + [pltpu.VMEM((tt, 512), F32)] * 4,
        compiler_params=_cparams(("parallel", "arbitrary")),
        name="hgrn2_mixer",
    )(hq, hf, hi, hg, loglb, log1mlb, onemlb, gain)


def _merge_kernel(x_ref, yf_ref, ys_ref, yh_ref, gl_ref, wb_ref, wo_ref, o_ref):
    m = None
    for n, y_ref in enumerate((yf_ref, ys_ref, yh_ref)):
        gate = _sigmoid(gl_ref[:, n * D_MODEL:(n + 1) * D_MODEL].astype(F32))
        term = gate * _dot(y_ref[...], wb_ref[n * BRANCH_W:(n + 1) * BRANCH_W, :])
        m = term if m is None else m + term
    o_ref[...] = x_ref[...] + _dot(m.astype(BF16), wo_ref[...])


def _merge(x, yf, ys, yh, gl, wb, wo, layer, tm):
    n = x.shape[0]
    row = lambda c: pl.BlockSpec((tm, c), lambda i: (i, 0))
    return pl.pallas_call(
        _merge_kernel,
        grid=(n // tm,),
        in_specs=[row(D_MODEL), row(512), row(512), row(512), row(3 * D_MODEL),
                  _const_spec((3 * BRANCH_W, D_MODEL), layer=layer),
                  _const_spec((D_MODEL, D_MODEL), layer=layer)],
        out_specs=row(D_MODEL),
        out_shape=jax.ShapeDtypeStruct((n, D_MODEL), F32),
        compiler_params=_cparams(("parallel",)),
        name="merge_outproj",
    )(x, yf, ys, yh, gl, wb, wo)


def _memkv_kernel(m_ref, g_ref, wk_ref, wv_ref, gk_ref, k_ref, v_ref):
    h = _rms(m_ref[0], g_ref[...]).astype(BF16)
    kk = _dot(h, wk_ref[...])
    for hd in range(X_HEADS):
        sl = slice(hd * X_HD, (hd + 1) * X_HD)
        k_ref[0, :, sl] = _rms(kk[:, sl], gk_ref[...]).astype(BF16)
    v_ref[0] = _dot(h, wv_ref[...]).astype(BF16)


def _memkv(mem, g, wk, wv, gk):
    bsz, nm, _ = mem.shape
    spec = pl.BlockSpec((1, nm, D_MODEL), lambda b: (b, 0, 0))
    return pl.pallas_call(
        _memkv_kernel,
        grid=(bsz,),
        in_specs=[spec, _const_spec((1, D_MODEL)), _const_spec((D_MODEL, D_MODEL)),
                  _const_spec((D_MODEL, D_MODEL)), _const_spec((1, X_HD))],
        out_specs=[spec, spec],
        out_shape=[jax.ShapeDtypeStruct(mem.shape, BF16)] * 2,
        compiler_params=_cparams(("parallel",)),
        name="mem_kv",
    )(mem, g, wk, wv, gk)


def _xattn_kernel(x_ref, g_ref, wq_ref, gq_ref, k_ref, v_ref, wo_ref, o_ref):
    x = x_ref[...]
    h = _rms(x, g_ref[...]).astype(BF16)
    q = _dot(h, wq_ref[...])
    outs = []
    for hd in range(X_HEADS):
        sl = slice(hd * X_HD, (hd + 1) * X_HD)
        qh = (_rms(q[:, sl], gq_ref[...]) * (X_HD ** -0.5)).astype(BF16)
        s = _dot_nt(qh, k_ref[0, :, sl])
        p = jnp.exp(s - jnp.max(s, axis=-1, keepdims=True))
        l = jnp.sum(p, axis=-1, keepdims=True)
        outs.append((_dot(p.astype(BF16), v_ref[0, :, sl]) / l).astype(BF16))
    o_ref[...] = x + _dot(jnp.concatenate(outs, axis=1), wo_ref[...])


def _xattn(x, g, wq, gq, km, vm, wo, layer, seq, tm):
    n = x.shape[0]
    nm = km.shape[1]
    per_b = seq // tm
    row = pl.BlockSpec((tm, D_MODEL), lambda i: (i, 0))
    kv = pl.BlockSpec((1, nm, D_MODEL), lambda i: (i // per_b, 0, 0))
    return pl.pallas_call(
        _xattn_kernel,
        grid=(n // tm,),
        in_specs=[row, _const_spec((1, D_MODEL)), _const_spec((D_MODEL, D_MODEL), layer=layer),
                  _const_spec((1, X_HD)), kv, kv, _const_spec((D_MODEL, D_MODEL), layer=layer)],
        out_specs=row,
        out_shape=jax.ShapeDtypeStruct((n, D_MODEL), F32),
        compiler_params=_cparams(("parallel",)),
        name="cross_attention",
    )(x, g, wq, gq, km, vm, wo)


FF_CHUNK = 256


def _ffn_kernel(x_ref, g_ref, wgu_ref, wd_ref, o_ref, act_ref):
    x = x_ref[...]
    h = _rms(x, g_ref[...]).astype(BF16)
    for c in range(0, D_FF, FF_CHUNK):
        a = _dot(h, wgu_ref[:, c:c + FF_CHUNK])
        b = _dot(h, wgu_ref[:, D_FF + c:D_FF + c + FF_CHUNK])
        act_ref[:, c:c + FF_CHUNK] = (a * _sigmoid(a) * b).astype(BF16)
    o_ref[...] = x + _dot(act_ref[...], wd_ref[...])


def _ffn(x, g, wgu, wd, layer, tm):
    n = x.shape[0]
    row = pl.BlockSpec((tm, D_MODEL), lambda i: (i, 0))
    return pl.pallas_call(
        _ffn_kernel,
        grid=(n // tm,),
        in_specs=[row, _const_spec((1, D_MODEL)), _const_spec((D_MODEL, 2 * D_FF), layer=layer),
                  _const_spec((D_FF, D_MODEL), layer=layer)],
        out_specs=row,
        out_shape=jax.ShapeDtypeStruct((n, D_MODEL), F32),
        scratch_shapes=[pltpu.VMEM((tm, D_FF), BF16)],
        compiler_params=_cparams(("parallel",)),
        name="swiglu",
    )(x, g, wgu, wd)


def _split_w_in(w_in):
    depth = w_in.shape[0]
    pad = jnp.zeros((depth, D_MODEL, A_END - A_FF - FOX_HEADS), w_in.dtype)
    w_fox = jnp.concatenate([w_in[:, :, :1544], pad], axis=-1).astype(BF16)
    return w_fox, w_in[:, :, 1544:].astype(BF16)


def _head_vec(v):
    return jnp.tile(v.astype(F32), FOX_HEADS).reshape(1, BRANCH_W)


def kernel(x, mem, norm_mix, w_in, fox_fbias, fox_qnorm, fox_knorm, s5_a_re, s5_a_im, s5_b_re, s5_b_im, s5_c_re, s5_c_im, s5_d, s5_log_dt, s5_w_glu, s5_b_glu, hg_lb, hg_onorm, w_branch, w_out, norm_x, norm_mem, xq, xk, xv, xo, x_qnorm, x_knorm, norm_ffn, w_gate_up, w_down):
    bsz, seq, _ = x.shape
    depth = w_in.shape[0]
    n = bsz * seq
    tm = min(1024, seq)
    tq = min(512, seq)
    tt = min(256, seq)
    s5_tt = min(2048, seq)
    assert seq % tm == 0 and seq % tq == 0 and seq % tt == 0 and tt % HG_CHUNK == 0
    assert seq % s5_tt == 0 and s5_tt % (8 * S5_T) == 0

    row = lambda v: v.astype(F32).reshape(1, -1)
    w_fox, w_rest = _split_w_in(w_in)
    s5_wglu = s5_w_glu.astype(BF16)
    wb, wo = w_branch.astype(BF16), w_out.astype(BF16)
    wq, wk, wv, wxo = xq.astype(BF16), xk.astype(BF16), xv.astype(BF16), xo.astype(BF16)
    wgu, wd = w_gate_up.astype(BF16), w_down.astype(BF16)

    lb_all = jnp.cumsum(jax.nn.softmax(hg_lb.astype(F32), axis=0), axis=0)
    lb_all = lb_all - lb_all[0:1]

    lane = jnp.arange(256)
    hsum = (lane[:, None] // FOX_HD == lane[None, :] // FOX_HD).astype(BF16)
    pad_lane = jnp.tile(jnp.arange(FOX_PAD), FOX_HEADS)
    k_lanes = (pad_lane >= FOX_HD) & (pad_lane < FOX_HD + FOX_BIAS_LANES)
    q_lanes = (pad_lane >= FOX_HD + FOX_BIAS_LANES) & (pad_lane < FOX_HD + 2 * FOX_BIAS_LANES)
    qone = k_lanes.astype(F32).reshape(1, FOX_QK_W)
    kone = q_lanes.astype(F32).reshape(1, FOX_QK_W)
    src = jnp.arange(128)
    dst = jnp.arange(2 * FOX_QK_W)
    src_n, src_h = src[:, None] // FOX_HEADS, src[:, None] % FOX_HEADS
    dst_key, dst_lane = dst[None, :] // FOX_QK_W, dst[None, :] % FOX_QK_W
    place = ((src_n < 2 * FOX_BIAS_LANES) & (dst_key == (src_n < FOX_BIAS_LANES))
             & (dst_lane // FOX_PAD == src_h) & (dst_lane % FOX_PAD == FOX_HD + src_n)).astype(BF16)

    col = jnp.arange(S5_KW)
    src = jnp.arange(S5_BLK)
    rep_th = ((src[:, None] // S5_GROUP == col[None, :] // S5_BLK)
              & (src[:, None] % S5_GROUP == col[None, :] % S5_GROUP)).astype(BF16)
    rep_rp = ((src[:, None] // S5_STATE == col[None, :] // S5_BST)
              & (src[:, None] % S5_STATE == col[None, :] % S5_STATE)).astype(BF16)

    s5_w, s5_bst, s5_cout, s5_pw = jax.vmap(_s5_params)(s5_a_re, s5_a_im, s5_b_re, s5_b_im,
                                                        s5_c_re, s5_c_im, s5_log_dt)

    xf = x.astype(F32).reshape(n, D_MODEL)
    for l in range(depth):
        gq = _head_vec(fox_qnorm[l]) * (FOX_HD ** -0.5 * LOG2E)
        gk = _head_vec(fox_knorm[l])
        fb = jnp.pad(fox_fbias[l].astype(F32), (0, 128 - FOX_HEADS)).reshape(1, 128)
        qk_bound = 1.01 * FOX_HD ** 0.5 * jnp.max(jnp.abs(fox_qnorm[l])) * jnp.max(jnp.abs(fox_knorm[l]))
        qk_bound2 = qk_bound * LOG2E
        qoff = jnp.full((1, 128), qk_bound2 - FOX_REF_MARGIN, F32)
        q, k, v, c = _inproj_fox(xf, row(norm_mix[l]), w_fox, l, gq, gk, qone, kone, hsum,
                                 fb, qoff, place, seq, tm)
        su, hq, hf, hi, hg, gl = _inproj_rest(xf, row(norm_mix[l]), w_rest, l, tm)

        jstart = _fox_first_blocks(c, qk_bound, bsz, seq, tq)
        bounded = (qk_bound2 <= FOX_BOUNDED_MAX).astype(jnp.int32).reshape(1)
        y_fox = _fox_attention(jstart, bounded, q, k, v, bsz, seq, tq)

        y_s5 = _s5(su, s5_w[l], s5_bst[l], s5_cout[l], rep_th, rep_rp, s5_pw[l], row(s5_d[l]), s5_wglu[l],
                   row(s5_b_glu[l]), bsz, seq, s5_tt)

        lb = lb_all[l].reshape(1, -1)
        y_hg = _hgrn(hq, hf, hi, hg, jnp.log(lb), jnp.log1p(-lb), 1.0 - lb, row(hg_onorm[l]), bsz, seq, tt)

        xf = _merge(xf, y_fox, y_s5, y_hg, gl, wb, wo, l, tm)

        km, vm = _memkv(mem.astype(F32), row(norm_mem[l]), wk[l], wv[l], row(x_knorm[l]))
        xf = _xattn(xf, row(norm_x[l]), wq, row(x_qnorm[l]), km, vm, wxo, l, seq, tm)
        xf = _ffn(xf, row(norm_ffn[l]), wgu, wd, l, tm)
    return xf.reshape(bsz, seq, D_MODEL).astype(x.dtype)
```

```python
import functools
import math

import jax
import jax.numpy as jnp
from jax import lax
from jax.experimental import pallas as pl
from jax.experimental.pallas import tpu as pltpu

F32 = jnp.float32
BF16 = jnp.bfloat16
HIGHEST = lax.Precision.HIGHEST

D_MODEL = 1024
BRANCH_W = 512
FOX_HD = 64
FOX_HEADS = 8
S5_GROUP = 16
S5_GROUPS = 32
S5_STATE = 64
S5_NSTATE = S5_GROUPS * S5_STATE
HG_HEADS = 4
HG_D = 128
HG_CHUNK = 64
HG_SUB = 8
X_HEADS = 4
X_HD = 256
D_FF = 2816
EPS = 1e-6

VMEM_LIMIT_BYTES = 56 * 1024 * 1024

FOX_PAD = 128
FOX_QK_W = FOX_HEADS * FOX_PAD
A_FQ, A_FK, A_FV, A_FF, A_END = 0, 512, 1024, 1536, 1664
B_SU, B_HQ, B_HF, B_HI, B_HG, B_GL, B_END = 0, 512, 1024, 1536, 2048, 2560, 5632
FOX_BIAS_LANES = 3
LOG2E = 1.4426950408889634
FOX_SKIP_NATS = 30.0
FOX_UNROLL = 2
FOX_REF_MARGIN = 60.0
FOX_BOUNDED_MAX = 60.0


NEG_BIG = -1e30


def _cparams(sem, vmem_limit_bytes=VMEM_LIMIT_BYTES):
    return pltpu.CompilerParams(dimension_semantics=sem, vmem_limit_bytes=vmem_limit_bytes)


def _rms(xf, g):
    return xf * lax.rsqrt(jnp.mean(xf * xf, axis=-1, keepdims=True) + EPS) * g


def _sigmoid(x):
    return 0.5 + 0.5 * jnp.tanh(0.5 * x)


def _log_sigmoid(x):
    return jnp.minimum(x, 0.0) - jnp.log(1.0 + jnp.exp(-jnp.abs(x)))


def _dot(a, b):
    return jnp.dot(a, b, preferred_element_type=F32)


def _dot_nt(a, b):
    return lax.dot_general(a, b, (((1,), (1,)), ((), ())), preferred_element_type=F32)


def _dot_tn(a, b):
    return lax.dot_general(a, b, (((0,), (0,)), ((), ())), preferred_element_type=F32)


def _const_spec(shape, single=False, layer=None):
    nd = len(shape)
    if layer is not None:
        return pl.BlockSpec((None,) + tuple(shape), lambda *_: (layer,) + (0,) * nd,
                            pipeline_mode=pl.Buffered(1))
    if single:
        return pl.BlockSpec(shape, lambda *_: (0,) * nd, pipeline_mode=pl.Buffered(1))
    return pl.BlockSpec(shape, lambda *_: (0,) * nd)


def _top16(v):
    bits = lax.bitcast_convert_type(v, jnp.uint32) & jnp.uint32(0xFFFF0000)
    return lax.bitcast_convert_type(bits, F32)


def _split3(v):
    hi = _top16(v)
    r1 = v - hi
    mid = _top16(r1)
    return [hi, mid, r1 - mid]


CUM_ROWS = 256


def _inproj_fox_kernel(x_ref, g_ref, w_ref, gq_ref, gk_ref, qone_ref, kone_ref, hsum_ref, fb_ref, qoff_ref,
                       place_ref, q_ref, k_ref, v_ref, c_ref, carry_ref, *, tiles_per_seq):
    tm = x_ref.shape[0]

    @pl.when(pl.program_id(0) % tiles_per_seq == 0)
    def _():
        carry_ref[...] = jnp.zeros(carry_ref.shape, F32)

    h = _rms(x_ref[...], g_ref[...]).astype(BF16)

    def proj(lo, hi):
        return _dot(h, w_ref[:, lo:hi])

    def sumsq(t):
        return _dot((t * t).astype(BF16), hsum_ref[...])

    def headnorm(t, ss, g):
        return t * lax.rsqrt(ss * (1.0 / FOX_HD) + EPS) * g

    lf = _log_sigmoid(proj(A_FF, A_END) + fb_ref[...])
    r = lax.broadcasted_iota(jnp.int32, (CUM_ROWS, CUM_ROWS), 0)
    cc = lax.broadcasted_iota(jnp.int32, (CUM_ROWS, CUM_ROWS), 1)
    lower = (cc <= r).astype(F32)
    carry = carry_ref[...]
    parts = []
    for r0 in range(0, tm, CUM_ROWS):
        part = jnp.dot(lower, lf[r0:r0 + CUM_ROWS], precision=HIGHEST, preferred_element_type=F32) + carry
        carry = part[CUM_ROWS - 1:CUM_ROWS, :]
        parts.append(part)
    carry_ref[...] = carry
    cs = jnp.concatenate(parts, axis=0)
    c_ref[...] = cs
    d = -(cs * LOG2E)
    lane = lax.broadcasted_iota(jnp.int32, (tm, 128), 1)
    packed = None
    for n, piece in enumerate(_split3(d) + _split3(-d - qoff_ref[...])):
        term = jnp.where(lane < FOX_HEADS, piece, 0.0)
        term = term if n == 0 else pltpu.roll(term, FOX_HEADS * n, 1)
        packed = term if packed is None else packed + term
    bias = _dot(packed.astype(BF16), place_ref[...])
    qbias, kbias = bias[:, :FOX_QK_W], bias[:, FOX_QK_W:]

    chunks = range(0, BRANCH_W, 256)
    tq = [proj(A_FQ + c, A_FQ + c + 256) for c in chunks]
    tk = [proj(A_FK + c, A_FK + c + 256) for c in chunks]
    sq = [sumsq(t) for t in tq]
    sk = [sumsq(t) for t in tk]

    def store_padded(dst_ref, t, extra_ref, bias_vals, blk):
        for odd in range(2):
            head = 2 * blk + odd
            sl = slice(head * FOX_PAD, (head + 1) * FOX_PAD)
            src = pltpu.roll(t, FOX_HD, 1) if odd else t
            dst_ref[:, sl] = (jnp.where(lane < FOX_HD, src, 0.0) + extra_ref[:, sl] + bias_vals[:, sl]).astype(BF16)

    for n, c in enumerate(chunks):
        qn = headnorm(tq[n], sq[n], gq_ref[:, c:c + 256])
        kn = headnorm(tk[n], sk[n], gk_ref[:, c:c + 256])
        for half in range(2):
            blk = 2 * n + half
            store_padded(q_ref, qn[:, half * 128:(half + 1) * 128], qone_ref, qbias, blk)
            store_padded(k_ref, kn[:, half * 128:(half + 1) * 128], kone_ref, kbias, blk)
    v_ref[...] = proj(A_FV, A_FF).astype(BF16)


def _inproj_fox(x, g, w, layer, gq, gk, qone, kone, hsum, fb, qoff, place, seq, tm):
    n = x.shape[0]
    row = lambda c: pl.BlockSpec((tm, c), lambda i: (i, 0))
    outs = [(FOX_QK_W, BF16)] * 2 + [(512, BF16), (128, F32)]
    vec = _const_spec((1, FOX_QK_W))
    gain = _const_spec((1, BRANCH_W))
    return pl.pallas_call(
        functools.partial(_inproj_fox_kernel, tiles_per_seq=seq // tm),
        grid=(n // tm,),
        in_specs=[row(D_MODEL), _const_spec((1, D_MODEL)), _const_spec((D_MODEL, A_END), layer=layer),
                  gain, gain, vec, vec, _const_spec((256, 256)), _const_spec((1, 128)), _const_spec((1, 128)),
                  _const_spec((128, 2 * FOX_QK_W))],
        out_specs=[row(c) for c, _ in outs],
        out_shape=[jax.ShapeDtypeStruct((n, c), dt) for c, dt in outs],
        scratch_shapes=[pltpu.VMEM((1, 128), F32)],
        compiler_params=_cparams(("arbitrary",)),
        name="inproj_fox",
    )(x, g, w, gq, gk, qone, kone, hsum, fb, qoff, place)


def _inproj_rest_kernel(x_ref, g_ref, w_ref, su_ref, hq_ref, hf_ref, hi_ref, hg_ref, gl_ref):
    h = _rms(x_ref[...], g_ref[...]).astype(BF16)

    def proj(lo, hi):
        return _dot(h, w_ref[:, lo:hi])

    su_ref[...] = proj(B_SU, B_HQ).astype(BF16)
    hq_ref[...] = proj(B_HQ, B_HF).astype(BF16)
    hf_ref[...] = proj(B_HF, B_HI)
    hi_ref[...] = proj(B_HI, B_HG).astype(BF16)
    hg_ref[...] = proj(B_HG, B_GL).astype(BF16)
    for c in range(B_GL, B_END, 512):
        gl_ref[:, c - B_GL:c - B_GL + 512] = proj(c, c + 512).astype(BF16)


def _inproj_rest(x, g, w, layer, tm):
    n = x.shape[0]
    row = lambda c: pl.BlockSpec((tm, c), lambda i: (i, 0))
    outs = [(512, BF16)] * 2 + [(512, F32)] + [(512, BF16)] * 2 + [(3072, BF16)]
    return pl.pallas_call(
        _inproj_rest_kernel,
        grid=(n // tm,),
        in_specs=[row(D_MODEL), _const_spec((1, D_MODEL)), _const_spec((D_MODEL, B_END), layer=layer)],
        out_specs=[row(c) for c, _ in outs],
        out_shape=[jax.ShapeDtypeStruct((n, c), dt) for c, dt in outs],
        compiler_params=_cparams(("parallel",)),
        name="inproj_rest",
    )(x, g, w)


def _fox_kernel(jstart_ref, bounded_ref, q_ref, k_ref, v_ref, o_ref, m_ref, l_ref, acc_ref, *, tq, nq):
    b, p, i = pl.program_id(0), pl.program_id(1), pl.program_id(2)
    nchunk = tq // 128
    l_ref[...] = jnp.zeros(l_ref.shape, F32)
    acc_ref[...] = jnp.zeros(acc_ref.shape, F32)

    def logits(j, hh, masked):
        start = pl.multiple_of(j * tq, tq)
        sl = slice(hh * FOX_PAD, (hh + 1) * FOX_PAD)
        s = _dot_nt(q_ref[:, sl], k_ref[pl.ds(start, tq), sl])
        if masked:
            rq = lax.broadcasted_iota(jnp.int32, (tq, tq), 0)
            ck = lax.broadcasted_iota(jnp.int32, (tq, tq), 1)
            s = jnp.where(ck <= rq, s, NEG_BIG)
        return s, v_ref[pl.ds(start, tq), :]

    def lane_partial_sum(pr):
        ls = pr[:, 0:128]
        for c in range(1, nchunk):
            ls = ls + pr[:, c * 128:(c + 1) * 128]
        return ls

    def step_bounded(j, masked, heads=(0, 1)):
        for hh in heads:
            s, vb = logits(j, hh, masked)
            pr = jnp.exp2(s)
            l_ref[hh] += lane_partial_sum(pr)
            acc_ref[hh] += _dot(pr.astype(BF16), vb)

    def step_online(j, masked, heads=(0, 1)):
        for hh in heads:
            s, vb = logits(j, hh, masked)
            mc = s[:, 0:128]
            for c in range(1, nchunk):
                mc = jnp.maximum(mc, s[:, c * 128:(c + 1) * 128])
            m_prev = m_ref[hh]
            m_new = jnp.maximum(m_prev, jnp.max(mc, axis=-1, keepdims=True))
            alpha = jnp.exp2(m_prev - m_new)
            pr = jnp.exp2(s - jnp.concatenate([m_new] * nchunk, axis=1))
            l_ref[hh] = alpha * l_ref[hh] + lane_partial_sum(pr)
            acc_ref[hh] = alpha * acc_ref[hh] + _dot(pr.astype(BF16), vb)
            m_ref[hh] = m_new

    base = ((b * (FOX_HEADS // 2) + p) * 2) * nq + i
    first = (jstart_ref[base], jstart_ref[base + nq])
    j0 = jnp.maximum(first[0], first[1])
    count = i - j0

    def sweep(step):
        for hh in range(2):
            def only(n, carry, hh=hh):
                step(first[hh] + n, False, (hh,))
                return carry

            lax.fori_loop(0, j0 - first[hh], only, 0)

        def body(n, carry):
            for u in range(FOX_UNROLL):
                step(j0 + FOX_UNROLL * n + u, False)
            return carry

        def single(n, carry):
            step(i - 1 - n, False)
            return carry

        lax.fori_loop(0, count // FOX_UNROLL, body, 0)
        lax.fori_loop(0, count % FOX_UNROLL, single, 0)
        step(i, True)

    bounded = bounded_ref[0] == 1

    @pl.when(bounded)
    def _():
        sweep(step_bounded)

    @pl.when(jnp.logical_not(bounded))
    def _():
        m_ref[...] = jnp.full(m_ref.shape, NEG_BIG, F32)
        sweep(step_online)

    lane = lax.broadcasted_iota(jnp.int32, (tq, 128), 1)
    o0 = acc_ref[0] / jnp.sum(l_ref[0], axis=-1, keepdims=True)
    o1 = acc_ref[1] / jnp.sum(l_ref[1], axis=-1, keepdims=True)
    o_ref[...] = jnp.where(lane < FOX_HD, o0, o1).astype(BF16)


def _fox_attention(jstart, bounded, q, k, v, bsz, seq, tq):
    n = q.shape[0]
    nq = seq // tq
    grid_spec = pltpu.PrefetchScalarGridSpec(
        num_scalar_prefetch=2,
        grid=(bsz, FOX_HEADS // 2, nq),
        in_specs=[pl.BlockSpec((tq, 2 * FOX_PAD), lambda b, p, i, js, bd: (b * nq + i, p)),
                  pl.BlockSpec((seq, 2 * FOX_PAD), lambda b, p, i, js, bd: (b, p)),
                  pl.BlockSpec((seq, 128), lambda b, p, i, js, bd: (b, p))],
        out_specs=pl.BlockSpec((tq, 128), lambda b, p, i, js, bd: (b * nq + i, p)),
        scratch_shapes=[pltpu.VMEM((2, tq, 128), F32), pltpu.VMEM((2, tq, 128), F32),
                        pltpu.VMEM((2, tq, 128), F32)],
    )
    return pl.pallas_call(
        functools.partial(_fox_kernel, tq=tq, nq=nq),
        grid_spec=grid_spec,
        out_shape=jax.ShapeDtypeStruct((n, 512), BF16),
        compiler_params=_cparams(("parallel", "parallel", "arbitrary")),
        name="fox_attention",
    )(jstart, bounded, q, k, v)


def _fox_first_blocks(c, qk_bound, bsz, seq, tq):
    nq = seq // tq
    cb = c.reshape(bsz, nq, tq, 128)
    c_end = cb[:, :, tq - 1, :FOX_HEADS].transpose(0, 2, 1)
    c_start = cb[:, :, 0, :FOX_HEADS].transpose(0, 2, 1)
    gap = c_end[:, :, None, :] - c_start[:, :, :, None]
    skip = gap > (2.0 * qk_bound + FOX_SKIP_NATS)
    jstart = jnp.sum(skip.astype(jnp.int32), axis=-1)
    jstart = jnp.minimum(jstart, jnp.arange(nq, dtype=jnp.int32)[None, None, :])
    return jstart.reshape(-1)


S5_T = 8
S5_BLK = 128
S5_NBLK = BRANCH_W // S5_BLK
S5_KW = S5_T * S5_BLK
S5_BST = S5_NSTATE // S5_NBLK
S5_VMEM_LIMIT_BYTES = 58 * 1024 * 1024


def _s5_expand(src_ref, rep_ref, dst_ref, row_div, col_div):
    n = S5_KW
    for j in range(S5_NBLK):
        for c in range(0, n, 256):
            rg = (lax.broadcasted_iota(jnp.int32, (n, 256), 0) // row_div) % 8
            cg = ((lax.broadcasted_iota(jnp.int32, (n, 256), 1) + c) // col_div) % 8
            rep = _dot(src_ref[j], rep_ref[:, c:c + 256])
            dst_ref[j, :, c:c + 256] = jnp.where(rg == cg, rep, 0.0).astype(BF16)


def _s5_kernel(u_ref, wc_ref, bc_ref, cc_ref, rep_th_ref, rep_rp_ref, pw_ref, d_ref, wglu_ref, bglu_ref,
               o_ref, w_ref, bst_ref, cout_ref, uf_ref, y_ref, hr_ref, hi_ref, cr_ref, ci_ref, *, tt):
    t = pl.program_id(1)
    rows = tt // S5_T

    @pl.when(t == 0)
    def _():
        cr_ref[...] = jnp.zeros(cr_ref.shape, F32)
        ci_ref[...] = jnp.zeros(ci_ref.shape, F32)
        _s5_expand(wc_ref, rep_th_ref, w_ref, S5_GROUP, S5_GROUP)
        _s5_expand(bc_ref, rep_rp_ref, bst_ref, S5_GROUP, S5_STATE)
        _s5_expand(cc_ref, rep_th_ref, cout_ref, S5_STATE, S5_GROUP)

    for j in range(S5_NBLK):
        uf_ref[j] = u_ref[:, j * S5_BLK:(j + 1) * S5_BLK].astype(F32)

    def chunk_rows(j):
        return jnp.concatenate([uf_ref[j, pl.ds(s, rows, stride=S5_T), :] for s in range(S5_T)],
                               axis=1).astype(BF16)

    for j in range(S5_NBLK):
        hin = _dot(chunk_rows(j), bst_ref[j])
        hr_ref[:, j * S5_BST:(j + 1) * S5_BST] = hin[:, :S5_BST]
        hi_ref[:, j * S5_BST:(j + 1) * S5_BST] = hin[:, S5_BST:]

    row8 = lax.broadcasted_iota(jnp.int32, (8, S5_NSTATE), 0)

    def scan_block(r, carry):
        cr, ci = carry
        rws = pl.ds(pl.multiple_of(r * 8, 8), 8)
        xr = hr_ref[rws, :]
        xi = hi_ref[rws, :]
        for n, k in enumerate((1, 2, 4)):
            ar = pw_ref[2 * n]
            ai = pw_ref[2 * n + 1]
            sr = pltpu.roll(xr, k, 0)
            si = pltpu.roll(xi, k, 0)
            xr, xi = xr + ar * sr - ai * si, xi + ar * si + ai * sr
        pr = pw_ref[6]
        pi = pw_ref[7]
        xr, xi = xr + pr * cr - pi * ci, xi + pr * ci + pi * cr
        hr_ref[rws, :] = jnp.where(row8 == 0, cr, pltpu.roll(xr, 1, 0))
        hi_ref[rws, :] = jnp.where(row8 == 0, ci, pltpu.roll(xi, 1, 0))
        return (jnp.broadcast_to(xr[7:8, :], xr.shape), jnp.broadcast_to(xi[7:8, :], xi.shape))

    cr, ci = lax.fori_loop(0, rows // 8, scan_block, (cr_ref[...], ci_ref[...]))
    cr_ref[...] = cr
    ci_ref[...] = ci

    for j in range(S5_NBLK):
        st = slice(j * S5_BST, (j + 1) * S5_BST)
        hb = jnp.concatenate([hr_ref[:, st], hi_ref[:, st]], axis=1).astype(BF16)
        yj = _dot(chunk_rows(j), w_ref[j]) + _dot(hb, cout_ref[j])
        for s in range(S5_T):
            y_ref[j, pl.ds(s, rows, stride=S5_T), :] = yj[:, s * S5_BLK:(s + 1) * S5_BLK]

    y = jnp.concatenate([y_ref[j] + d_ref[:, j * S5_BLK:(j + 1) * S5_BLK] * uf_ref[j]
                         for j in range(S5_NBLK)], axis=1)
    z = 0.5 * y * (1.0 + jnp.tanh(math.sqrt(2.0 / math.pi) * (y + 0.044715 * (y * y * y))))
    gate = _sigmoid(_dot(z.astype(BF16), wglu_ref[...]) + bglu_ref[...])
    o_ref[...] = (z * gate).astype(BF16)


def _s5(u, wc, bc, cc, rep_th, rep_rp, pw, d, wglu, bglu, bsz, seq, tt):
    n = u.shape[0]
    nt = seq // tt
    rowspec = pl.BlockSpec((tt, BRANCH_W), lambda b, t: (b * nt + t, 0))
    compact = _const_spec((S5_NBLK, S5_KW, S5_BLK), single=True)
    expanded = pltpu.VMEM((S5_NBLK, S5_KW, S5_KW), BF16)
    return pl.pallas_call(
        functools.partial(_s5_kernel, tt=tt),
        grid=(bsz, nt),
        in_specs=[rowspec, compact, compact, compact,
                  _const_spec((S5_BLK, S5_KW), single=True), _const_spec((S5_BLK, S5_KW), single=True),
                  _const_spec((8, 8, S5_NSTATE), single=True), _const_spec((1, BRANCH_W)),
                  _const_spec((BRANCH_W, BRANCH_W)), _const_spec((1, BRANCH_W))],
        out_specs=rowspec,
        out_shape=jax.ShapeDtypeStruct((n, BRANCH_W), BF16),
        scratch_shapes=[expanded, expanded, expanded,
                        pltpu.VMEM((S5_NBLK, tt, S5_BLK), F32), pltpu.VMEM((S5_NBLK, tt, S5_BLK), F32),
                        pltpu.VMEM((tt // S5_T, S5_NSTATE), F32), pltpu.VMEM((tt // S5_T, S5_NSTATE), F32),
                        pltpu.VMEM((8, S5_NSTATE), F32), pltpu.VMEM((8, S5_NSTATE), F32)],
        compiler_params=_cparams(("parallel", "arbitrary"), S5_VMEM_LIMIT_BYTES),
        name="s5_mixer",
    )(u, wc, bc, cc, rep_th, rep_rp, pw, d, wglu, bglu)


def _s5_params(a_re, a_im, b_re, b_im, c_re, c_im, log_dt):
    g, p, gs, tt = S5_GROUPS, S5_STATE, S5_GROUP, S5_T
    dt = jnp.exp(log_dt.astype(F32))[:, None]
    ar, ai = a_re.astype(F32), a_im.astype(F32)
    mag = jnp.exp(dt * ar)
    abar_r, abar_i = mag * jnp.cos(dt * ai), mag * jnp.sin(dt * ai)
    inv_den = 1.0 / (ar * ar + ai * ai)
    nr, ni = abar_r - 1.0, abar_i
    coef_r = (nr * ar + ni * ai) * inv_den
    coef_i = (ni * ar - nr * ai) * inv_den
    br, bi = b_re.astype(F32), b_im.astype(F32)
    bbar_r = coef_r[..., None] * br - coef_i[..., None] * bi
    bbar_i = coef_r[..., None] * bi + coef_i[..., None] * br
    cr, ci = c_re.astype(F32), c_im.astype(F32)

    pows = [(jnp.ones_like(abar_r), jnp.zeros_like(abar_i))]
    for _ in range(tt):
        pr, pi = pows[-1]
        pows.append((pr * abar_r - pi * abar_i, pr * abar_i + pi * abar_r))
    pw_r = jnp.stack([x[0] for x in pows])
    pw_i = jnp.stack([x[1] for x in pows])

    bt_r, bt_i = bbar_r.transpose(0, 2, 1), bbar_i.transpose(0, 2, 1)
    cb_r = cr[:, :, None, :] * bt_r[:, None, :, :] - ci[:, :, None, :] * bt_i[:, None, :, :]
    cb_i = cr[:, :, None, :] * bt_i[:, None, :, :] + ci[:, :, None, :] * bt_r[:, None, :, :]
    kl = jnp.sum(pw_r[:tt, :, None, None, :] * cb_r[None] - pw_i[:tt, :, None, None, :] * cb_i[None],
                 axis=-1)
    nb, gl = S5_NBLK, g // S5_NBLK

    lag = jnp.arange(tt)[None, :] - jnp.arange(tt)[:, None]
    m = jnp.where((lag >= 0)[:, :, None, None, None], kl[jnp.clip(lag, 0, tt - 1)], 0.0)
    m = m.transpose(2, 0, 4, 1, 3).reshape(nb, gl, tt, gs, tt * gs)
    w = m.transpose(0, 2, 1, 3, 4).reshape(nb, S5_KW, S5_BLK)

    e_r, e_i = pw_r[:tt][::-1], pw_i[:tt][::-1]
    bs_r = e_r[:, :, None, :] * bt_r[None] - e_i[:, :, None, :] * bt_i[None]
    bs_i = e_r[:, :, None, :] * bt_i[None] + e_i[:, :, None, :] * bt_r[None]
    bs = jnp.stack([bs_r, bs_i]).transpose(2, 1, 3, 0, 4).reshape(nb, gl, tt, gs, 2 * p)
    bst = bs.transpose(0, 2, 1, 3, 4).reshape(nb, S5_KW, S5_BLK)

    q_r, q_i = pw_r[1:tt + 1], pw_i[1:tt + 1]
    co_r = cr[None] * q_r[:, :, None, :] - ci[None] * q_i[:, :, None, :]
    co_i = cr[None] * q_i[:, :, None, :] + ci[None] * q_r[:, :, None, :]
    co = jnp.stack([co_r, -co_i]).transpose(2, 0, 4, 1, 3).reshape(nb, gl, 2, p, tt * gs)
    cout = co.transpose(0, 2, 1, 3, 4).reshape(nb, S5_KW, S5_BLK)

    ar1, ai1 = pw_r[tt].reshape(-1), pw_i[tt].reshape(-1)
    apow = [(ar1, ai1)]
    for _ in range(7):
        pr, pi = apow[-1]
        apow.append((pr * ar1 - pi * ai1, pr * ai1 + pi * ar1))
    rows8 = jnp.arange(8)[:, None]
    tabs = []
    for k in (1, 2, 4):
        mask = (rows8 >= k).astype(F32)
        tabs += [mask * apow[k - 1][0][None, :], mask * apow[k - 1][1][None, :]]
    tabs.append(jnp.stack([apow[r][0] for r in range(8)]))
    tabs.append(jnp.stack([apow[r][1] for r in range(8)]))
    pw = jnp.stack(tabs)
    return w.astype(BF16), bst.astype(BF16), cout.astype(BF16), pw


def _hgrn_kernel(q_ref, f_ref, i_ref, g_ref, loglb_ref, log1mlb_ref, onemlb_ref, gain_ref,
                 o_ref, st_ref, b_ref, key_ref, qf_ref, oacc_ref, *, tt):
    t = pl.program_id(1)

    @pl.when(t == 0)
    def _():
        st_ref[...] = jnp.zeros(st_ref.shape, F32)

    c_sz, sub = HG_CHUNK, HG_SUB
    n_sub = c_sz // sub
    row_s = lax.broadcasted_iota(jnp.int32, (sub, 1), 0)
    lane_s = lax.broadcasted_iota(jnp.int32, (sub, c_sz), 1)

    z = f_ref[...]
    loglb = loglb_ref[...]
    bb = log1mlb_ref[...] + _log_sigmoid(z)
    logf = jnp.maximum(loglb, bb) + jnp.log(1.0 + jnp.exp(-jnp.abs(loglb - bb)))
    key_ref[...] = onemlb_ref[...] * _sigmoid(-z)
    qx = q_ref[...].astype(F32)
    qf_ref[...] = qx * _sigmoid(qx)
    rr = lax.broadcasted_iota(jnp.int32, (tt, tt), 0)
    cc = lax.broadcasted_iota(jnp.int32, (tt, tt), 1)
    lower = ((cc <= rr) & (rr // c_sz == cc // c_sz)).astype(F32)
    b_ref[...] = jnp.dot(lower, logf * LOG2E, precision=HIGHEST, preferred_element_type=F32)

    def head_chunk(rows, hd):
        sl = slice(hd * HG_D, (hd + 1) * HG_D)
        key = key_ref[rows, sl]
        qf = qf_ref[rows, sl]
        vb = i_ref[rows, sl]
        b = b_ref[rows, sl]
        b_last = b[c_sz - 1:c_sz, :]
        st = st_ref[hd]
        o_inter = _dot_nt((qf * jnp.exp2(b)).astype(BF16), st.astype(BF16))
        kd = (key * jnp.exp2(b_last - b)).astype(BF16)
        st_ref[hd] = st * jnp.exp2(b_last) + _dot_tn(vb, kd)

        srows = []
        for blk in range(n_sub):
            lo = blk * sub
            b_i = b[lo:lo + sub]
            q_i = qf[lo:lo + sub]
            k_i = key[lo:lo + sub]
            sd = jnp.zeros((sub, c_sz), F32)
            for s in range(sub):
                e = jnp.exp2(b_i - b_i[s:s + 1])
                col = jnp.sum(q_i * e * k_i[s:s + 1], axis=-1, keepdims=True)
                sd = jnp.where(lane_s == lo + s, col, sd)
            sd = jnp.where(lane_s <= lo + row_s, sd, 0.0)
            if blk > 0:
                ref = b[lo - 1:lo]
                qt = (q_i * jnp.exp2(b_i - ref)).astype(BF16)
                kt = jnp.concatenate([key[:lo] * jnp.exp2(ref - b[:lo]), jnp.zeros((c_sz - lo, HG_D), F32)],
                                     axis=0).astype(BF16)
                sd = jnp.where(lane_s < lo, _dot_nt(qt, kt), sd)
            srows.append(sd)
        scores = jnp.concatenate(srows, axis=0)
        oacc_ref[rows, sl] = o_inter + _dot(scores.astype(BF16), vb)

    def chunk(c, carry):
        rows = pl.ds(pl.multiple_of(c * c_sz, c_sz), c_sz)
        for hd in range(HG_HEADS):
            head_chunk(rows, hd)
        return carry

    lax.fori_loop(0, tt // c_sz, chunk, 0, unroll=True)

    for hd in range(HG_HEADS):
        sl = slice(hd * HG_D, (hd + 1) * HG_D)
        o = oacc_ref[:, sl]
        gx = g_ref[:, sl].astype(F32)
        y = o * lax.rsqrt(jnp.mean(o * o, axis=-1, keepdims=True) + EPS) * gain_ref[...]
        o_ref[:, sl] = (y * (gx * _sigmoid(gx))).astype(BF16)


def _hgrn(hq, hf, hi, hg, loglb, log1mlb, onemlb, gain, bsz, seq, tt):
    n = hq.shape[0]
    nt = seq // tt
    spec = pl.BlockSpec((tt, 512), lambda b, t: (b * nt + t, 0))
    return pl.pallas_call(
        functools.partial(_hgrn_kernel, tt=tt),
        grid=(bsz, nt),
        in_specs=[spec, spec, spec, spec, _const_spec((1, 512)), _const_spec((1, 512)),
                  _const_spec((1, 512)), _const_spec((1, HG_D))],
        out_specs=spec,
        out_shape=jax.ShapeDtypeStruct((n, 512), BF16),
        scratch_shapes=[pltpu.VMEM((HG_HEADS, HG_D, HG_D), F32)] + [pltpu.VMEM((tt, 512), F32)] * 4,
        compiler_params=_cparams(("parallel", "arbitrary")),
        name="hgrn2_mixer",
    )(hq, hf, hi, hg, loglb, log1mlb, onemlb, gain)


def _merge_kernel(x_ref, yf_ref, ys_ref, yh_ref, gl_ref, wb_ref, wo_ref, o_ref):
    m = None
    for n, y_ref in enumerate((yf_ref, ys_ref, yh_ref)):
        gate = _sigmoid(gl_ref[:, n * D_MODEL:(n + 1) * D_MODEL].astype(F32))
        term = gate * _dot(y_ref[...], wb_ref[n * BRANCH_W:(n + 1) * BRANCH_W, :])
        m = term if m is None else m + term
    o_ref[...] = x_ref[...] + _dot(m.astype(BF16), wo_ref[...])


def _merge(x, yf, ys, yh, gl, wb, wo, layer, tm):
    n = x.shape[0]
    row = lambda c: pl.BlockSpec((tm, c), lambda i: (i, 0))
    return pl.pallas_call(
        _merge_kernel,
        grid=(n // tm,),
        in_specs=[row(D_MODEL), row(512), row(512), row(512), row(3 * D_MODEL),
                  _const_spec((3 * BRANCH_W, D_MODEL), layer=layer),
                  _const_spec((D_MODEL, D_MODEL), layer=layer)],
        out_specs=row(D_MODEL),
        out_shape=jax.ShapeDtypeStruct((n, D_MODEL), F32),
        compiler_params=_cparams(("parallel",)),
        name="merge_outproj",
    )(x, yf, ys, yh, gl, wb, wo)


def _memkv_kernel(m_ref, g_ref, wk_ref, wv_ref, gk_ref, k_ref, v_ref):
    h = _rms(m_ref[0], g_ref[...]).astype(BF16)
    kk = _dot(h, wk_ref[...])
    for hd in range(X_HEADS):
        sl = slice(hd * X_HD, (hd + 1) * X_HD)
        k_ref[0, :, sl] = _rms(kk[:, sl], gk_ref[...]).astype(BF16)
    v_ref[0] = _dot(h, wv_ref[...]).astype(BF16)


def _memkv(mem, g, wk, wv, gk):
    bsz, nm, _ = mem.shape
    spec = pl.BlockSpec((1, nm, D_MODEL), lambda b: (b, 0, 0))
    return pl.pallas_call(
        _memkv_kernel,
        grid=(bsz,),
        in_specs=[spec, _const_spec((1, D_MODEL)), _const_spec((D_MODEL, D_MODEL)),
                  _const_spec((D_MODEL, D_MODEL)), _const_spec((1, X_HD))],
        out_specs=[spec, spec],
        out_shape=[jax.ShapeDtypeStruct(mem.shape, BF16)] * 2,
        compiler_params=_cparams(("parallel",)),
        name="mem_kv",
    )(mem, g, wk, wv, gk)


def _xattn_kernel(x_ref, g_ref, wq_ref, gq_ref, k_ref, v_ref, wo_ref, o_ref):
    x = x_ref[...]
    h = _rms(x, g_ref[...]).astype(BF16)
    q = _dot(h, wq_ref[...])
    outs = []
    for hd in range(X_HEADS):
        sl = slice(hd * X_HD, (hd + 1) * X_HD)
        qh = (_rms(q[:, sl], gq_ref[...]) * (X_HD ** -0.5)).astype(BF16)
        s = _dot_nt(qh, k_ref[0, :, sl])
        p = jnp.exp(s - jnp.max(s, axis=-1, keepdims=True))
        l = jnp.sum(p, axis=-1, keepdims=True)
        outs.append((_dot(p.astype(BF16), v_ref[0, :, sl]) / l).astype(BF16))
    o_ref[...] = x + _dot(jnp.concatenate(outs, axis=1), wo_ref[...])


def _xattn(x, g, wq, gq, km, vm, wo, layer, seq, tm):
    n = x.shape[0]
    nm = km.shape[1]
    per_b = seq // tm
    row = pl.BlockSpec((tm, D_MODEL), lambda i: (i, 0))
    kv = pl.BlockSpec((1, nm, D_MODEL), lambda i: (i // per_b, 0, 0))
    return pl.pallas_call(
        _xattn_kernel,
        grid=(n // tm,),
        in_specs=[row, _const_spec((1, D_MODEL)), _const_spec((D_MODEL, D_MODEL), layer=layer),
                  _const_spec((1, X_HD)), kv, kv, _const_spec((D_MODEL, D_MODEL), layer=layer)],
        out_specs=row,
        out_shape=jax.ShapeDtypeStruct((n, D_MODEL), F32),
        compiler_params=_cparams(("parallel",)),
        name="cross_attention",
    )(x, g, wq, gq, km, vm, wo)


FF_CHUNK = 256


def _ffn_kernel(x_ref, g_ref, wgu_ref, wd_ref, o_ref, act_ref):
    x = x_ref[...]
    h = _rms(x, g_ref[...]).astype(BF16)
    for c in range(0, D_FF, FF_CHUNK):
        a = _dot(h, wgu_ref[:, c:c + FF_CHUNK])
        b = _dot(h, wgu_ref[:, D_FF + c:D_FF + c + FF_CHUNK])
        act_ref[:, c:c + FF_CHUNK] = (a * _sigmoid(a) * b).astype(BF16)
    o_ref[...] = x + _dot(act_ref[...], wd_ref[...])


def _ffn(x, g, wgu, wd, layer, tm):
    n = x.shape[0]
    row = pl.BlockSpec((tm, D_MODEL), lambda i: (i, 0))
    return pl.pallas_call(
        _ffn_kernel,
        grid=(n // tm,),
        in_specs=[row, _const_spec((1, D_MODEL)), _const_spec((D_MODEL, 2 * D_FF), layer=layer),
                  _const_spec((D_FF, D_MODEL), layer=layer)],
        out_specs=row,
        out_shape=jax.ShapeDtypeStruct((n, D_MODEL), F32),
        scratch_shapes=[pltpu.VMEM((tm, D_FF), BF16)],
        compiler_params=_cparams(("parallel",)),
        name="swiglu",
    )(x, g, wgu, wd)


def _split_w_in(w_in):
    depth = w_in.shape[0]
    pad = jnp.zeros((depth, D_MODEL, A_END - A_FF - FOX_HEADS), w_in.dtype)
    w_fox = jnp.concatenate([w_in[:, :, :1544], pad], axis=-1).astype(BF16)
    return w_fox, w_in[:, :, 1544:].astype(BF16)


def _head_vec(v):
    return jnp.tile(v.astype(F32), FOX_HEADS).reshape(1, BRANCH_W)


def kernel(x, mem, norm_mix, w_in, fox_fbias, fox_qnorm, fox_knorm, s5_a_re, s5_a_im, s5_b_re, s5_b_im, s5_c_re, s5_c_im, s5_d, s5_log_dt, s5_w_glu, s5_b_glu, hg_lb, hg_onorm, w_branch, w_out, norm_x, norm_mem, xq, xk, xv, xo, x_qnorm, x_knorm, norm_ffn, w_gate_up, w_down):
    bsz, seq, _ = x.shape
    depth = w_in.shape[0]
    n = bsz * seq
    tm = min(1024, seq)
    tq = min(512, seq)
    tt = min(256, seq)
    s5_tt = min(2048, seq)
    assert seq % tm == 0 and seq % tq == 0 and seq % tt == 0 and tt % HG_CHUNK == 0
    assert seq % s5_tt == 0 and s5_tt % (8 * S5_T) == 0

    row = lambda v: v.astype(F32).reshape(1, -1)
    w_fox, w_rest = _split_w_in(w_in)
    s5_wglu = s5_w_glu.astype(BF16)
    wb, wo = w_branch.astype(BF16), w_out.astype(BF16)
    wq, wk, wv, wxo = xq.astype(BF16), xk.astype(BF16), xv.astype(BF16), xo.astype(BF16)
    wgu, wd = w_gate_up.astype(BF16), w_down.astype(BF16)

    lb_all = jnp.cumsum(jax.nn.softmax(hg_lb.astype(F32), axis=0), axis=0)
    lb_all = lb_all - lb_all[0:1]

    lane = jnp.arange(256)
    hsum = (lane[:, None] // FOX_HD == lane[None, :] // FOX_HD).astype(BF16)
    pad_lane = jnp.tile(jnp.arange(FOX_PAD), FOX_HEADS)
    k_lanes = (pad_lane >= FOX_HD) & (pad_lane < FOX_HD + FOX_BIAS_LANES)
    q_lanes = (pad_lane >= FOX_HD + FOX_BIAS_LANES) & (pad_lane < FOX_HD + 2 * FOX_BIAS_LANES)
    qone = k_lanes.astype(F32).reshape(1, FOX_QK_W)
    kone = q_lanes.astype(F32).reshape(1, FOX_QK_W)
    src = jnp.arange(128)
    dst = jnp.arange(2 * FOX_QK_W)
    src_n, src_h = src[:, None] // FOX_HEADS, src[:, None] % FOX_HEADS
    dst_key, dst_lane = dst[None, :] // FOX_QK_W, dst[None, :] % FOX_QK_W
    place = ((src_n < 2 * FOX_BIAS_LANES) & (dst_key == (src_n < FOX_BIAS_LANES))
             & (dst_lane // FOX_PAD == src_h) & (dst_lane % FOX_PAD == FOX_HD + src_n)).astype(BF16)

    col = jnp.arange(S5_KW)
    src = jnp.arange(S5_BLK)
    rep_th = ((src[:, None] // S5_GROUP == col[None, :] // S5_BLK)
              & (src[:, None] % S5_GROUP == col[None, :] % S5_GROUP)).astype(BF16)
    rep_rp = ((src[:, None] // S5_STATE == col[None, :] // S5_BST)
              & (src[:, None] % S5_STATE == col[None, :] % S5_STATE)).astype(BF16)

    s5_w, s5_bst, s5_cout, s5_pw = jax.vmap(_s5_params)(s5_a_re, s5_a_im, s5_b_re, s5_b_im,
                                                        s5_c_re, s5_c_im, s5_log_dt)

    xf = x.astype(F32).reshape(n, D_MODEL)
    for l in range(depth):
        gq = _head_vec(fox_qnorm[l]) * (FOX_HD ** -0.5 * LOG2E)
        gk = _head_vec(fox_knorm[l])
        fb = jnp.pad(fox_fbias[l].astype(F32), (0, 128 - FOX_HEADS)).reshape(1, 128)
        qk_bound = 1.01 * FOX_HD ** 0.5 * jnp.max(jnp.abs(fox_qnorm[l])) * jnp.max(jnp.abs(fox_knorm[l]))
        qk_bound2 = qk_bound * LOG2E
        qoff = jnp.full((1, 128), qk_bound2 - FOX_REF_MARGIN, F32)
        q, k, v, c = _inproj_fox(xf, row(norm_mix[l]), w_fox, l, gq, gk, qone, kone, hsum,
                                 fb, qoff, place, seq, tm)
        su, hq, hf, hi, hg, gl = _inproj_rest(xf, row(norm_mix[l]), w_rest, l, tm)

        jstart = _fox_first_blocks(c, qk_bound, bsz, seq, tq)
        bounded = (qk_bound2 <= FOX_BOUNDED_MAX).astype(jnp.int32).reshape(1)
        y_fox = _fox_attention(jstart, bounded, q, k, v, bsz, seq, tq)

        y_s5 = _s5(su, s5_w[l], s5_bst[l], s5_cout[l], rep_th, rep_rp, s5_pw[l], row(s5_d[l]), s5_wglu[l],
                   row(s5_b_glu[l]), bsz, seq, s5_tt)

        lb = lb_all[l].reshape(1, -1)
        y_hg = _hgrn(hq, hf, hi, hg, jnp.log(lb), jnp.log1p(-lb), 1.0 - lb, row(hg_onorm[l]), bsz, seq, tt)

        xf = _merge(xf, y_fox, y_s5, y_hg, gl, wb, wo, l, tm)

        km, vm = _memkv(mem.astype(F32), row(norm_mem[l]), wk[l], wv[l], row(x_knorm[l]))
        xf = _xattn(xf, row(norm_x[l]), wq, row(x_qnorm[l]), km, vm, wxo, l, seq, tm)
        xf = _ffn(xf, row(norm_ffn[l]), wgu, wd, l, tm)
    return xf.reshape(bsz, seq, D_MODEL).astype(x.dtype)
```
